```python
import jax
import jax.numpy as jnp
from jax import lax
import numpy as np

D_MODEL = 2048
BATCH = 4
SEQ = 4096
DEPTH = 2

CTX_LEN = 256
GRID_W = 64
EPS = 1e-6
NEG = -1e30
N_MOD = 3
A_HEADS = 4
A_DK = 128
A_DV = 128
A_WIDTH = A_HEADS * A_DV
B_HEADS = 4
B_DH = 128
B_WIDTH = B_HEADS * B_DH
C_Q_HEADS = 8
C_KV_HEADS = 2
C_DH = 128
C_WIDTH = C_Q_HEADS * C_DH
MIX_WIDTH = A_WIDTH + B_WIDTH + C_WIDTH
CHUNK = 64
CONV_WIDTH = 3
ATTN_BLOCK = 128
WINDOW = 128
ROPE_BASE = 10000.0
ROPE_AXIS_DIM = C_DH // 2
PROJ_LAYOUT = (
    ('a_q', A_HEADS * A_DK), ('a_f_fwd', A_HEADS * A_DK), ('a_f_bwd', A_HEADS * A_DK),
    ('a_i', A_WIDTH), ('a_gate', A_WIDTH),
    ('b_q', B_WIDTH), ('b_k', B_WIDTH), ('b_v', B_WIDTH), ('b_o', B_WIDTH),
    ('b_gates', 4 * B_HEADS), ('b_z', B_WIDTH),
    ('c_q', C_WIDTH), ('c_k', C_KV_HEADS * C_DH), ('c_v', C_KV_HEADS * C_DH), ('c_z', C_WIDTH),
)
PROJ_WIDTH = 3 * A_HEADS * A_DK + 2 * A_WIDTH + 5 * B_WIDTH + 4 * B_HEADS + 2 * C_WIDTH + 2 * C_KV_HEADS * C_DH

kernel_name = 'hybrid_hgrn2_mlstm_swa_prefix_block'


def rms_norm(h, gain=None):
    h32 = h.astype(jnp.float32)
    y = h32 * lax.rsqrt(jnp.mean(h32 * h32, axis=-1, keepdims=True) + EPS)
    if gain is not None:
        y = y * gain.astype(jnp.float32)
    return y.astype(h.dtype)


def modulate(h, gain, shift, scale):
    return rms_norm(h, gain) * (1.0 + scale) + shift


def split_proj(p):
    parts, start = {}, 0
    for name, width in PROJ_LAYOUT:
        parts[name] = p[..., start:start + width]
        start += width
    return parts


def to_heads(a, n_heads):
    return a.reshape(a.shape[:-1] + (n_heads, a.shape[-1] // n_heads))


def join_dir(a_ctx, a_lat, reverse):
    if reverse:
        a_ctx, a_lat = a_ctx[:, ::-1], a_lat[:, ::-1]
    return jnp.concatenate([a_ctx, a_lat], axis=1)


def split_dir(y, n_ctx, reverse):
    y_ctx, y_lat = y[:, :n_ctx], y[:, n_ctx:]
    if reverse:
        y_ctx, y_lat = y_ctx[:, ::-1], y_lat[:, ::-1]
    return y_ctx, y_lat


def to_chunks(a):
    b, t = a.shape[:2]
    a = a.reshape((b, t // CHUNK, CHUNK) + a.shape[2:])
    return jnp.moveaxis(a, (1, 3), (0, 2))


def from_chunks(o):
    o = jnp.moveaxis(o, (0, 2), (1, 3))
    return o.reshape((o.shape[0], o.shape[1] * o.shape[2]) + o.shape[3:])


def hgrn2_scan(q, k, v, log_f):
    b, _, h, dk = q.shape
    dv = v.shape[-1]
    causal = jnp.tril(jnp.ones((CHUNK, CHUNK), bool))

    def step(state, inp):
        qc, kc, vc, fc = inp
        cum = jnp.cumsum(fc, axis=2)
        diff = cum[:, :, :, None, :] - cum[:, :, None, :, :]
        decay = jnp.exp(jnp.where(causal[:, :, None], diff, NEG))
        scores = jnp.einsum('bhtd,bhtsd,bhsd->bhts', qc, decay, kc)
        out = (jnp.einsum('bhtd,bhdv->bhtv', qc * jnp.exp(cum), state)
               + jnp.einsum('bhts,bhsv->bhtv', scores, vc))
        cum_end = cum[:, :, -1]
        state = (jnp.exp(cum_end)[..., None] * state
                 + jnp.einsum('bhsd,bhsv->bhdv', kc * jnp.exp(cum_end[:, :, None] - cum), vc))
        return state, out

    s0 = jnp.zeros((b, h, dk, dv), jnp.float32)
    _, out = lax.scan(step, s0, tuple(to_chunks(a) for a in (q, k, v, log_f)))
    return from_chunks(out)


def mlstm_scan(q, k, v, log_i, log_f):
    b, _, h, dk = q.shape
    dv = v.shape[-1]
    causal = jnp.tril(jnp.ones((CHUNK, CHUNK), bool))

    def step(carry, inp):
        c_mat, n_vec, m = carry
        qc, kc, vc, ic, fc = inp
        cum = jnp.cumsum(fc, axis=-1)
        log_d = jnp.where(causal, cum[..., :, None] - cum[..., None, :] + ic[..., None, :], NEG)
        inter = cum + m[..., None]
        m_t = jnp.maximum(inter, jnp.max(log_d, axis=-1))
        w_intra = jnp.exp(log_d - m_t[..., None])
        w_inter = jnp.exp(inter - m_t)
        s = jnp.einsum('bhtd,bhsd->bhts', qc, kc) * w_intra
        num = (w_inter[..., None] * jnp.einsum('bhtd,bhdv->bhtv', qc, c_mat)
               + jnp.einsum('bhts,bhsv->bhtv', s, vc))
        den = w_inter * jnp.einsum('bhtd,bhd->bht', qc, n_vec) + jnp.sum(s, axis=-1)
        h_out = num / jnp.maximum(jnp.abs(den), jnp.exp(-m_t))[..., None]
        log_w = cum[..., -1:] - cum + ic
        m_new = jnp.maximum(cum[..., -1] + m, jnp.max(log_w, axis=-1))
        w = jnp.exp(log_w - m_new[..., None])
        carry_scale = jnp.exp(cum[..., -1] + m - m_new)
        c_mat = carry_scale[..., None, None] * c_mat + jnp.einsum('bhs,bhsd,bhsv->bhdv', w, kc, vc)
        n_vec = carry_scale[..., None] * n_vec + jnp.einsum('bhs,bhsd->bhd', w, kc)
        return (c_mat, n_vec, m_new), h_out

    init = (jnp.zeros((b, h, dk, dv), jnp.float32), jnp.zeros((b, h, dk), jnp.float32),
            jnp.zeros((b, h), jnp.float32))
    _, out = lax.scan(step, init, tuple(to_chunks(a) for a in (q, k, v, log_i, log_f)))
    return from_chunks(out)


def short_conv(a, w):
    pad = CONV_WIDTH // 2
    t = a.shape[1]
    ap = jnp.pad(a, ((0, 0), (pad, pad), (0, 0)))
    acc = w[0] * ap[:, 0:t]
    for j in range(1, CONV_WIDTH):
        acc = acc + w[j] * ap[:, j:j + t]
    return acc


def rotate_pairs(x, cos, sin):
    x1, x2 = jnp.split(x, 2, axis=-1)
    cos = cos[None, :, None, :]
    sin = sin[None, :, None, :]
    return jnp.concatenate([x1 * cos - x2 * sin, x2 * cos + x1 * sin], axis=-1)


def axial_rope(x, rope):
    cos_r, sin_r, cos_c, sin_c = rope
    x_row, x_col = jnp.split(x.astype(jnp.float32), 2, axis=-1)
    y = jnp.concatenate([rotate_pairs(x_row, cos_r, sin_r), rotate_pairs(x_col, cos_c, sin_c)], axis=-1)
    return y.astype(x.dtype)


def window_attention(q, k, v, k_ctx, v_ctx, sink):
    b, n, hq, dh = q.shape
    hkv = k.shape[2]
    g = hq // hkv
    nb = n // ATTN_BLOCK
    span = 3 * ATTN_BLOCK
    n_ctx = k_ctx.shape[1]
    qb = q.reshape(b, nb, ATTN_BLOCK, hkv, g, dh)

    def band_windows(a):
        ap = jnp.pad(a.reshape(b, nb, ATTN_BLOCK, hkv, dh), ((0, 0), (1, 1), (0, 0), (0, 0), (0, 0)))
        return jnp.concatenate([ap[:, :-2], ap[:, 1:-1], ap[:, 2:]], axis=2)

    kw, vw = band_windows(k), band_windows(v)
    qi = jnp.arange(ATTN_BLOCK)
    kj = jnp.arange(span)
    blk = jnp.arange(nb)
    band = jnp.abs(kj[None, :] - qi[:, None] - ATTN_BLOCK) <= WINDOW
    kpos = (blk[:, None] - 1) * ATTN_BLOCK + kj[None, :]
    mask = band[None] & ((kpos >= 0) & (kpos < n))[:, None, :]
    scale = dh ** -0.5
    s_loc = jnp.einsum('bnqkgd,bnskd->bnkgqs', qb, kw, preferred_element_type=jnp.float32) * scale
    s_loc = jnp.where(mask[None, :, None, None], s_loc, NEG)
    s_ctx = jnp.einsum('bnqkgd,bckd->bnkgqc', qb, k_ctx, preferred_element_type=jnp.float32) * scale
    s_sink = jnp.broadcast_to(sink.astype(jnp.float32).reshape(1, 1, hkv, g, 1, 1), s_loc.shape[:-1] + (1,))
    p = jax.nn.softmax(jnp.concatenate([s_loc, s_ctx, s_sink], axis=-1), axis=-1).astype(v.dtype)
    out = (jnp.einsum('bnkgqs,bnskd->bnqkgd', p[..., :span], vw)
           + jnp.einsum('bnkgqc,bckd->bnqkgd', p[..., span:span + n_ctx], v_ctx))
    return out.reshape(b, n, hq * dh)


def context_attention(q, k, v, sink):
    b, n_ctx, hq, dh = q.shape
    hkv = k.shape[2]
    g = hq // hkv
    qg = q.reshape(b, n_ctx, hkv, g, dh)
    s = jnp.einsum('bqkgd,bskd->bkgqs', qg, k, preferred_element_type=jnp.float32) * dh ** -0.5
    s_sink = jnp.broadcast_to(sink.astype(jnp.float32).reshape(1, hkv, g, 1, 1), s.shape[:-1] + (1,))
    p = jax.nn.softmax(jnp.concatenate([s, s_sink], axis=-1), axis=-1).astype(v.dtype)
    out = jnp.einsum('bkgqs,bskd->bqkgd', p[..., :n_ctx], v)
    return out.reshape(b, n_ctx, hq * dh)


def hgrn2_branch(pc, pl, lower_bound, with_ctx):
    n_ctx = pc['a_q'].shape[1]
    ys_ctx, ys_lat = [], []
    for d, (f_name, reverse) in enumerate((('a_f_fwd', False), ('a_f_bwd', True))):
        q = to_heads(jax.nn.silu(join_dir(pc['a_q'], pl['a_q'], reverse).astype(jnp.float32)), A_HEADS)
        zf = to_heads(join_dir(pc[f_name], pl[f_name], reverse).astype(jnp.float32), A_HEADS)
        i_in = to_heads(join_dir(pc['a_i'], pl['a_i'], reverse).astype(jnp.float32), A_HEADS)
        lb = lower_bound[d]
        log_f = jnp.log(lb + (1.0 - lb) * jax.nn.sigmoid(zf))
        k = (1.0 - lb) * jax.nn.sigmoid(-zf)
        y_ctx, y_lat = split_dir(hgrn2_scan(q, k, i_in, log_f), n_ctx, reverse)
        ys_ctx.append(y_ctx)
        ys_lat.append(y_lat)

    def readout(y, gate):
        y = rms_norm(y).reshape(y.shape[:2] + (A_WIDTH,))
        return (y * jax.nn.silu(gate.astype(jnp.float32))).astype(gate.dtype)

    lat = readout(ys_lat[0] + ys_lat[1], pl['a_gate'])
    ctx = readout(ys_ctx[0] + ys_ctx[1], pc['a_gate']) if with_ctx else None
    return ctx, lat


def mlstm_branch(pc, pl, conv_w, gate_bias, with_ctx):
    n_ctx = pc['b_q'].shape[1]

    def qk_path(p):
        a = jax.nn.silu(short_conv(jnp.concatenate([p['b_q'], p['b_k']], axis=-1), conv_w))
        return a[..., :B_WIDTH], a[..., B_WIDTH:]

    q_c, k_c = qk_path(pc)
    q_l, k_l = qk_path(pl)
    g_c = pc['b_gates'] + gate_bias
    g_l = pl['b_gates'] + gate_bias
    ys_ctx, ys_lat = [], []
    for d, reverse in enumerate((False, True)):
        i_sl = slice((2 * d) * B_HEADS, (2 * d + 1) * B_HEADS)
        f_sl = slice((2 * d + 1) * B_HEADS, (2 * d + 2) * B_HEADS)
        q = to_heads(join_dir(q_c, q_l, reverse).astype(jnp.float32), B_HEADS) * (B_DH ** -0.5)
        k = to_heads(join_dir(k_c, k_l, reverse).astype(jnp.float32), B_HEADS)
        v = to_heads(join_dir(pc['b_v'], pl['b_v'], reverse).astype(jnp.float32), B_HEADS)
        log_i = join_dir(g_c[..., i_sl], g_l[..., i_sl], reverse).astype(jnp.float32)
        log_f = jax.nn.log_sigmoid(join_dir(g_c[..., f_sl], g_l[..., f_sl], reverse).astype(jnp.float32))
        y_ctx, y_lat = split_dir(mlstm_scan(q, k, v, log_i, log_f), n_ctx, reverse)
        ys_ctx.append(y_ctx)
        ys_lat.append(y_lat)

    def readout(y, p):
        h = jax.nn.sigmoid(to_heads(p['b_o'], B_HEADS).astype(jnp.float32)) * y
        h = rms_norm(h).reshape(h.shape[:2] + (B_WIDTH,))
        return (h * jax.nn.silu(p['b_z'].astype(jnp.float32))).astype(p['b_z'].dtype)

    lat = readout(ys_lat[0] + ys_lat[1], pl)
    ctx = readout(ys_ctx[0] + ys_ctx[1], pc) if with_ctx else None
    return ctx, lat


def attention_branch(pc, pl, sink, rope, with_ctx):
    q = axial_rope(to_heads(pl['c_q'], C_Q_HEADS), rope)
    k = axial_rope(to_heads(pl['c_k'], C_KV_HEADS), rope)
    v = to_heads(pl['c_v'], C_KV_HEADS)
    k_ctx = to_heads(pc['c_k'], C_KV_HEADS)
    v_ctx = to_heads(pc['c_v'], C_KV_HEADS)
    lat = window_attention(q, k, v, k_ctx, v_ctx, sink) * jax.nn.silu(pl['c_z'])
    ctx = None
    if with_ctx:
        ctx = context_attention(to_heads(pc['c_q'], C_Q_HEADS), k_ctx, v_ctx, sink) * jax.nn.silu(pc['c_z'])
    return ctx, lat


def setup_inputs(seed: int = 0) -> dict:
    key = jax.random.key(seed)
    ks = jax.random.split(key, 16)
    nrm = jax.random.normal
    x = nrm(ks[0], (BATCH, SEQ, D_MODEL), jnp.float32)
    c = nrm(ks[1], (BATCH, D_MODEL), jnp.float32)
    ctx = nrm(ks[2], (BATCH, CTX_LEN, D_MODEL), jnp.float32)
    c_ctx = nrm(ks[3], (D_MODEL,), jnp.float32)
    w_mod = nrm(ks[4], (DEPTH, D_MODEL, N_MOD * D_MODEL), jnp.float32) * D_MODEL ** -0.5
    b_mod = 0.02 * nrm(ks[5], (DEPTH, N_MOD * D_MODEL), jnp.float32)
    g_pre = 1.0 + 0.05 * nrm(ks[6], (DEPTH, D_MODEL), jnp.float32)
    g_post = 1.0 + 0.05 * nrm(ks[7], (DEPTH, D_MODEL), jnp.float32)
    w_in = nrm(ks[8], (DEPTH, D_MODEL, PROJ_WIDTH), jnp.float32) * D_MODEL ** -0.5
    hgrn_lb_logits = nrm(ks[9], (DEPTH, 2 * A_HEADS * A_DK), jnp.float32)
    mlstm_conv_w = nrm(ks[10], (DEPTH, CONV_WIDTH, 2 * B_WIDTH), jnp.float32) * CONV_WIDTH ** -0.5
    i_bias = 0.1 * nrm(ks[11], (DEPTH, 2, 1, B_HEADS), jnp.float32)
    f_bias = 3.0 + 3.0 * jax.random.uniform(ks[12], (DEPTH, 2, 1, B_HEADS), jnp.float32)
    mlstm_gate_bias = jnp.concatenate([i_bias, f_bias], axis=2).reshape(DEPTH, 4 * B_HEADS)
    attn_sink = nrm(ks[13], (DEPTH, C_Q_HEADS), jnp.float32)
    w_out = nrm(ks[14], (DEPTH, MIX_WIDTH, D_MODEL), jnp.float32) * MIX_WIDTH ** -0.5
    return {'x': x, 'c': c, 'ctx': ctx, 'c_ctx': c_ctx, 'w_mod': w_mod, 'b_mod': b_mod,
            'g_pre': g_pre, 'g_post': g_post, 'w_in': w_in, 'hgrn_lb_logits': hgrn_lb_logits,
            'mlstm_conv_w': mlstm_conv_w, 'mlstm_gate_bias': mlstm_gate_bias,
            'attn_sink': attn_sink, 'w_out': w_out}


def reference(x, c, ctx, c_ctx, w_mod, b_mod, g_pre, g_post, w_in, hgrn_lb_logits,
              mlstm_conv_w, mlstm_gate_bias, attn_sink, w_out):
    n_lat = x.shape[1]
    rows = n_lat // GRID_W
    row = jnp.repeat(jnp.arange(rows, dtype=jnp.float32), GRID_W)
    col = jnp.tile(jnp.arange(GRID_W, dtype=jnp.float32), rows)
    inv_freq = ROPE_BASE ** (-jnp.arange(0, ROPE_AXIS_DIM, 2, dtype=jnp.float32) / ROPE_AXIS_DIM)
    ang_r = row[:, None] * inv_freq[None, :]
    ang_c = col[:, None] * inv_freq[None, :]
    rope = (jnp.cos(ang_r), jnp.sin(ang_r), jnp.cos(ang_c), jnp.sin(ang_c))
    lb_w = jax.nn.softmax(hgrn_lb_logits.astype(jnp.float32), axis=0)
    lower_bounds = jnp.cumsum(lb_w, axis=0) - lb_w[0]
    h_ctx = ctx
    for layer in range(DEPTH):
        with_ctx = layer < DEPTH - 1
        mod_lat = (jax.nn.silu(c) @ w_mod[layer] + b_mod[layer])[:, None, :]
        mod_ctx = (jax.nn.silu(c_ctx) @ w_mod[layer] + b_mod[layer])[None, None, :]
        shift_l, scale_l, gate_l = jnp.split(mod_lat, N_MOD, axis=-1)
        shift_c, scale_c, gate_c = jnp.split(mod_ctx, N_MOD, axis=-1)
        pl = split_proj(modulate(x, g_pre[layer], shift_l, scale_l) @ w_in[layer])
        pc = split_proj(modulate(h_ctx, g_pre[layer], shift_c, scale_c) @ w_in[layer])
        a_c, a_l = hgrn2_branch(pc, pl, lower_bounds[layer].reshape(2, A_HEADS, A_DK), with_ctx)
        b_c, b_l = mlstm_branch(pc, pl, mlstm_conv_w[layer], mlstm_gate_bias[layer], with_ctx)
        c_c, c_l = attention_branch(pc, pl, attn_sink[layer], rope, with_ctx)
        y_lat = jnp.concatenate([a_l, b_l, c_l], axis=-1) @ w_out[layer]
        x = x + gate_l * rms_norm(y_lat, g_post[layer])
        if with_ctx:
            y_ctx = jnp.concatenate([a_c, b_c, c_c], axis=-1) @ w_out[layer]
            h_ctx = h_ctx + gate_c * rms_norm(y_ctx, g_post[layer])
    return x
```

```python
import functools

import numpy as np
import jax
import jax.numpy as jnp
from jax import lax
from jax.experimental import pallas as pl
from jax.experimental.pallas import tpu as pltpu

F32 = jnp.float32
BF16 = jnp.bfloat16

EPS = 1e-6
NEG = -1e30
HEAD = 128
A_HEADS = 4
B_HEADS = 4
C_Q_HEADS = 8
C_KV_HEADS = 2
C_GROUP = C_Q_HEADS // C_KV_HEADS
CHUNK = 64
ATTN_BLOCK = 128
WINDOW = 128
GRID_W = 64
ROPE_BASE = 10000.0
N_GATES = 4 * B_HEADS

COL_A_Q, COL_A_FF, COL_A_FB, COL_A_I, COL_A_GATE = 0, 4, 8, 12, 16
COL_B_Q, COL_B_K, COL_B_V, COL_B_O, COL_B_Z = 20, 24, 28, 32, 36
COL_C_Q, COL_C_K, COL_C_V, COL_C_Z = 40, 48, 50, 52
MAIN_BLOCKS = 60
MAIN_WIDTH = MAIN_BLOCKS * HEAD
GATES_OFFSET = 9 * 4 * HEAD

VMEM_LIMIT = 56 * 1024 * 1024


def _params(sem):
    return pltpu.CompilerParams(dimension_semantics=sem, vmem_limit_bytes=VMEM_LIMIT)


def _silu(x):
    return x * jax.nn.sigmoid(x)


def _log_sigmoid(x):
    return jnp.minimum(x, 0.0) - jnp.log(1.0 + jnp.exp(-jnp.abs(x)))


def _dot(a, b):
    return jnp.dot(a, b, preferred_element_type=F32)


def _dot_nt(a, b):
    return lax.dot_general(a, b, (((1,), (1,)), ((), ())), preferred_element_type=F32)


def _dot_tn(a, b):
    return lax.dot_general(a, b, (((0,), (0,)), ((), ())), preferred_element_type=F32)


def _split3(x):
    hi = x.astype(BF16)
    r = x - hi.astype(F32)
    mid = r.astype(BF16)
    lo = (r - mid.astype(F32)).astype(BF16)
    return hi, mid, lo


def _dot3(w, x):
    hi, mid, lo = _split3(x)
    return _dot(w, hi) + _dot(w, mid) + _dot(w, lo)


def _dot3_nt(w, x):
    hi, mid, lo = _split3(x)
    return _dot_nt(w, hi) + _dot_nt(w, mid) + _dot_nt(w, lo)


def _dot3_nt_lhs(x, w):
    hi, mid, lo = _split3(x)
    return _dot_nt(hi, w) + _dot_nt(mid, w) + _dot_nt(lo, w)


def _mod_kernel(c_ref, w_ref, b_ref, o_ref):
    a = _silu(c_ref[...])
    o_ref[0] = jnp.dot(a, w_ref[0], preferred_element_type=F32,
                       precision=lax.Precision.HIGHEST) + b_ref[0]


def _modulation(cond, w_mod, b_mod):
    depth, d, n = w_mod.shape
    rows = cond.shape[0]
    tn = 768
    assert n % tn == 0
    return pl.pallas_call(
        _mod_kernel,
        grid=(depth, n // tn),
        in_specs=[
            pl.BlockSpec((rows, d), lambda l, j: (0, 0)),
            pl.BlockSpec((1, d, tn), lambda l, j: (l, 0, j)),
            pl.BlockSpec((1, 1, tn), lambda l, j: (l, 0, j)),
        ],
        out_specs=pl.BlockSpec((1, rows, tn), lambda l, j: (l, 0, j)),
        out_shape=jax.ShapeDtypeStruct((depth, rows, n), F32),
        compiler_params=_params(("arbitrary", "arbitrary")),
        name="modulation",
    )(cond, w_mod, b_mod.reshape(depth, 1, n))


NORM_ROWS = 32


def _inproj_kernel(x_ref, shift_ref, scale_ref, g_ref, w_ref, wg_ref, p_ref, gates_ref, xn_ref):
    tm = x_ref.shape[0]

    @pl.when(pl.program_id(1) == 0)
    def _():
        def body(r, carry):
            rows = pl.ds(pl.multiple_of(r * NORM_ROWS, NORM_ROWS), NORM_ROWS)
            xr = x_ref[rows, :]
            ms = jnp.mean(xr * xr, axis=-1, keepdims=True)
            y = xr * lax.rsqrt(ms + EPS) * g_ref[...]
            xn_ref[rows, :] = (y * (1.0 + scale_ref[0]) + shift_ref[0]).astype(BF16)
            return carry

        lax.fori_loop(0, tm // NORM_ROWS, body, 0)
        gates_ref[...] = _dot(xn_ref[...], wg_ref[...])

    p_ref[...] = _dot(xn_ref[...], w_ref[...])


def _in_projection(x2d, shift, scale, gain, w_main, w_gates, rows_per_mod, tm, tn):
    r, d = x2d.shape
    n = w_main.shape[1]
    assert r % tm == 0 and n % tn == 0 and rows_per_mod % tm == 0
    per = rows_per_mod // tm
    return pl.pallas_call(
        _inproj_kernel,
        grid=(r // tm, n // tn),
        in_specs=[
            pl.BlockSpec((tm, d), lambda i, j: (i, 0)),
            pl.BlockSpec((1, 1, d), lambda i, j: (i // per, 0, 0)),
            pl.BlockSpec((1, 1, d), lambda i, j: (i // per, 0, 0)),
            pl.BlockSpec((1, d), lambda i, j: (0, 0)),
            pl.BlockSpec((d, tn), lambda i, j: (0, j)),
            pl.BlockSpec((d, HEAD), lambda i, j: (0, 0)),
        ],
        out_specs=[
            pl.BlockSpec((tm, tn), lambda i, j: (i, j)),
            pl.BlockSpec((tm, HEAD), lambda i, j: (i, 0)),
        ],
        out_shape=[
            jax.ShapeDtypeStruct((r, n), F32),
            jax.ShapeDtypeStruct((r, HEAD), F32),
        ],
        scratch_shapes=[pltpu.VMEM((tm, d), BF16)],
        compiler_params=_params(("arbitrary", "arbitrary")),
        name="in_projection",
    )(x2d, shift, scale, gain, w_main, w_gates)


def _outproj_kernel(a_ref, b_ref, c_ref, w_ref, x_ref, gate_ref, g_ref, o_ref, y_ref):
    tm = x_ref.shape[0]
    wa = a_ref.shape[1]
    wb = b_ref.shape[1]
    y_ref[...] = (_dot(a_ref[...], w_ref[0:wa, :])
                  + _dot(b_ref[...], w_ref[wa:wa + wb, :])
                  + _dot(c_ref[...], w_ref[wa + wb:, :]))
    def body(r, carry):
        rows = pl.ds(pl.multiple_of(r * NORM_ROWS, NORM_ROWS), NORM_ROWS)
        y = y_ref[rows, :]
        ms = jnp.mean(y * y, axis=-1, keepdims=True)
        o_ref[rows, :] = x_ref[rows, :] + gate_ref[0] * (y * lax.rsqrt(ms + EPS) * g_ref[...])
        return carry

    lax.fori_loop(0, tm // NORM_ROWS, body, 0)


def _out_projection(a, b, c, w_out, x2d, gate, gain, rows_per_mod, tm):
    r, d = x2d.shape
    assert r % tm == 0 and rows_per_mod % tm == 0
    per = rows_per_mod // tm
    return pl.pallas_call(
        _outproj_kernel,
        grid=(r // tm,),
        in_specs=[
            pl.BlockSpec((tm, a.shape[1]), lambda i: (i, 0)),
            pl.BlockSpec((tm, b.shape[1]), lambda i: (i, 0)),
            pl.BlockSpec((tm, c.shape[1]), lambda i: (i, 0)),
            pl.BlockSpec(w_out.shape, lambda i: (0, 0)),
            pl.BlockSpec((tm, d), lambda i: (i, 0)),
            pl.BlockSpec((1, 1, d), lambda i: (i // per, 0, 0)),
            pl.BlockSpec((1, d), lambda i: (0, 0)),
        ],
        out_specs=pl.BlockSpec((tm, d), lambda i: (i, 0)),
        out_shape=jax.ShapeDtypeStruct((r, d), F32),
        scratch_shapes=[pltpu.VMEM((tm, d), F32)],
        compiler_params=_params(("arbitrary",)),
        name="out_projection",
    )(a, b, c, w_out, x2d, gate, gain)


N_LEVELS = 7


def _scan_matrices(reverse):
    n = CHUNK
    t = np.arange(n)[:, None]
    u = np.arange(n)[None, :]
    if reverse:
        mats = [u >= t, u < t]
    else:
        mats = [u <= t, u > t]
    masks = [t == u]
    size = 2
    while size <= n:
        half = size // 2
        base = (t // size) * size
        same = (t // size) == (u // size)
        if reverse:
            m0 = base + half
            query_side = (t - base) < half
            w = np.where(query_side, (u >= t) & (u < m0), (u >= m0) & (u < t))
            m = same & ((t % size) < half) & ((u % size) >= half)
        else:
            mid = base + half - 1
            query_side = (t - base) >= half
            w = np.where(query_side, (u > mid) & (u <= t), (u > t) & (u <= mid))
            m = same & ((t % size) >= half) & ((u % size) < half)
        mats.append(w)
        masks.append(m)
        size *= 2
    mats.append(np.ones((8, n), bool))
    w_all = np.concatenate(mats, axis=0).astype(np.float32)
    return w_all, np.stack(masks).astype(np.float32)


def _scan_constants():
    wf, mf = _scan_matrices(False)
    wb, mb = _scan_matrices(True)
    return (jnp.asarray(np.stack([wf, wb]), BF16), jnp.asarray(np.stack([mf, mb]), F32))


def _scan_row(j, reverse, n_ctx_chunks, n_chunks):
    if not reverse:
        return j * CHUNK
    ctx_part = (n_ctx_chunks - 1 - j) * CHUNK
    lat_part = (n_ctx_chunks + (n_chunks - 1 - j)) * CHUNK
    return jnp.where(j < n_ctx_chunks, ctx_part, lat_part)


def _hgrn2_kernel(lbl_ref, wm_ref, mk_ref,
                  qc_ref, ffc_ref, fbc_ref, ic_ref, gc_ref,
                  ql_ref, ffl_ref, fbl_ref, il_ref, gl_ref,
                  oc_ref, ol_ref,
                  q_s, zf_s, zb_s, v_s, yf_s, yb_s, st_s, *, layer):
    n_ctx = qc_ref.shape[1]
    n_lat = ql_ref.shape[1]
    n_ctx_chunks = n_ctx // CHUNK
    n_chunks = (n_ctx + n_lat) // CHUNK

    logits = lbl_ref[...]
    e = jnp.exp(logits - jnp.max(logits, axis=0, keepdims=True))
    sm = e / jnp.sum(e, axis=0, keepdims=True)
    lbs = jnp.sum(sm[:layer + 1], axis=0) - sm[0]

    def stage(src, dst, off, n, fn):
        def body(c, carry):
            rows = pl.ds(pl.multiple_of(c * CHUNK, CHUNK), CHUNK)
            dst[pl.ds(pl.multiple_of(off + c * CHUNK, CHUNK), CHUNK), :] = fn(src[0, rows, :])
            return carry
        lax.fori_loop(0, n // CHUNK, body, 0)

    ident = lambda a: a
    for src_c, src_l, dst, fn in ((qc_ref, ql_ref, q_s, _silu), (ffc_ref, ffl_ref, zf_s, ident),
                                  (fbc_ref, fbl_ref, zb_s, ident), (ic_ref, il_ref, v_s, ident)):
        stage(src_c, dst, 0, n_ctx, fn)
        stage(src_l, dst, n_ctx, n_lat, fn)

    st_s[...] = jnp.zeros_like(st_s)

    def direction(j, d, z_s, y_s):
        r0 = pl.multiple_of(_scan_row(j, d == 1, n_ctx_chunks, n_chunks), CHUNK)
        rows = pl.ds(r0, CHUNK)
        lb = lbs[d]
        q = q_s[rows, :]
        z = z_s[rows, :]
        v = v_s[rows, :].astype(BF16)
        logf = jnp.log(lb + (1.0 - lb) * jax.nn.sigmoid(z))
        k = (1.0 - lb) * jax.nn.sigmoid(-z)
        ex = _dot3(wm_ref[d], logf)
        cum = ex[0:CHUNK]
        tail = ex[CHUNK:2 * CHUNK]
        total = ex[(N_LEVELS + 1) * CHUNK:(N_LEVELS + 1) * CHUNK + 1]
        scores = mk_ref[d, 0] * _dot_nt(q.astype(BF16), k.astype(BF16))
        for lvl in range(1, N_LEVELS):
            x = jnp.exp(ex[(lvl + 1) * CHUNK:(lvl + 2) * CHUNK])
            scores = scores + mk_ref[d, lvl] * _dot_nt((q * x).astype(BF16), (k * x).astype(BF16))
        state = st_s[d]
        out = (_dot_nt((q * jnp.exp(cum)).astype(BF16), state.astype(BF16))
               + _dot(scores.astype(BF16), v))
        y_s[rows, :] = out
        kd = (k * jnp.exp(tail)).astype(BF16)
        st_s[d] = state * jnp.exp(total) + _dot_tn(v, kd)

    def step(j, carry):
        direction(j, 0, zf_s, yf_s)
        direction(j, 1, zb_s, yb_s)
        return carry

    lax.fori_loop(0, n_chunks, step, 0)

    def readout(gate_ref, o_ref, off, n):
        def body(c, carry):
            rows = pl.ds(pl.multiple_of(c * CHUNK, CHUNK), CHUNK)
            srows = pl.ds(pl.multiple_of(off + c * CHUNK, CHUNK), CHUNK)
            y = yf_s[srows, :] + yb_s[srows, :]
            ms = jnp.mean(y * y, axis=-1, keepdims=True)
            o_ref[0, rows, :] = (y * lax.rsqrt(ms + EPS) * _silu(gate_ref[0, rows, :])).astype(o_ref.dtype)
            return carry
        lax.fori_loop(0, n // CHUNK, body, 0)

    readout(gc_ref, oc_ref, 0, n_ctx)
    readout(gl_ref, ol_ref, n_ctx, n_lat)


def _hgrn2(p_ctx, p_lat, lb_logits, consts, layer):
    bsz, n_ctx, _ = p_ctx.shape
    n_lat = p_lat.shape[1]
    t = n_ctx + n_lat
    wm, mk = consts
    depth = lb_logits.shape[0]
    lbl = lb_logits.reshape(depth, 2, A_HEADS, 1, HEAD).transpose(0, 1, 3, 2, 4).reshape(depth, 2, 1, A_HEADS * HEAD)

    def col(n, c0):
        return pl.BlockSpec((1, n, HEAD), lambda b, h, c0=c0: (b, 0, c0 + h))

    cols = (COL_A_Q, COL_A_FF, COL_A_FB, COL_A_I, COL_A_GATE)
    return pl.pallas_call(
        functools.partial(_hgrn2_kernel, layer=layer),
        grid=(bsz, A_HEADS),
        in_specs=[
            pl.BlockSpec((depth, 2, 1, HEAD), lambda b, h: (0, 0, 0, h)),
            pl.BlockSpec(wm.shape, lambda b, h: (0, 0, 0)),
            pl.BlockSpec(mk.shape, lambda b, h: (0, 0, 0, 0)),
        ] + [col(n_ctx, c) for c in cols] + [col(n_lat, c) for c in cols],
        out_specs=[
            pl.BlockSpec((1, n_ctx, HEAD), lambda b, h: (b, 0, h)),
            pl.BlockSpec((1, n_lat, HEAD), lambda b, h: (b, 0, h)),
        ],
        out_shape=[
            jax.ShapeDtypeStruct((bsz, n_ctx, A_HEADS * HEAD), BF16),
            jax.ShapeDtypeStruct((bsz, n_lat, A_HEADS * HEAD), BF16),
        ],
        scratch_shapes=[pltpu.VMEM((t, HEAD), F32)] * 6 + [pltpu.VMEM((2, HEAD, HEAD), F32)],
        compiler_params=_params(("arbitrary", "arbitrary")),
        name="hgrn2",
    )(lbl, wm, mk, *([p_ctx] * 5), *([p_lat] * 5))


def _mlstm_kernel(wm_ref, mk_ref, eye_ref, cw_ref, bias_ref,
                  qc_ref, kc_ref, vc_ref, oc_ref, zc_ref, gc_ref,
                  ql_ref, kl_ref, vl_ref, ol_ref, zl_ref, gl_ref,
                  outc_ref, outl_ref,
                  q_s, k_s, v_s, g_s, yf_s, yb_s, c_s):
    h = pl.program_id(1)
    n_ctx = qc_ref.shape[1]
    n_lat = ql_ref.shape[1]
    n_ctx_chunks = n_ctx // CHUNK
    n_chunks = (n_ctx + n_lat) // CHUNK
    row_id = lax.broadcasted_iota(jnp.int32, (CHUNK, 1), 0)

    def conv_stage(src, dst, off, n, w, post_scale):
        def body(c, carry):
            s = pl.multiple_of(c * CHUNK, CHUNK)
            cur = src[0, pl.ds(s, CHUNK), :]
            up = src[0, pl.ds(pl.multiple_of(jnp.maximum(s - 8, 0), 8), 8), :]
            dn = src[0, pl.ds(pl.multiple_of(jnp.minimum(s + CHUNK, n - 8), 8), 8), :]
            prev_row = jnp.where(s > 0, up[7:8, :], 0.0)
            next_row = jnp.where(s + CHUNK < n, dn[0:1, :], 0.0)
            prev = jnp.where(row_id == 0, prev_row, pltpu.roll(cur, 1, 0))
            nxt = jnp.where(row_id == CHUNK - 1, next_row, pltpu.roll(cur, CHUNK - 1, 0))
            acc = w[0:1, :] * prev + w[1:2, :] * cur + w[2:3, :] * nxt
            dst[pl.ds(pl.multiple_of(off + s, CHUNK), CHUNK), :] = _silu(acc) * post_scale
            return carry
        lax.fori_loop(0, n // CHUNK, body, 0)

    def copy_stage(src, dst, off, n, add):
        def body(c, carry):
            s = pl.multiple_of(c * CHUNK, CHUNK)
            dst[pl.ds(pl.multiple_of(off + s, CHUNK), CHUNK), :] = src[0, pl.ds(s, CHUNK), :] + add
            return carry
        lax.fori_loop(0, n // CHUNK, body, 0)

    wq = cw_ref[0]
    wk = cw_ref[1]
    bias = bias_ref[...]
    for (src_c, src_l, dst, w, sc) in ((qc_ref, ql_ref, q_s, wq, HEAD ** -0.5), (kc_ref, kl_ref, k_s, wk, 1.0)):
        conv_stage(src_c, dst, 0, n_ctx, w, sc)
        conv_stage(src_l, dst, n_ctx, n_lat, w, sc)
    copy_stage(vc_ref, v_s, 0, n_ctx, 0.0)
    copy_stage(vl_ref, v_s, n_ctx, n_lat, 0.0)
    copy_stage(gc_ref, g_s, 0, n_ctx, bias)
    copy_stage(gl_ref, g_s, n_ctx, n_lat, bias)

    c_s[...] = jnp.zeros_like(c_s)
    lane = lax.broadcasted_iota(jnp.int32, (1, HEAD), 1)
    sub = lax.broadcasted_iota(jnp.int32, (N_GATES, 1), 0)

    def pick_col(a, idx):
        return jnp.sum(jnp.where(lane == idx, a, 0.0), axis=1, keepdims=True)

    def pick_row(a, idx):
        return jnp.sum(jnp.where(sub == idx, a, 0.0), axis=0, keepdims=True)

    def direction(j, d, y_s, m, nvec):
        r0 = pl.multiple_of(_scan_row(j, d == 1, n_ctx_chunks, n_chunks), CHUNK)
        rows = pl.ds(r0, CHUNK)
        q = q_s[rows, :]
        k = k_s[rows, :]
        v = v_s[rows, :].astype(BF16)
        g = g_s[rows, :]
        ci = 2 * B_HEADS * d + h
        cf = ci + B_HEADS
        wm = wm_ref[d]
        ex = _dot3(wm, _log_sigmoid(g))
        cum_col = pick_col(ex[0:CHUNK], cf)
        tail_col = pick_col(ex[CHUNK:2 * CHUNK], cf)
        total = pick_col(ex[2 * CHUNK:2 * CHUNK + 1], cf)
        li_col = pick_col(g, ci)
        gt = _dot3_nt(eye_ref[...], g)
        cum_t = _dot3_nt_lhs(_log_sigmoid(gt), wm[0:CHUNK])
        li_row = pick_row(gt, ci)
        cum_row = pick_row(cum_t, cf)
        causal = mk_ref[d] > 0.0
        log_d = jnp.where(causal, cum_col + (li_row - cum_row), NEG)
        inter = cum_col + m
        m_t = jnp.maximum(inter, jnp.max(log_d, axis=1, keepdims=True))
        w_intra = jnp.exp(log_d - m_t)
        w_inter = jnp.exp(inter - m_t)
        qb = q.astype(BF16)
        s = _dot_nt(qb, k.astype(BF16)) * w_intra
        cmat = c_s[d]
        num = w_inter * _dot(qb, cmat.astype(BF16)) + _dot(s.astype(BF16), v)
        den = w_inter * jnp.sum(q * nvec, axis=1, keepdims=True) + jnp.sum(s, axis=1, keepdims=True)
        y_s[rows, :] = num / jnp.maximum(jnp.abs(den), jnp.exp(-m_t))
        log_w = tail_col + li_col
        m_new = jnp.maximum(total + m, jnp.max(log_w, axis=0, keepdims=True))
        kw = k * jnp.exp(log_w - m_new)
        scale = jnp.exp(total + m - m_new)
        c_s[d] = scale * cmat + _dot_tn(kw.astype(BF16), v)
        return m_new, scale * nvec + jnp.sum(kw, axis=0, keepdims=True)

    def step(j, carry):
        mf, nf, mb, nb = carry
        mf, nf = direction(j, 0, yf_s, mf, nf)
        mb, nb = direction(j, 1, yb_s, mb, nb)
        return mf, nf, mb, nb

    zero1 = jnp.zeros((1, 1), F32)
    zeron = jnp.zeros((1, HEAD), F32)
    lax.fori_loop(0, n_chunks, step, (zero1, zeron, zero1, zeron))

    def readout(o_ref, z_ref, out_ref, off, n):
        def body(c, carry):
            rows = pl.ds(pl.multiple_of(c * CHUNK, CHUNK), CHUNK)
            srows = pl.ds(pl.multiple_of(off + c * CHUNK, CHUNK), CHUNK)
            hh = jax.nn.sigmoid(o_ref[0, rows, :]) * (yf_s[srows, :] + yb_s[srows, :])
            ms = jnp.mean(hh * hh, axis=-1, keepdims=True)
            out_ref[0, rows, :] = (hh * lax.rsqrt(ms + EPS) * _silu(z_ref[0, rows, :])).astype(out_ref.dtype)
            return carry
        lax.fori_loop(0, n // CHUNK, body, 0)

    readout(oc_ref, zc_ref, outc_ref, 0, n_ctx)
    readout(ol_ref, zl_ref, outl_ref, n_ctx, n_lat)


def _mlstm(p_ctx, p_lat, g_ctx, g_lat, conv_w, gate_bias, consts):
    bsz, n_ctx, _ = p_ctx.shape
    n_lat = p_lat.shape[1]
    t = n_ctx + n_lat
    wm_all, mk = consts
    wm = jnp.concatenate([wm_all[:, :2 * CHUNK], wm_all[:, -8:]], axis=1)
    causal = jnp.sum(mk, axis=1)
    eye = jnp.eye(N_GATES, HEAD, dtype=BF16)
    cw = conv_w.reshape(3, 2, B_HEADS * HEAD).transpose(1, 0, 2)
    bias = jnp.pad(gate_bias.reshape(1, N_GATES), ((0, 0), (0, HEAD - N_GATES)))

    def col(n, c0):
        return pl.BlockSpec((1, n, HEAD), lambda b, h, c0=c0: (b, 0, c0 + h))

    def gate_spec(n):
        return pl.BlockSpec((1, n, HEAD), lambda b, h: (b, 0, 0))

    cols = (COL_B_Q, COL_B_K, COL_B_V, COL_B_O, COL_B_Z)
    return pl.pallas_call(
        _mlstm_kernel,
        grid=(bsz, B_HEADS),
        in_specs=[
            pl.BlockSpec(wm.shape, lambda b, h: (0, 0, 0)),
            pl.BlockSpec(causal.shape, lambda b, h: (0, 0, 0)),
            pl.BlockSpec(eye.shape, lambda b, h: (0, 0)),
            pl.BlockSpec((2, 3, HEAD), lambda b, h: (0, 0, h)),
            pl.BlockSpec((1, HEAD), lambda b, h: (0, 0)),
        ] + [col(n_ctx, c) for c in cols] + [gate_spec(n_ctx)]
          + [col(n_lat, c) for c in cols] + [gate_spec(n_lat)],
        out_specs=[
            pl.BlockSpec((1, n_ctx, HEAD), lambda b, h: (b, 0, h)),
            pl.BlockSpec((1, n_lat, HEAD), lambda b, h: (b, 0, h)),
        ],
        out_shape=[
            jax.ShapeDtypeStruct((bsz, n_ctx, B_HEADS * HEAD), BF16),
            jax.ShapeDtypeStruct((bsz, n_lat, B_HEADS * HEAD), BF16),
        ],
        scratch_shapes=[pltpu.VMEM((t, HEAD), F32)] * 6 + [pltpu.VMEM((2, HEAD, HEAD), F32)],
        compiler_params=_params(("arbitrary", "arbitrary")),
        name="mlstm",
    )(wm, causal, eye, cw, bias, *([p_ctx] * 5), g_ctx, *([p_lat] * 5), g_lat)


def _rope(x, cos, sin_lo, sin_hi):
    return x * cos + pltpu.roll(x, HEAD - 32, 1) * sin_lo + pltpu.roll(x, 32, 1) * sin_hi


def _softmax_attend(sink_col, parts):
    m = sink_col
    for s, _ in parts:
        m = jnp.maximum(m, jnp.max(s, axis=1, keepdims=True))
    den = jnp.exp(sink_col - m)
    acc = None
    for s, v in parts:
        p = jnp.exp(s - m)
        den = den + jnp.sum(p, axis=1, keepdims=True)
        pv = _dot(p.astype(BF16), v)
        acc = pv if acc is None else acc + pv
    return acc / den


def _sink_column(sink_ref, kv, rows_per_head):
    row = lax.broadcasted_iota(jnp.int32, (C_GROUP * rows_per_head, 1), 0)
    col = jnp.zeros((C_GROUP * rows_per_head, 1), F32)
    for g in range(C_GROUP):
        col = jnp.where(row // rows_per_head == g, sink_ref[kv * C_GROUP + g], col)
    return col


def _attn_kernel(sink_ref, cos_ref, slo_ref, shi_ref, q_ref, k_ref, v_ref, kc_ref, vc_ref, z_ref,
                 o_ref, kr_s):
    kv = pl.program_id(1)
    i = pl.program_id(2)
    n = k_ref.shape[1]
    span = 3 * ATTN_BLOCK

    @pl.when(i == 0)
    def _():
        def body(c, carry):
            rows = pl.ds(pl.multiple_of(c * ATTN_BLOCK, ATTN_BLOCK), ATTN_BLOCK)
            kr_s[rows, :] = _rope(k_ref[0, rows, :], cos_ref[rows, :], slo_ref[rows, :],
                                  shi_ref[rows, :]).astype(BF16)
            return carry
        lax.fori_loop(0, n // ATTN_BLOCK, body, 0)

    qrows = pl.ds(pl.multiple_of(i * ATTN_BLOCK, ATTN_BLOCK), ATTN_BLOCK)
    cos = cos_ref[qrows, :]
    slo = slo_ref[qrows, :]
    shi = shi_ref[qrows, :]
    scale = HEAD ** -0.5
    q4 = jnp.concatenate(
        [_rope(q_ref[0, :, g * HEAD:(g + 1) * HEAD], cos, slo, shi) * scale for g in range(C_GROUP)],
        axis=0).astype(BF16)
    start = pl.multiple_of(jnp.clip((i - 1) * ATTN_BLOCK, 0, n - span), ATTN_BLOCK)
    kw = kr_s[pl.ds(start, span), :]
    vw = v_ref[0, pl.ds(start, span), :].astype(BF16)
    s_loc = _dot_nt(q4, kw)
    rows_all = C_GROUP * ATTN_BLOCK
    qpos = i * ATTN_BLOCK + lax.broadcasted_iota(jnp.int32, (rows_all, 1), 0) % ATTN_BLOCK
    kpos = start + lax.broadcasted_iota(jnp.int32, (1, span), 1)
    s_loc = jnp.where(jnp.abs(kpos - qpos) <= WINDOW, s_loc, NEG)
    s_ctx = _dot_nt(q4, kc_ref[0].astype(BF16))
    out = _softmax_attend(_sink_column(sink_ref, kv, ATTN_BLOCK),
                          [(s_loc, vw), (s_ctx, vc_ref[0].astype(BF16))])
    for g in range(C_GROUP):
        zg = z_ref[0, :, g * HEAD:(g + 1) * HEAD]
        o_ref[0, :, g * HEAD:(g + 1) * HEAD] = (out[g * ATTN_BLOCK:(g + 1) * ATTN_BLOCK] * _silu(zg)).astype(o_ref.dtype)


def _attention(p_ctx, p_lat, sink, rope):
    bsz, n_ctx, _ = p_ctx.shape
    n = p_lat.shape[1]
    cos, slo, shi = rope
    gw = C_GROUP * HEAD
    gblk = C_GROUP

    full = pl.BlockSpec((n, HEAD), lambda b, kv, i: (0, 0))
    return pl.pallas_call(
        _attn_kernel,
        grid=(bsz, C_KV_HEADS, n // ATTN_BLOCK),
        in_specs=[
            pl.BlockSpec(memory_space=pltpu.SMEM),
            full, full, full,
            pl.BlockSpec((1, ATTN_BLOCK, gw), lambda b, kv, i: (b, i, COL_C_Q // gblk + kv)),
            pl.BlockSpec((1, n, HEAD), lambda b, kv, i: (b, 0, COL_C_K + kv)),
            pl.BlockSpec((1, n, HEAD), lambda b, kv, i: (b, 0, COL_C_V + kv)),
            pl.BlockSpec((1, n_ctx, HEAD), lambda b, kv, i: (b, 0, COL_C_K + kv)),
            pl.BlockSpec((1, n_ctx, HEAD), lambda b, kv, i: (b, 0, COL_C_V + kv)),
            pl.BlockSpec((1, ATTN_BLOCK, gw), lambda b, kv, i: (b, i, COL_C_Z // gblk + kv)),
        ],
        out_specs=pl.BlockSpec((1, ATTN_BLOCK, gw), lambda b, kv, i: (b, i, kv)),
        out_shape=jax.ShapeDtypeStruct((bsz, n, C_Q_HEADS * HEAD), BF16),
        scratch_shapes=[pltpu.VMEM((n, HEAD), BF16)],
        compiler_params=_params(("arbitrary", "arbitrary", "arbitrary")),
        name="window_attention",
    )(sink, cos, slo, shi, p_lat, p_lat, p_lat, p_ctx, p_ctx, p_lat)


def _ctx_attn_kernel(sink_ref, q_ref, k_ref, v_ref, z_ref, o_ref):
    kv = pl.program_id(1)
    n_ctx = q_ref.shape[1]
    scale = HEAD ** -0.5
    q4 = jnp.concatenate([q_ref[0, :, g * HEAD:(g + 1) * HEAD] * scale for g in range(C_GROUP)],
                         axis=0).astype(BF16)
    s = _dot_nt(q4, k_ref[0].astype(BF16))
    out = _softmax_attend(_sink_column(sink_ref, kv, n_ctx), [(s, v_ref[0].astype(BF16))])
    for g in range(C_GROUP):
        zg = z_ref[0, :, g * HEAD:(g + 1) * HEAD]
        o_ref[0, :, g * HEAD:(g + 1) * HEAD] = (out[g * n_ctx:(g + 1) * n_ctx] * _silu(zg)).astype(o_ref.dtype)


def _context_attention(p_ctx, sink):
    bsz, n_ctx, _ = p_ctx.shape
    gw = C_GROUP * HEAD
    gblk = C_GROUP
    return pl.pallas_call(
        _ctx_attn_kernel,
        grid=(bsz, C_KV_HEADS),
        in_specs=[
            pl.BlockSpec(memory_space=pltpu.SMEM),
            pl.BlockSpec((1, n_ctx, gw), lambda b, kv: (b, 0, COL_C_Q // gblk + kv)),
            pl.BlockSpec((1, n_ctx, HEAD), lambda b, kv: (b, 0, COL_C_K + kv)),
            pl.BlockSpec((1, n_ctx, HEAD), lambda b, kv: (b, 0, COL_C_V + kv)),
            pl.BlockSpec((1, n_ctx, gw), lambda b, kv: (b, 0, COL_C_Z // gblk + kv)),
        ],
        out_specs=pl.BlockSpec((1, n_ctx, gw), lambda b, kv: (b, 0, kv)),
        out_shape=jax.ShapeDtypeStruct((bsz, n_ctx, C_Q_HEADS * HEAD), BF16),
        compiler_params=_params(("arbitrary", "arbitrary")),
        name="context_attention",
    )(sink, p_ctx, p_ctx, p_ctx, p_ctx)


def _rope_tables(n):
    axis_dim = HEAD // 2
    rows = n // GRID_W
    row = jnp.repeat(jnp.arange(rows, dtype=F32), GRID_W)
    col = jnp.tile(jnp.arange(GRID_W, dtype=F32), rows)
    inv_freq = ROPE_BASE ** (-jnp.arange(0, axis_dim, 2, dtype=F32) / axis_dim)
    ang_r = row[:, None] * inv_freq[None, :]
    ang_c = col[:, None] * inv_freq[None, :]
    zero = jnp.zeros_like(ang_r)
    cos = jnp.concatenate([jnp.cos(ang_r)] * 2 + [jnp.cos(ang_c)] * 2, axis=-1)
    sin_lo = jnp.concatenate([-jnp.sin(ang_r), zero, -jnp.sin(ang_c), zero], axis=-1)
    sin_hi = jnp.concatenate([zero, jnp.sin(ang_r), zero, jnp.sin(ang_c)], axis=-1)
    return cos, sin_lo, sin_hi


def _pick_tile(n, candidates):
    for c in candidates:
        if n % c == 0:
            return c
    raise ValueError(f"no tile for {n}")


def kernel(x, c, ctx, c_ctx, w_mod, b_mod, g_pre, g_post, w_in, hgrn_lb_logits, mlstm_conv_w,
           mlstm_gate_bias, attn_sink, w_out):
    bsz, n_lat, d = x.shape
    n_ctx = ctx.shape[1]
    depth = w_mod.shape[0]
    assert n_ctx % CHUNK == 0 and n_lat % ATTN_BLOCK == 0 and n_lat >= 3 * ATTN_BLOCK

    cond = jnp.concatenate([c, c_ctx[None, :]], axis=0)
    pad = (-cond.shape[0]) % 8
    cond = jnp.pad(cond, ((0, pad), (0, 0)))
    mod = _modulation(cond, w_mod, b_mod)

    consts = _scan_constants()
    rope = _rope_tables(n_lat)

    tm_lat = _pick_tile(n_lat, (1024, 512, 256, 128))
    tm_ctx = _pick_tile(bsz * n_ctx, (1024, 512, 256, 128))
    tn = _pick_tile(MAIN_WIDTH, (1280, 768, 512))
    to_lat = _pick_tile(n_lat, (512, 256, 128))
    to_ctx = _pick_tile(bsz * n_ctx, (512, 256, 128))

    x2d = x.reshape(bsz * n_lat, d)
    h2d = ctx.reshape(bsz * n_ctx, d)
    for layer in range(depth):
        with_ctx = layer < depth - 1
        shift_l = mod[layer, :bsz, 0:d].reshape(bsz, 1, d)
        scale_l = mod[layer, :bsz, d:2 * d].reshape(bsz, 1, d)
        gate_l = mod[layer, :bsz, 2 * d:].reshape(bsz, 1, d)
        shift_c = mod[layer, bsz:bsz + 1, 0:d].reshape(1, 1, d)
        scale_c = mod[layer, bsz:bsz + 1, d:2 * d].reshape(1, 1, d)
        gate_c = mod[layer, bsz:bsz + 1, 2 * d:].reshape(1, 1, d)

        w = w_in[layer]
        w_main = jnp.concatenate([w[:, :GATES_OFFSET], w[:, GATES_OFFSET + N_GATES:]], axis=1).astype(BF16)
        w_gates = jnp.pad(w[:, GATES_OFFSET:GATES_OFFSET + N_GATES], ((0, 0), (0, HEAD - N_GATES))).astype(BF16)
        gain_pre = g_pre[layer].reshape(1, d)
        gain_post = g_post[layer].reshape(1, d)
        wo = w_out[layer].astype(BF16)

        p_lat, g_lat = _in_projection(x2d, shift_l, scale_l, gain_pre, w_main, w_gates, n_lat, tm_lat, tn)
        p_ctx, g_ctx = _in_projection(h2d, shift_c, scale_c, gain_pre, w_main, w_gates, bsz * n_ctx, tm_ctx, tn)
        p_lat = p_lat.reshape(bsz, n_lat, MAIN_WIDTH)
        p_ctx = p_ctx.reshape(bsz, n_ctx, MAIN_WIDTH)
        g_lat = g_lat.reshape(bsz, n_lat, HEAD)
        g_ctx = g_ctx.reshape(bsz, n_ctx, HEAD)

        a_c, a_l = _hgrn2(p_ctx, p_lat, hgrn_lb_logits, consts, layer)
        b_c, b_l = _mlstm(p_ctx, p_lat, g_ctx, g_lat, mlstm_conv_w[layer], mlstm_gate_bias[layer], consts)
        c_l = _attention(p_ctx, p_lat, attn_sink[layer], rope)

        x2d = _out_projection(a_l.reshape(bsz * n_lat, -1), b_l.reshape(bsz * n_lat, -1),
                              c_l.reshape(bsz * n_lat, -1), wo, x2d, gate_l, gain_post, n_lat, to_lat)
        if with_ctx:
            c_c = _context_attention(p_ctx, attn_sink[layer])
            h2d = _out_projection(a_c.reshape(bsz * n_ctx, -1), b_c.reshape(bsz * n_ctx, -1),
                                  c_c.reshape(bsz * n_ctx, -1), wo, h2d, gate_c, gain_post,
                                  bsz * n_ctx, to_ctx)
    return x2d.reshape(bsz, n_lat, d)
```

```python
import functools

import numpy as np
import jax
import jax.numpy as jnp
from jax import lax
from jax.experimental import pallas as pl
from jax.experimental.pallas import tpu as pltpu

F32 = jnp.float32
BF16 = jnp.bfloat16

EPS = 1e-6
NEG = -1e30
HEAD = 128
A_HEADS = 4
B_HEADS = 4
C_Q_HEADS = 8
C_KV_HEADS = 2
C_GROUP = C_Q_HEADS // C_KV_HEADS
CHUNK = 64
ATTN_BLOCK = 128
WINDOW = 128
GRID_W = 64
ROPE_BASE = 10000.0
N_GATES = 4 * B_HEADS

COL_A_Q, COL_A_FF, COL_A_FB, COL_A_I, COL_A_GATE = 0, 4, 8, 12, 16
COL_B_Q, COL_B_K, COL_B_V, COL_B_O, COL_B_Z = 20, 24, 28, 32, 36
COL_C_Q, COL_C_K, COL_C_V, COL_C_Z = 40, 48, 50, 52
MAIN_BLOCKS = 60
MAIN_WIDTH = MAIN_BLOCKS * HEAD
GATES_OFFSET = 9 * 4 * HEAD

VMEM_LIMIT = 56 * 1024 * 1024


def _params(sem):
    return pltpu.CompilerParams(dimension_semantics=sem, vmem_limit_bytes=VMEM_LIMIT)


def _silu(x):
    return x * jax.nn.sigmoid(x)


def _log_sigmoid(x):
    return jnp.minimum(x, 0.0) - jnp.log(1.0 + jnp.exp(-jnp.abs(x)))


def _dot(a, b):
    return jnp.dot(a, b, preferred_element_type=F32)


def _dot_nt(a, b):
    return lax.dot_general(a, b, (((1,), (1,)), ((), ())), preferred_element_type=F32)


def _dot_tn(a, b):
    return lax.dot_general(a, b, (((0,), (0,)), ((), ())), preferred_element_type=F32)


def _split3(x):
    hi = x.astype(BF16)
    r = x - hi.astype(F32)
    mid = r.astype(BF16)
    lo = (r - mid.astype(F32)).astype(BF16)
    return hi, mid, lo


def _dot3_nt(w, x):
    hi, mid, lo = _split3(x)
    return _dot_nt(w, hi) + _dot_nt(w, mid) + _dot_nt(w, lo)


def _mod_kernel(c_ref, w_ref, b_ref, o_ref):
    a = _silu(c_ref[...])
    o_ref[0] = jnp.dot(a, w_ref[0], preferred_element_type=F32,
                       precision=lax.Precision.HIGHEST) + b_ref[0]


def _modulation(cond, w_mod, b_mod):
    depth, d, n = w_mod.shape
    rows = cond.shape[0]
    tn = 768
    assert n % tn == 0
    return pl.pallas_call(
        _mod_kernel,
        grid=(depth, n // tn),
        in_specs=[
            pl.BlockSpec((rows, d), lambda l, j: (0, 0)),
            pl.BlockSpec((1, d, tn), lambda l, j: (l, 0, j)),
            pl.BlockSpec((1, 1, tn), lambda l, j: (l, 0, j)),
        ],
        out_specs=pl.BlockSpec((1, rows, tn), lambda l, j: (l, 0, j)),
        out_shape=jax.ShapeDtypeStruct((depth, rows, n), F32),
        compiler_params=_params(("arbitrary", "arbitrary")),
        name="modulation",
    )(cond, w_mod, b_mod.reshape(depth, 1, n))


NORM_ROWS = 32


def _inproj_kernel(x_ref, shift_ref, scale_ref, g_ref, w_ref, wg_ref, p_ref, gates_ref, xn_ref):
    tm = x_ref.shape[0]

    @pl.when(pl.program_id(1) == 0)
    def _():
        def body(r, carry):
            rows = pl.ds(pl.multiple_of(r * NORM_ROWS, NORM_ROWS), NORM_ROWS)
            xr = x_ref[rows, :]
            ms = jnp.mean(xr * xr, axis=-1, keepdims=True)
            y = xr * lax.rsqrt(ms + EPS) * g_ref[...]
            xn_ref[rows, :] = (y * (1.0 + scale_ref[0]) + shift_ref[0]).astype(BF16)
            return carry

        lax.fori_loop(0, tm // NORM_ROWS, body, 0)
        gates_ref[...] = _dot(xn_ref[...], wg_ref[...])

    p_ref[...] = _dot(xn_ref[...], w_ref[...])


def _in_projection(x2d, shift, scale, gain, w_main, w_gates, rows_per_mod, tm, tn):
    r, d = x2d.shape
    n = w_main.shape[1]
    assert r % tm == 0 and n % tn == 0 and rows_per_mod % tm == 0
    per = rows_per_mod // tm
    return pl.pallas_call(
        _inproj_kernel,
        grid=(r // tm, n // tn),
        in_specs=[
            pl.BlockSpec((tm, d), lambda i, j: (i, 0)),
            pl.BlockSpec((1, 1, d), lambda i, j: (i // per, 0, 0)),
            pl.BlockSpec((1, 1, d), lambda i, j: (i // per, 0, 0)),
            pl.BlockSpec((1, d), lambda i, j: (0, 0)),
            pl.BlockSpec((d, tn), lambda i, j: (0, j)),
            pl.BlockSpec((d, HEAD), lambda i, j: (0, 0)),
        ],
        out_specs=[
            pl.BlockSpec((tm, tn), lambda i, j: (i, j)),
            pl.BlockSpec((tm, HEAD), lambda i, j: (i, 0)),
        ],
        out_shape=[
            jax.ShapeDtypeStruct((r, n), F32),
            jax.ShapeDtypeStruct((r, HEAD), F32),
        ],
        scratch_shapes=[pltpu.VMEM((tm, d), BF16)],
        compiler_params=_params(("arbitrary", "arbitrary")),
        name="in_projection",
    )(x2d, shift, scale, gain, w_main, w_gates)


def _outproj_kernel(a_ref, b_ref, c_ref, w_ref, x_ref, gate_ref, g_ref, o_ref, y_ref):
    tm = x_ref.shape[0]
    wa = a_ref.shape[1]
    wb = b_ref.shape[1]
    y_ref[...] = (_dot(a_ref[...], w_ref[0:wa, :])
                  + _dot(b_ref[...], w_ref[wa:wa + wb, :])
                  + _dot(c_ref[...], w_ref[wa + wb:, :]))
    def body(r, carry):
        rows = pl.ds(pl.multiple_of(r * NORM_ROWS, NORM_ROWS), NORM_ROWS)
        y = y_ref[rows, :]
        ms = jnp.mean(y * y, axis=-1, keepdims=True)
        o_ref[rows, :] = x_ref[rows, :] + gate_ref[0] * (y * lax.rsqrt(ms + EPS) * g_ref[...])
        return carry

    lax.fori_loop(0, tm // NORM_ROWS, body, 0)


def _out_projection(a, b, c, w_out, x2d, gate, gain, rows_per_mod, tm):
    r, d = x2d.shape
    assert r % tm == 0 and rows_per_mod % tm == 0
    per = rows_per_mod // tm
    return pl.pallas_call(
        _outproj_kernel,
        grid=(r // tm,),
        in_specs=[
            pl.BlockSpec((tm, a.shape[1]), lambda i: (i, 0)),
            pl.BlockSpec((tm, b.shape[1]), lambda i: (i, 0)),
            pl.BlockSpec((tm, c.shape[1]), lambda i: (i, 0)),
            pl.BlockSpec(w_out.shape, lambda i: (0, 0)),
            pl.BlockSpec((tm, d), lambda i: (i, 0)),
            pl.BlockSpec((1, 1, d), lambda i: (i // per, 0, 0)),
            pl.BlockSpec((1, d), lambda i: (0, 0)),
        ],
        out_specs=pl.BlockSpec((tm, d), lambda i: (i, 0)),
        out_shape=jax.ShapeDtypeStruct((r, d), F32),
        scratch_shapes=[pltpu.VMEM((tm, d), F32)],
        compiler_params=_params(("arbitrary",)),
        name="out_projection",
    )(a, b, c, w_out, x2d, gate, gain)


N_LEVELS = 7


def _scan_matrices(reverse):
    n = CHUNK
    t = np.arange(n)[:, None]
    u = np.arange(n)[None, :]
    cum = (u >= t) if reverse else (u <= t)
    masks = [t == u]
    size = 2
    while size <= n:
        half = size // 2
        same = (t // size) == (u // size)
        if reverse:
            masks.append(same & ((t % size) < half) & ((u % size) >= half))
        else:
            masks.append(same & ((t % size) >= half) & ((u % size) < half))
        size *= 2
    return cum.astype(np.float32), np.stack(masks).astype(np.float32)


def _scan_constants():
    cf, mf = _scan_matrices(False)
    cb, mb = _scan_matrices(True)
    return (jnp.asarray(np.stack([cf, cb]), BF16), jnp.asarray(np.stack([mf, mb]), F32))


def _largest_divisor(n, candidates):
    return next(c for c in candidates if n % c == 0)


def _scan_row(j, reverse, n_ctx_chunks, n_chunks):
    if not reverse:
        return j * CHUNK
    ctx_part = (n_ctx_chunks - 1 - j) * CHUNK
    lat_part = (n_ctx_chunks + (n_chunks - 1 - j)) * CHUNK
    return jnp.where(j < n_ctx_chunks, ctx_part, lat_part)


def _hgrn2_kernel(lbl_ref, lmat_ref, mk_ref,
                  qc_ref, ffc_ref, fbc_ref, ic_ref, gc_ref,
                  ql_ref, ffl_ref, fbl_ref, il_ref, gl_ref,
                  oc_ref, ol_ref,
                  qd_s, y_s, u_s, dec_s, st_s, *, layer):
    n_ctx = qc_ref.shape[1]
    n_lat = ql_ref.shape[1]
    n_ctx_chunks = n_ctx // CHUNK
    n_chunks = (n_ctx + n_lat) // CHUNK

    logits = lbl_ref[...]
    e = jnp.exp(logits - jnp.max(logits, axis=0, keepdims=True))
    sm = e / jnp.sum(e, axis=0, keepdims=True)
    lbs = jnp.sum(sm[:layer + 1], axis=0) - sm[0]

    row = lax.broadcasted_iota(jnp.int32, (CHUNK, 1), 0)
    r4 = row % 4

    def level_factor(size, d, f, cum):
        half = size // 2
        if size == 2:
            return jnp.where((row % 2 == 1) if d == 0 else (row % 2 == 0), f, 1.0)
        if size == 4:
            f_next = pltpu.roll(f, CHUNK - 1, 0)
            f_prev = pltpu.roll(f, 1, 0)
            if d == 0:
                return jnp.where(r4 == 0, f_next, jnp.where(r4 == 1, 1.0, jnp.where(r4 == 2, f, f * f_prev)))
            return jnp.where(r4 == 0, f * f_next, jnp.where(r4 == 1, f, jnp.where(r4 == 2, 1.0, f_prev)))
        mids = [b * size + (half - 1 if d == 0 else half) for b in range(CHUNK // size)]
        ref = jnp.concatenate([jnp.broadcast_to(cum[m:m + 1, :], (size, HEAD)) for m in mids], axis=0)
        return jnp.exp(-jnp.abs(cum - ref))

    def phase_a(q_ref, f_refs, i_ref, off, n):
        par = _largest_divisor(n // CHUNK, (4, 2, 1))

        def body(c, carry):
            chains = []
            for u in range(par):
                ci = c * par + u
                rows = pl.ds(pl.multiple_of(ci * CHUNK, CHUNK), CHUNK)
                srows = pl.ds(pl.multiple_of(off + ci * CHUNK, CHUNK), CHUNK)
                q = _silu(q_ref[0, rows, :])
                v = i_ref[0, rows, :].astype(BF16)
                for d in range(2):
                    lb = lbs[d]
                    sig = jax.nn.sigmoid(f_refs[d][0, rows, :])
                    f = lb + (1.0 - lb) * sig
                    k = (1.0 - lb) * (1.0 - sig)
                    logf = jnp.log(f)
                    hi = logf.astype(BF16)
                    mid = (logf - hi.astype(F32)).astype(BF16)
                    res = _dot(lmat_ref[d], jnp.concatenate([hi, mid], axis=1))
                    chains.append(dict(d=d, q=q, v=v, k=k, f=f, res=res, srows=srows,
                                       chunk=off // CHUNK + ci))
            for ch in chains:
                d, q, k = ch["d"], ch["q"], ch["k"]
                cum = ch["res"][:, :HEAD] + ch["res"][:, HEAD:]
                total = cum[CHUNK - 1:CHUNK, :] if d == 0 else cum[0:1, :]
                dots = [_dot_nt(q.astype(BF16), k.astype(BF16))]
                for lvl in range(1, N_LEVELS):
                    x = level_factor(2 ** lvl, d, ch["f"], cum)
                    dots.append(_dot_nt((q * x).astype(BF16), (k * x).astype(BF16)))
                ch["dots"] = dots
                qd_s[d, ch["srows"], :] = (q * jnp.exp(cum)).astype(BF16)
                ch["u"] = _dot_tn(ch["v"], (k * jnp.exp(total - cum)).astype(BF16))
                drow = pl.multiple_of(ch["chunk"] * 8, 8)
                dec_s[d, pl.ds(drow, 8), :] = jnp.broadcast_to(jnp.exp(total), (8, HEAD))
            for ch in chains:
                d = ch["d"]
                scores = mk_ref[d, 0] * ch["dots"][0]
                for lvl in range(1, N_LEVELS):
                    scores = scores + mk_ref[d, lvl] * ch["dots"][lvl]
                ch["y"] = _dot(scores.astype(BF16), ch["v"])
                u_s[d, pl.ds(pl.multiple_of(ch["chunk"] * HEAD, HEAD), HEAD), :] = ch["u"]
            for ch in chains:
                y_s[ch["d"], ch["srows"], :] = ch["y"]
            return carry
        lax.fori_loop(0, n // CHUNK // par, body, 0)

    phase_a(qc_ref, (ffc_ref, fbc_ref), ic_ref, 0, n_ctx)
    phase_a(ql_ref, (ffl_ref, fbl_ref), il_ref, n_ctx, n_lat)

    st_s[...] = jnp.zeros_like(st_s)

    steps = _largest_divisor(n_chunks, (4, 2, 1))

    def phase_b(jj, carry):
        states = [st_s[0], st_s[1]]
        for u in range(steps):
            j = jj * steps + u
            for d in range(2):
                r0 = pl.multiple_of(_scan_row(j, d == 1, n_ctx_chunks, n_chunks), CHUNK)
                rows = pl.ds(r0, CHUNK)
                chunk = r0 // CHUNK
                inter = _dot_nt(qd_s[d, rows, :], states[d].astype(BF16))
                dec = dec_s[d, pl.ds(pl.multiple_of(chunk * 8, 8), 1), :]
                y_s[d, rows, :] = y_s[d, rows, :] + inter
                states[d] = states[d] * dec + u_s[d, pl.ds(pl.multiple_of(chunk * HEAD, HEAD), HEAD), :]
        st_s[0] = states[0]
        st_s[1] = states[1]
        return carry

    lax.fori_loop(0, n_chunks // steps, phase_b, 0)

    def readout(gate_ref, o_ref, off, n):
        par = _largest_divisor(n // CHUNK, (4, 2, 1))

        def body(c, carry):
            for u in range(par):
                ci = c * par + u
                rows = pl.ds(pl.multiple_of(ci * CHUNK, CHUNK), CHUNK)
                srows = pl.ds(pl.multiple_of(off + ci * CHUNK, CHUNK), CHUNK)
                y = y_s[0, srows, :] + y_s[1, srows, :]
                ms = jnp.mean(y * y, axis=-1, keepdims=True)
                o_ref[0, rows, :] = (y * lax.rsqrt(ms + EPS) * _silu(gate_ref[0, rows, :])).astype(o_ref.dtype)
            return carry
        lax.fori_loop(0, n // CHUNK // par, body, 0)

    readout(gc_ref, oc_ref, 0, n_ctx)
    readout(gl_ref, ol_ref, n_ctx, n_lat)


def _hgrn2(p_ctx, p_lat, lb_logits, consts, layer):
    bsz, n_ctx, _ = p_ctx.shape
    n_lat = p_lat.shape[1]
    t = n_ctx + n_lat
    lmat, mk = consts
    depth = lb_logits.shape[0]
    lbl = lb_logits.reshape(depth, 2, A_HEADS, 1, HEAD).transpose(0, 1, 3, 2, 4).reshape(depth, 2, 1, A_HEADS * HEAD)

    def col(n, c0):
        return pl.BlockSpec((1, n, HEAD), lambda b, h, c0=c0: (b, 0, c0 + h))

    cols = (COL_A_Q, COL_A_FF, COL_A_FB, COL_A_I, COL_A_GATE)
    return pl.pallas_call(
        functools.partial(_hgrn2_kernel, layer=layer),
        grid=(bsz, A_HEADS),
        in_specs=[
            pl.BlockSpec((depth, 2, 1, HEAD), lambda b, h: (0, 0, 0, h)),
            pl.BlockSpec(lmat.shape, lambda b, h: (0, 0, 0)),
            pl.BlockSpec(mk.shape, lambda b, h: (0, 0, 0, 0)),
        ] + [col(n_ctx, c) for c in cols] + [col(n_lat, c) for c in cols],
        out_specs=[
            pl.BlockSpec((1, n_ctx, HEAD), lambda b, h: (b, 0, h)),
            pl.BlockSpec((1, n_lat, HEAD), lambda b, h: (b, 0, h)),
        ],
        out_shape=[
            jax.ShapeDtypeStruct((bsz, n_ctx, A_HEADS * HEAD), BF16),
            jax.ShapeDtypeStruct((bsz, n_lat, A_HEADS * HEAD), BF16),
        ],
        scratch_shapes=[
            pltpu.VMEM((2, t, HEAD), BF16),
            pltpu.VMEM((2, t, HEAD), F32),
            pltpu.VMEM((2, (t // CHUNK) * HEAD, HEAD), F32),
            pltpu.VMEM((2, (t // CHUNK) * 8, HEAD), F32),
            pltpu.VMEM((2, HEAD, HEAD), F32),
        ],
        compiler_params=_params(("arbitrary", "arbitrary")),
        name="hgrn2",
    )(lbl, lmat, mk, *([p_ctx] * 5), *([p_lat] * 5))


def _mlstm_constants():
    n = CHUNK
    a = np.arange(n)[:, None]
    b = np.arange(n)[None, :]
    lt, rhs, causal = [], [], []
    for reverse in (False, True):
        upto = (b >= a) if reverse else (b <= a)
        after = (b < a) if reverse else (b > a)
        lt.append(np.stack([upto, after]))
        m = after.T
        ext = np.concatenate([m, np.zeros((n, HEAD - n)), np.ones((n, HEAD))], axis=1)
        rhs.append(np.concatenate([ext, ext], axis=0))
        causal.append(upto)
    return (jnp.asarray(np.stack(lt), F32), jnp.asarray(np.stack(rhs), BF16),
            jnp.asarray(np.stack(causal), F32))


def _mlstm_kernel(lt_ref, rhs_ref, causal_ref, eye_ref, cw_ref, bias_ref,
                  qc_ref, kc_ref, vc_ref, oc_ref, zc_ref, gc_ref,
                  ql_ref, kl_ref, vl_ref, ol_ref, zl_ref, gl_ref,
                  outc_ref, outl_ref,
                  q_s, s_s, cols_s, u_s, sc_s, st_s):
    h = pl.program_id(1)
    n_ctx = qc_ref.shape[1]
    n_lat = ql_ref.shape[1]
    n_ctx_chunks = n_ctx // CHUNK
    n_chunks = (n_ctx + n_lat) // CHUNK
    row_id = lax.broadcasted_iota(jnp.int32, (CHUNK, 1), 0)
    lane = lax.broadcasted_iota(jnp.int32, (1, HEAD), 1)
    sub8 = lax.broadcasted_iota(jnp.int32, (8, 1), 0)
    gate_row = lax.broadcasted_iota(jnp.int32, (N_GATES, 1), 0)
    eye_c = (row_id == lax.broadcasted_iota(jnp.int32, (1, CHUNK), 1)).astype(F32)
    ones_b = jnp.ones((CHUNK, HEAD), BF16)

    def conv_silu(src, ci, n, w):
        s = pl.multiple_of(ci * CHUNK, CHUNK)
        cur = src[0, pl.ds(s, CHUNK), :]
        up = src[0, pl.ds(pl.multiple_of(jnp.maximum(s - 8, 0), 8), 8), :]
        dn = src[0, pl.ds(pl.multiple_of(jnp.minimum(s + CHUNK, n - 8), 8), 8), :]
        prev_row = jnp.where(s > 0, up[7:8, :], 0.0)
        next_row = jnp.where(s + CHUNK < n, dn[0:1, :], 0.0)
        prev = jnp.where(row_id == 0, prev_row, pltpu.roll(cur, 1, 0))
        nxt = jnp.where(row_id == CHUNK - 1, next_row, pltpu.roll(cur, CHUNK - 1, 0))
        return _silu(w[0:1, :] * prev + w[1:2, :] * cur + w[2:3, :] * nxt)

    def pick_row(a, idx):
        return jnp.sum(jnp.where(gate_row == idx, a, 0.0), axis=0, keepdims=True)

    def phase_a(q_ref, k_ref, v_ref, g_ref, off, n):
        par = _largest_divisor(n // CHUNK, (4, 2, 1))

        def body(c, carry):
            chunks = []
            for u in range(par):
                ci = c * par + u
                rows = pl.ds(pl.multiple_of(ci * CHUNK, CHUNK), CHUNK)
                srows = pl.ds(pl.multiple_of(off + ci * CHUNK, CHUNK), CHUNK)
                gt = _dot3_nt(eye_ref[...], g_ref[0, rows, :]) + bias_ref[:, 0:CHUNK]
                qb = (conv_silu(q_ref, ci, n, cw_ref[0]) * (HEAD ** -0.5)).astype(BF16)
                k = conv_silu(k_ref, ci, n, cw_ref[1])
                q_s[srows, :] = qb
                vaug = jnp.concatenate([v_ref[0, rows, :].astype(BF16), ones_b], axis=1)
                chunks.append(dict(k=k, qk=_dot_nt(qb, k.astype(BF16)), vaug=vaug, gt=gt, srows=srows,
                                   chunk=off // CHUNK + ci))
            chains = []
            for ck in chunks:
                for d in range(2):
                    gi = 2 * B_HEADS * d + h
                    li_row = pick_row(ck["gt"], gi)
                    lf_row = _log_sigmoid(pick_row(ck["gt"], gi + B_HEADS))
                    lhs = jnp.concatenate([lt_ref[d, 0] * lf_row,
                                           lt_ref[d, 1] * lf_row + eye_c * li_row], axis=0)
                    hi = lhs.astype(BF16)
                    mid = (lhs - hi.astype(F32)).astype(BF16)
                    out = _dot(jnp.concatenate([hi, mid], axis=1), rhs_ref[d])
                    chains.append(dict(ck, d=d, li_row=li_row, out=out))
            for ch in chains:
                d, out = ch["d"], ch["out"]
                cum_rep = out[0:CHUNK, HEAD:]
                logw_rep = out[CHUNK:, HEAD:]
                log_d = jnp.where(causal_ref[d] > 0.0, out[0:CHUNK, 0:CHUNK] + ch["li_row"], NEG)
                a_loc = jnp.max(log_d, axis=1, keepdims=True)
                s_s[d, ch["srows"], :] = (ch["qk"] * jnp.exp(log_d - a_loc)).astype(BF16)
                cols_s[d, ch["srows"], :] = jnp.where(lane < CHUNK, a_loc, cum_rep)
                total = cum_rep[CHUNK - 1:CHUNK, :] if d == 0 else cum_rep[0:1, :]
                m_loc = jnp.max(logw_rep, axis=0, keepdims=True)
                kw = (ch["k"] * jnp.exp(logw_rep - m_loc)).astype(BF16)
                ch["u"] = _dot_tn(kw, ch["vaug"])
                sc_s[d, pl.ds(pl.multiple_of(ch["chunk"] * 8, 8), 8), :] = jnp.where(
                    sub8 == 0, total, jnp.where(sub8 == 1, m_loc, 0.0))
            for ch in chains:
                u_s[ch["d"], pl.ds(pl.multiple_of(ch["chunk"] * HEAD, HEAD), HEAD), :] = ch["u"].astype(BF16)
            return carry
        lax.fori_loop(0, n // CHUNK // par, body, 0)

    phase_a(qc_ref, kc_ref, vc_ref, gc_ref, 0, n_ctx)
    phase_a(ql_ref, kl_ref, vl_ref, gl_ref, n_ctx, n_lat)

    st_s[...] = jnp.zeros_like(st_s)
    steps = _largest_divisor(n_chunks, (4, 2, 1))
    for d in range(2):
        def phase_b(jj, m, d=d):
            for u in range(steps):
                chunk = _scan_row(jj * steps + u, d == 1, n_ctx_chunks, n_chunks) // CHUNK
                srow = pl.ds(pl.multiple_of(chunk * 8, 8), 8)
                urow = pl.ds(pl.multiple_of(chunk * HEAD, HEAD), HEAD)
                sc = sc_s[d, srow, :]
                total, m_loc = sc[0:1, :], sc[1:2, :]
                m_new = jnp.maximum(total + m, m_loc)
                state = st_s[d]
                inc = u_s[d, urow, :].astype(F32)
                u_s[d, urow, :] = state.astype(BF16)
                sc_s[d, srow, :] = jnp.where(sub8 == 2, m, sc)
                keep = jnp.exp(total + m - m_new)
                gain = jnp.exp(m_loc - m_new)
                st_s[d] = (jnp.concatenate([keep, keep], axis=1) * state
                           + jnp.concatenate([gain, gain], axis=1) * inc)
                m = m_new
            return m
        lax.fori_loop(0, n_chunks // steps, phase_b, jnp.zeros((1, HEAD), F32))

    def phase_c(v_ref, o_ref, z_ref, out_ref, off, n):
        par = _largest_divisor(n // CHUNK, (4, 2, 1))

        def body(c, carry):
            chains = []
            for u in range(par):
                ci = c * par + u
                rows = pl.ds(pl.multiple_of(ci * CHUNK, CHUNK), CHUNK)
                srows = pl.ds(pl.multiple_of(off + ci * CHUNK, CHUNK), CHUNK)
                chunk = off // CHUNK + ci
                qb = q_s[srows, :]
                vaug = jnp.concatenate([v_ref[0, rows, :].astype(BF16), ones_b], axis=1)
                for d in range(2):
                    intra = _dot(s_s[d, srows, :], vaug)
                    inter = _dot(qb, u_s[d, pl.ds(pl.multiple_of(chunk * HEAD, HEAD), HEAD), :])
                    chains.append(dict(d=d, u=u, rows=rows, srows=srows, chunk=chunk, intra=intra, inter=inter))
            ys = [None] * par
            for ch in chains:
                d = ch["d"]
                cols = cols_s[d, ch["srows"], :]
                swapped = pltpu.roll(cols, CHUNK, 1)
                a_rep = jnp.where(lane < CHUNK, cols, swapped)
                cum_rep = jnp.where(lane < CHUNK, swapped, cols)
                m = sc_s[d, pl.ds(pl.multiple_of(ch["chunk"] * 8, 8), 8), :][2:3, :]
                inter_l = cum_rep + m
                m_t = jnp.maximum(a_rep, inter_l)
                wi = jnp.exp(a_rep - m_t)
                we = jnp.exp(inter_l - m_t)
                num = wi * ch["intra"][:, :HEAD] + we * ch["inter"][:, :HEAD]
                den = wi * ch["intra"][:, HEAD:] + we * ch["inter"][:, HEAD:]
                hd = num / jnp.maximum(jnp.abs(den), jnp.exp(-m_t))
                ys[ch["u"]] = hd if ys[ch["u"]] is None else ys[ch["u"]] + hd
                if d == 1:
                    rows = ch["rows"]
                    hh = jax.nn.sigmoid(o_ref[0, rows, :]) * ys[ch["u"]]
                    ms = jnp.mean(hh * hh, axis=-1, keepdims=True)
                    out_ref[0, rows, :] = (hh * lax.rsqrt(ms + EPS) * _silu(z_ref[0, rows, :])).astype(out_ref.dtype)
            return carry
        lax.fori_loop(0, n // CHUNK // par, body, 0)

    phase_c(vc_ref, oc_ref, zc_ref, outc_ref, 0, n_ctx)
    phase_c(vl_ref, ol_ref, zl_ref, outl_ref, n_ctx, n_lat)


def _mlstm(p_ctx, p_lat, g_ctx, g_lat, conv_w, gate_bias, consts):
    bsz, n_ctx, _ = p_ctx.shape
    n_lat = p_lat.shape[1]
    t = n_ctx + n_lat
    n_chunks = t // CHUNK
    lt, rhs, causal = consts
    eye = jnp.eye(N_GATES, HEAD, dtype=BF16)
    cw = conv_w.reshape(3, 2, B_HEADS * HEAD).transpose(1, 0, 2)
    bias = jnp.broadcast_to(gate_bias.reshape(N_GATES, 1), (N_GATES, HEAD))

    def col(n, c0):
        return pl.BlockSpec((1, n, HEAD), lambda b, h, c0=c0: (b, 0, c0 + h))

    def gate_spec(n):
        return pl.BlockSpec((1, n, HEAD), lambda b, h: (b, 0, 0))

    cols = (COL_B_Q, COL_B_K, COL_B_V, COL_B_O, COL_B_Z)
    return pl.pallas_call(
        _mlstm_kernel,
        grid=(bsz, B_HEADS),
        in_specs=[
            pl.BlockSpec(lt.shape, lambda b, h: (0, 0, 0, 0)),
            pl.BlockSpec(rhs.shape, lambda b, h: (0, 0, 0)),
            pl.BlockSpec(causal.shape, lambda b, h: (0, 0, 0)),
            pl.BlockSpec(eye.shape, lambda b, h: (0, 0)),
            pl.BlockSpec((2, 3, HEAD), lambda b, h: (0, 0, h)),
            pl.BlockSpec((N_GATES, HEAD), lambda b, h: (0, 0)),
        ] + [col(n_ctx, c) for c in cols] + [gate_spec(n_ctx)]
          + [col(n_lat, c) for c in cols] + [gate_spec(n_lat)],
        out_specs=[
            pl.BlockSpec((1, n_ctx, HEAD), lambda b, h: (b, 0, h)),
            pl.BlockSpec((1, n_lat, HEAD), lambda b, h: (b, 0, h)),
        ],
        out_shape=[
            jax.ShapeDtypeStruct((bsz, n_ctx, B_HEADS * HEAD), BF16),
            jax.ShapeDtypeStruct((bsz, n_lat, B_HEADS * HEAD), BF16),
        ],
        scratch_shapes=[
            pltpu.VMEM((t, HEAD), BF16),
            pltpu.VMEM((2, t, CHUNK), BF16),
            pltpu.VMEM((2, t, HEAD), F32),
            pltpu.VMEM((2, n_chunks * HEAD, 2 * HEAD), BF16),
            pltpu.VMEM((2, n_chunks * 8, HEAD), F32),
            pltpu.VMEM((2, HEAD, 2 * HEAD), F32),
        ],
        compiler_params=_params(("arbitrary", "arbitrary")),
        name="mlstm",
    )(lt, rhs, causal, eye, cw, bias, *([p_ctx] * 5), g_ctx, *([p_lat] * 5), g_lat)


def _rope(x, cos, sin_lo, sin_hi):
    return x * cos + pltpu.roll(x, HEAD - 32, 1) * sin_lo + pltpu.roll(x, 32, 1) * sin_hi


def _softmax_attend(sink_col, parts):
    m = sink_col
    for s, _ in parts:
        m = jnp.maximum(m, jnp.max(s, axis=1, keepdims=True))
    den = jnp.exp(sink_col - m)
    acc = None
    for s, v in parts:
        p = jnp.exp(s - m)
        den = den + jnp.sum(p, axis=1, keepdims=True)
        pv = _dot(p.astype(BF16), v)
        acc = pv if acc is None else acc + pv
    return acc / den


def _sink_column(sink_ref, kv, rows_per_head):
    row = lax.broadcasted_iota(jnp.int32, (C_GROUP * rows_per_head, 1), 0)
    col = jnp.zeros((C_GROUP * rows_per_head, 1), F32)
    for g in range(C_GROUP):
        col = jnp.where(row // rows_per_head == g, sink_ref[kv * C_GROUP + g], col)
    return col


def _attn_kernel(sink_ref, cos_ref, slo_ref, shi_ref, q_ref, k_ref, v_ref, kc_ref, vc_ref, z_ref,
                 o_ref, kr_s):
    kv = pl.program_id(1)
    i = pl.program_id(2)
    n = k_ref.shape[1]
    span = 3 * ATTN_BLOCK

    @pl.when(i == 0)
    def _():
        def body(c, carry):
            rows = pl.ds(pl.multiple_of(c * ATTN_BLOCK, ATTN_BLOCK), ATTN_BLOCK)
            kr_s[rows, :] = _rope(k_ref[0, rows, :], cos_ref[rows, :], slo_ref[rows, :],
                                  shi_ref[rows, :]).astype(BF16)
            return carry
        lax.fori_loop(0, n // ATTN_BLOCK, body, 0)

    qrows = pl.ds(pl.multiple_of(i * ATTN_BLOCK, ATTN_BLOCK), ATTN_BLOCK)
    cos = cos_ref[qrows, :]
    slo = slo_ref[qrows, :]
    shi = shi_ref[qrows, :]
    scale = HEAD ** -0.5
    q4 = jnp.concatenate(
        [_rope(q_ref[0, :, g * HEAD:(g + 1) * HEAD], cos, slo, shi) * scale for g in range(C_GROUP)],
        axis=0).astype(BF16)
    start = pl.multiple_of(jnp.clip((i - 1) * ATTN_BLOCK, 0, n - span), ATTN_BLOCK)
    kw = kr_s[pl.ds(start, span), :]
    vw = v_ref[0, pl.ds(start, span), :].astype(BF16)
    s_loc = _dot_nt(q4, kw)
    rows_all = C_GROUP * ATTN_BLOCK
    qpos = i * ATTN_BLOCK + lax.broadcasted_iota(jnp.int32, (rows_all, 1), 0) % ATTN_BLOCK
    kpos = start + lax.broadcasted_iota(jnp.int32, (1, span), 1)
    s_loc = jnp.where(jnp.abs(kpos - qpos) <= WINDOW, s_loc, NEG)
    s_ctx = _dot_nt(q4, kc_ref[0].astype(BF16))
    out = _softmax_attend(_sink_column(sink_ref, kv, ATTN_BLOCK),
                          [(s_loc, vw), (s_ctx, vc_ref[0].astype(BF16))])
    for g in range(C_GROUP):
        zg = z_ref[0, :, g * HEAD:(g + 1) * HEAD]
        o_ref[0, :, g * HEAD:(g + 1) * HEAD] = (out[g * ATTN_BLOCK:(g + 1) * ATTN_BLOCK] * _silu(zg)).astype(o_ref.dtype)


def _attention(p_ctx, p_lat, sink, rope):
    bsz, n_ctx, _ = p_ctx.shape
    n = p_lat.shape[1]
    cos, slo, shi = rope
    gw = C_GROUP * HEAD
    gblk = C_GROUP

    full = pl.BlockSpec((n, HEAD), lambda b, kv, i: (0, 0))
    return pl.pallas_call(
        _attn_kernel,
        grid=(bsz, C_KV_HEADS, n // ATTN_BLOCK),
        in_specs=[
            pl.BlockSpec(memory_space=pltpu.SMEM),
            full, full, full,
            pl.BlockSpec((1, ATTN_BLOCK, gw), lambda b, kv, i: (b, i, COL_C_Q // gblk + kv)),
            pl.BlockSpec((1, n, HEAD), lambda b, kv, i: (b, 0, COL_C_K + kv)),
            pl.BlockSpec((1, n, HEAD), lambda b, kv, i: (b, 0, COL_C_V + kv)),
            pl.BlockSpec((1, n_ctx, HEAD), lambda b, kv, i: (b, 0, COL_C_K + kv)),
            pl.BlockSpec((1, n_ctx, HEAD), lambda b, kv, i: (b, 0, COL_C_V + kv)),
            pl.BlockSpec((1, ATTN_BLOCK, gw), lambda b, kv, i: (b, i, COL_C_Z // gblk + kv)),
        ],
        out_specs=pl.BlockSpec((1, ATTN_BLOCK, gw), lambda b, kv, i: (b, i, kv)),
        out_shape=jax.ShapeDtypeStruct((bsz, n, C_Q_HEADS * HEAD), BF16),
        scratch_shapes=[pltpu.VMEM((n, HEAD), BF16)],
        compiler_params=_params(("arbitrary", "arbitrary", "arbitrary")),
        name="window_attention",
    )(sink, cos, slo, shi, p_lat, p_lat, p_lat, p_ctx, p_ctx, p_lat)


def _ctx_attn_kernel(sink_ref, q_ref, k_ref, v_ref, z_ref, o_ref):
    kv = pl.program_id(1)
    n_ctx = q_ref.shape[1]
    scale = HEAD ** -0.5
    q4 = jnp.concatenate([q_ref[0, :, g * HEAD:(g + 1) * HEAD] * scale for g in range(C_GROUP)],
                         axis=0).astype(BF16)
    s = _dot_nt(q4, k_ref[0].astype(BF16))
    out = _softmax_attend(_sink_column(sink_ref, kv, n_ctx), [(s, v_ref[0].astype(BF16))])
    for g in range(C_GROUP):
        zg = z_ref[0, :, g * HEAD:(g + 1) * HEAD]
        o_ref[0, :, g * HEAD:(g + 1) * HEAD] = (out[g * n_ctx:(g + 1) * n_ctx] * _silu(zg)).astype(o_ref.dtype)


def _context_attention(p_ctx, sink):
    bsz, n_ctx, _ = p_ctx.shape
    gw = C_GROUP * HEAD
    gblk = C_GROUP
    return pl.pallas_call(
        _ctx_attn_kernel,
        grid=(bsz, C_KV_HEADS),
        in_specs=[
            pl.BlockSpec(memory_space=pltpu.SMEM),
            pl.BlockSpec((1, n_ctx, gw), lambda b, kv: (b, 0, COL_C_Q // gblk + kv)),
            pl.BlockSpec((1, n_ctx, HEAD), lambda b, kv: (b, 0, COL_C_K + kv)),
            pl.BlockSpec((1, n_ctx, HEAD), lambda b, kv: (b, 0, COL_C_V + kv)),
            pl.BlockSpec((1, n_ctx, gw), lambda b, kv: (b, 0, COL_C_Z // gblk + kv)),
        ],
        out_specs=pl.BlockSpec((1, n_ctx, gw), lambda b, kv: (b, 0, kv)),
        out_shape=jax.ShapeDtypeStruct((bsz, n_ctx, C_Q_HEADS * HEAD), BF16),
        compiler_params=_params(("arbitrary", "arbitrary")),
        name="context_attention",
    )(sink, p_ctx, p_ctx, p_ctx, p_ctx)


def _rope_tables(n):
    axis_dim = HEAD // 2
    rows = n // GRID_W
    row = jnp.repeat(jnp.arange(rows, dtype=F32), GRID_W)
    col = jnp.tile(jnp.arange(GRID_W, dtype=F32), rows)
    inv_freq = ROPE_BASE ** (-jnp.arange(0, axis_dim, 2, dtype=F32) / axis_dim)
    ang_r = row[:, None] * inv_freq[None, :]
    ang_c = col[:, None] * inv_freq[None, :]
    zero = jnp.zeros_like(ang_r)
    cos = jnp.concatenate([jnp.cos(ang_r)] * 2 + [jnp.cos(ang_c)] * 2, axis=-1)
    sin_lo = jnp.concatenate([-jnp.sin(ang_r), zero, -jnp.sin(ang_c), zero], axis=-1)
    sin_hi = jnp.concatenate([zero, jnp.sin(ang_r), zero, jnp.sin(ang_c)], axis=-1)
    return cos, sin_lo, sin_hi


def _pick_tile(n, candidates):
    for c in candidates:
        if n % c == 0:
            return c
    raise ValueError(f"no tile for {n}")


def kernel(x, c, ctx, c_ctx, w_mod, b_mod, g_pre, g_post, w_in, hgrn_lb_logits, mlstm_conv_w,
           mlstm_gate_bias, attn_sink, w_out):
    bsz, n_lat, d = x.shape
    n_ctx = ctx.shape[1]
    depth = w_mod.shape[0]
    assert n_ctx % CHUNK == 0 and n_lat % ATTN_BLOCK == 0 and n_lat >= 3 * ATTN_BLOCK

    cond = jnp.concatenate([c, c_ctx[None, :]], axis=0)
    pad = (-cond.shape[0]) % 8
    cond = jnp.pad(cond, ((0, pad), (0, 0)))
    mod = _modulation(cond, w_mod, b_mod)

    consts = _scan_constants()
    mlstm_consts = _mlstm_constants()
    rope = _rope_tables(n_lat)

    tm_lat = _pick_tile(n_lat, (1024, 512, 256, 128))
    tm_ctx = _pick_tile(bsz * n_ctx, (1024, 512, 256, 128))
    tn = _pick_tile(MAIN_WIDTH, (1280, 768, 512))
    to_lat = _pick_tile(n_lat, (512, 256, 128))
    to_ctx = _pick_tile(bsz * n_ctx, (512, 256, 128))

    x2d = x.reshape(bsz * n_lat, d)
    h2d = ctx.reshape(bsz * n_ctx, d)
    for layer in range(depth):
        with_ctx = layer < depth - 1
        shift_l = mod[layer, :bsz, 0:d].reshape(bsz, 1, d)
        scale_l = mod[layer, :bsz, d:2 * d].reshape(bsz, 1, d)
        gate_l = mod[layer, :bsz, 2 * d:].reshape(bsz, 1, d)
        shift_c = mod[layer, bsz:bsz + 1, 0:d].reshape(1, 1, d)
        scale_c = mod[layer, bsz:bsz + 1, d:2 * d].reshape(1, 1, d)
        gate_c = mod[layer, bsz:bsz + 1, 2 * d:].reshape(1, 1, d)

        w = w_in[layer]
        w_main = jnp.concatenate([w[:, :GATES_OFFSET], w[:, GATES_OFFSET + N_GATES:]], axis=1).astype(BF16)
        w_gates = jnp.pad(w[:, GATES_OFFSET:GATES_OFFSET + N_GATES], ((0, 0), (0, HEAD - N_GATES))).astype(BF16)
        gain_pre = g_pre[layer].reshape(1, d)
        gain_post = g_post[layer].reshape(1, d)
        wo = w_out[layer].astype(BF16)

        p_lat, g_lat = _in_projection(x2d, shift_l, scale_l, gain_pre, w_main, w_gates, n_lat, tm_lat, tn)
        p_ctx, g_ctx = _in_projection(h2d, shift_c, scale_c, gain_pre, w_main, w_gates, bsz * n_ctx, tm_ctx, tn)
        p_lat = p_lat.reshape(bsz, n_lat, MAIN_WIDTH)
        p_ctx = p_ctx.reshape(bsz, n_ctx, MAIN_WIDTH)
        g_lat = g_lat.reshape(bsz, n_lat, HEAD)
        g_ctx = g_ctx.reshape(bsz, n_ctx, HEAD)

        a_c, a_l = _hgrn2(p_ctx, p_lat, hgrn_lb_logits, consts, layer)
        b_c, b_l = _mlstm(p_ctx, p_lat, g_ctx, g_lat, mlstm_conv_w[layer], mlstm_gate_bias[layer],
                          mlstm_consts)
        c_l = _attention(p_ctx, p_lat, attn_sink[layer], rope)

        x2d = _out_projection(a_l.reshape(bsz * n_lat, -1), b_l.reshape(bsz * n_lat, -1),
                              c_l.reshape(bsz * n_lat, -1), wo, x2d, gate_l, gain_post, n_lat, to_lat)
        if with_ctx:
            c_c = _context_attention(p_ctx, attn_sink[layer])
            h2d = _out_projection(a_c.reshape(bsz * n_ctx, -1), b_c.reshape(bsz * n_ctx, -1),
                                  c_c.reshape(bsz * n_ctx, -1), wo, h2d, gate_c, gain_post,
                                  bsz * n_ctx, to_ctx)
    return x2d.reshape(bsz, n_lat, d)
```

```python
import functools

import numpy as np
import jax
import jax.numpy as jnp
from jax import lax
from jax.experimental import pallas as pl
from jax.experimental.pallas import tpu as pltpu

F32 = jnp.float32
BF16 = jnp.bfloat16

EPS = 1e-6
NEG = -1e30
HEAD = 128
A_HEADS = 4
B_HEADS = 4
C_Q_HEADS = 8
C_KV_HEADS = 2
C_GROUP = C_Q_HEADS // C_KV_HEADS
CHUNK = 64
ATTN_BLOCK = 128
WINDOW = 128
GRID_W = 64
ROPE_BASE = 10000.0
N_GATES = 4 * B_HEADS

COL_A_Q, COL_A_FF, COL_A_FB, COL_A_I, COL_A_GATE = 0, 4, 8, 12, 16
COL_B_Q, COL_B_K, COL_B_V, COL_B_O, COL_B_Z = 20, 24, 28, 32, 36
COL_C_Q, COL_C_K, COL_C_V, COL_C_Z = 40, 48, 50, 52
MAIN_BLOCKS = 60
MAIN_WIDTH = MAIN_BLOCKS * HEAD
GATES_OFFSET = 9 * 4 * HEAD

VMEM_LIMIT = 56 * 1024 * 1024


def _params(sem):
    return pltpu.CompilerParams(dimension_semantics=sem, vmem_limit_bytes=VMEM_LIMIT)


def _silu(x):
    return x * jax.nn.sigmoid(x)


def _log_sigmoid(x):
    return jnp.minimum(x, 0.0) - jnp.log(1.0 + jnp.exp(-jnp.abs(x)))


def _dot(a, b):
    return jnp.dot(a, b, preferred_element_type=F32)


def _dot_nt(a, b):
    return lax.dot_general(a, b, (((1,), (1,)), ((), ())), preferred_element_type=F32)


def _dot_tn(a, b):
    return lax.dot_general(a, b, (((0,), (0,)), ((), ())), preferred_element_type=F32)


def _split3(x):
    hi = x.astype(BF16)
    r = x - hi.astype(F32)
    mid = r.astype(BF16)
    lo = (r - mid.astype(F32)).astype(BF16)
    return hi, mid, lo


def _dot3_nt(w, x):
    hi, mid, lo = _split3(x)
    return _dot_nt(w, hi) + _dot_nt(w, mid) + _dot_nt(w, lo)


def _mod_kernel(c_ref, w_ref, b_ref, o_ref):
    a = _silu(c_ref[...])
    o_ref[0] = jnp.dot(a, w_ref[0], preferred_element_type=F32,
                       precision=lax.Precision.HIGHEST) + b_ref[0]


def _modulation(cond, w_mod, b_mod):
    depth, d, n = w_mod.shape
    rows = cond.shape[0]
    tn = 768
    assert n % tn == 0
    return pl.pallas_call(
        _mod_kernel,
        grid=(depth, n // tn),
        in_specs=[
            pl.BlockSpec((rows, d), lambda l, j: (0, 0)),
            pl.BlockSpec((1, d, tn), lambda l, j: (l, 0, j)),
            pl.BlockSpec((1, 1, tn), lambda l, j: (l, 0, j)),
        ],
        out_specs=pl.BlockSpec((1, rows, tn), lambda l, j: (l, 0, j)),
        out_shape=jax.ShapeDtypeStruct((depth, rows, n), F32),
        compiler_params=_params(("arbitrary", "arbitrary")),
        name="modulation",
    )(cond, w_mod, b_mod.reshape(depth, 1, n))


NORM_ROWS = 32
EPILOGUE_ROWS = 8


def _inproj_kernel(x_ref, shift_ref, scale_ref, g_ref, w_ref, wg_ref, p_ref, gates_ref, xn_ref, gm_ref):
    tm = x_ref.shape[0]

    @pl.when(pl.program_id(1) == 0)
    def _():
        gm_ref[...] = g_ref[...] * (1.0 + scale_ref[0])

        def body(r, carry):
            for u in range(2):
                rows = pl.ds(pl.multiple_of((2 * r + u) * NORM_ROWS, NORM_ROWS), NORM_ROWS)
                xr = x_ref[rows, :]
                ms = jnp.mean(xr * xr, axis=-1, keepdims=True)
                xn_ref[rows, :] = (xr * (lax.rsqrt(ms + EPS) * gm_ref[...]) + shift_ref[0]).astype(BF16)
            return carry

        lax.fori_loop(0, tm // NORM_ROWS // 2, body, 0)
        gates_ref[...] = _dot(xn_ref[...], wg_ref[...])

    p_ref[...] = _dot(xn_ref[...], w_ref[...])


def _in_projection(x2d, shift, scale, gain, w_main, w_gates, rows_per_mod, tm, tn):
    r, d = x2d.shape
    n = w_main.shape[1]
    assert r % tm == 0 and n % tn == 0 and rows_per_mod % tm == 0
    per = rows_per_mod // tm
    return pl.pallas_call(
        _inproj_kernel,
        grid=(r // tm, n // tn),
        in_specs=[
            pl.BlockSpec((tm, d), lambda i, j: (i, 0)),
            pl.BlockSpec((1, 1, d), lambda i, j: (i // per, 0, 0)),
            pl.BlockSpec((1, 1, d), lambda i, j: (i // per, 0, 0)),
            pl.BlockSpec((1, d), lambda i, j: (0, 0)),
            pl.BlockSpec((d, tn), lambda i, j: (0, j)),
            pl.BlockSpec((d, HEAD), lambda i, j: (0, 0)),
        ],
        out_specs=[
            pl.BlockSpec((tm, tn), lambda i, j: (i, j)),
            pl.BlockSpec((tm, HEAD), lambda i, j: (i, 0)),
        ],
        out_shape=[
            jax.ShapeDtypeStruct((r, n), F32),
            jax.ShapeDtypeStruct((r, HEAD), F32),
        ],
        scratch_shapes=[pltpu.VMEM((tm, d), BF16), pltpu.VMEM((1, d), F32)],
        compiler_params=_params(("arbitrary", "arbitrary")),
        name="in_projection",
    )(x2d, shift, scale, gain, w_main, w_gates)


def _outproj_kernel(a_ref, b_ref, c_ref, w_ref, x_ref, gate_ref, g_ref, o_ref, y_s):
    i = pl.program_id(0)
    tm = x_ref.shape[0]
    slot = i % 2

    @pl.when(i == 0)
    def _():
        y_s[1] = jnp.zeros(y_s.shape[1:], F32)

    for r in range(tm // EPILOGUE_ROWS):
        rows = slice(r * EPILOGUE_ROWS, (r + 1) * EPILOGUE_ROWS)
        y = y_s[1 - slot, rows, :]
        ms = jnp.mean(y * y, axis=-1, keepdims=True)
        o_ref[rows, :] = x_ref[rows, :] + y * (lax.rsqrt(ms + EPS) * (gate_ref[0] * g_ref[...]))
    y_s[slot] = _dot(jnp.concatenate([a_ref[...], b_ref[...], c_ref[...]], axis=1), w_ref[...])


def _out_projection(a, b, c, w_out, x2d, gate, gain, rows_per_mod, tm):
    r, d = x2d.shape
    assert r % tm == 0 and rows_per_mod % tm == 0
    per = rows_per_mod // tm
    n = r // tm
    cur = lambda i: (jnp.minimum(i, n - 1), 0)
    prev = lambda i: (jnp.maximum(i - 1, 0), 0)
    return pl.pallas_call(
        _outproj_kernel,
        grid=(n + 1,),
        in_specs=[
            pl.BlockSpec((tm, a.shape[1]), cur),
            pl.BlockSpec((tm, b.shape[1]), cur),
            pl.BlockSpec((tm, c.shape[1]), cur),
            pl.BlockSpec(w_out.shape, lambda i: (0, 0)),
            pl.BlockSpec((tm, d), prev),
            pl.BlockSpec((1, 1, d), lambda i: (jnp.maximum(i - 1, 0) // per, 0, 0)),
            pl.BlockSpec((1, d), lambda i: (0, 0)),
        ],
        out_specs=pl.BlockSpec((tm, d), prev),
        out_shape=jax.ShapeDtypeStruct((r, d), F32),
        scratch_shapes=[pltpu.VMEM((2, tm, d), F32)],
        compiler_params=_params(("arbitrary",)),
        name="out_projection",
    )(a, b, c, w_out, x2d, gate, gain)


N_LEVELS = 7


def _scan_matrices(reverse):
    n = CHUNK
    t = np.arange(n)[:, None]
    u = np.arange(n)[None, :]
    cum = (u >= t) if reverse else (u <= t)
    masks = [t == u]
    size = 2
    while size <= n:
        half = size // 2
        same = (t // size) == (u // size)
        if reverse:
            masks.append(same & ((t % size) < half) & ((u % size) >= half))
        else:
            masks.append(same & ((t % size) >= half) & ((u % size) < half))
        size *= 2
    return cum.astype(np.float32), np.stack(masks).astype(np.float32)


def _scan_constants():
    cf, mf = _scan_matrices(False)
    cb, mb = _scan_matrices(True)
    return (jnp.asarray(np.stack([cf, cb]), BF16), jnp.asarray(np.stack([mf, mb]), F32))


def _largest_divisor(n, candidates):
    return next(c for c in candidates if n % c == 0)


def _scan_row(j, reverse, n_ctx_chunks, n_chunks):
    if not reverse:
        return j * CHUNK
    ctx_part = (n_ctx_chunks - 1 - j) * CHUNK
    lat_part = (n_ctx_chunks + (n_chunks - 1 - j)) * CHUNK
    return jnp.where(j < n_ctx_chunks, ctx_part, lat_part)


def _hgrn2_kernel(lbl_ref, lmat_ref, mk_ref,
                  qc_ref, ffc_ref, fbc_ref, ic_ref, gc_ref,
                  ql_ref, ffl_ref, fbl_ref, il_ref, gl_ref,
                  oc_ref, ol_ref,
                  qd_s, y_s, u_s, dec_s, st_s, *, layer):
    n_ctx = qc_ref.shape[1]
    n_lat = ql_ref.shape[1]
    n_ctx_chunks = n_ctx // CHUNK
    n_chunks = (n_ctx + n_lat) // CHUNK

    logits = lbl_ref[...]
    e = jnp.exp(logits - jnp.max(logits, axis=0, keepdims=True))
    sm = e / jnp.sum(e, axis=0, keepdims=True)
    lbs = jnp.sum(sm[:layer + 1], axis=0) - sm[0]

    row = lax.broadcasted_iota(jnp.int32, (CHUNK, 1), 0)
    r4 = row % 4

    def level_factor(size, d, f, cum):
        half = size // 2
        if size == 2:
            return jnp.where((row % 2 == 1) if d == 0 else (row % 2 == 0), f, 1.0)
        if size == 4:
            f_next = pltpu.roll(f, CHUNK - 1, 0)
            f_prev = pltpu.roll(f, 1, 0)
            if d == 0:
                return jnp.where(r4 == 0, f_next, jnp.where(r4 == 1, 1.0, jnp.where(r4 == 2, f, f * f_prev)))
            return jnp.where(r4 == 0, f * f_next, jnp.where(r4 == 1, f, jnp.where(r4 == 2, 1.0, f_prev)))
        mids = [b * size + (half - 1 if d == 0 else half) for b in range(CHUNK // size)]
        ref = jnp.concatenate([jnp.broadcast_to(cum[m:m + 1, :], (size, HEAD)) for m in mids], axis=0)
        return jnp.exp(-jnp.abs(cum - ref))

    key_lane = lax.broadcasted_iota(jnp.int32, (1, CHUNK), 1)
    fine_levels = 4

    def assemble_scores(d, dots):
        groups = []
        for g in range(CHUNK // 8):
            r = slice(8 * g, 8 * g + 8)
            acc = mk_ref[d, 0, r, :] * dots[0][r]
            for lvl in range(1, fine_levels):
                acc = acc + mk_ref[d, lvl, r, :] * dots[lvl][r]
            for lvl in range(fine_levels, N_LEVELS):
                size = 2 ** lvl
                half = size // 2
                base = (8 * g // size) * size
                on_query_side = (8 * g - base >= half) if d == 0 else (8 * g - base < half)
                if on_query_side:
                    lo = base if d == 0 else base + half
                    acc = jnp.where((key_lane >= lo) & (key_lane < lo + half), dots[lvl][r], acc)
            groups.append(acc)
        return jnp.concatenate(groups, axis=0)

    def phase_a(q_ref, f_refs, i_ref, off, n):
        par = _largest_divisor(n // CHUNK, (4, 2, 1))

        def body(c, carry):
            chains = []
            for u in range(par):
                ci = c * par + u
                rows = pl.ds(pl.multiple_of(ci * CHUNK, CHUNK), CHUNK)
                srows = pl.ds(pl.multiple_of(off + ci * CHUNK, CHUNK), CHUNK)
                q = _silu(q_ref[0, rows, :])
                v = i_ref[0, rows, :].astype(BF16)
                for d in range(2):
                    lb = lbs[d]
                    open_part = (1.0 - lb) * jax.nn.sigmoid(f_refs[d][0, rows, :])
                    f = lb + open_part
                    k = (1.0 - lb) - open_part
                    logf = jnp.log(f)
                    hi = logf.astype(BF16)
                    mid = (logf - hi.astype(F32)).astype(BF16)
                    res = _dot(lmat_ref[d], jnp.concatenate([hi, mid], axis=1))
                    chains.append(dict(d=d, q=q, v=v, k=k, f=f, res=res, srows=srows,
                                       chunk=off // CHUNK + ci))
            for ch in chains:
                d, q, k = ch["d"], ch["q"], ch["k"]
                cum = ch["res"][:, :HEAD] + ch["res"][:, HEAD:]
                total = cum[CHUNK - 1:CHUNK, :] if d == 0 else cum[0:1, :]
                qb, kb = q.astype(BF16), k.astype(BF16)
                dots = [_dot_nt(qb, kb)]
                for lvl in range(1, N_LEVELS):
                    xb = level_factor(2 ** lvl, d, ch["f"], cum).astype(BF16)
                    dots.append(_dot_nt(qb * xb, kb * xb))
                ch["dots"] = dots
                qd_s[d, ch["srows"], :] = (q * jnp.exp(cum)).astype(BF16)
                ch["u"] = _dot_tn(ch["v"], (k * jnp.exp(total - cum)).astype(BF16))
                drow = pl.multiple_of(ch["chunk"] * 8, 8)
                dec_s[d, pl.ds(drow, 8), :] = jnp.broadcast_to(jnp.exp(total), (8, HEAD))
            for ch in chains:
                d = ch["d"]
                ch["y"] = _dot(assemble_scores(d, ch["dots"]).astype(BF16), ch["v"])
                u_s[d, pl.ds(pl.multiple_of(ch["chunk"] * HEAD, HEAD), HEAD), :] = ch["u"]
            for ch in chains:
                y_s[ch["d"], ch["srows"], :] = ch["y"]
            return carry
        lax.fori_loop(0, n // CHUNK // par, body, 0)

    phase_a(qc_ref, (ffc_ref, fbc_ref), ic_ref, 0, n_ctx)
    phase_a(ql_ref, (ffl_ref, fbl_ref), il_ref, n_ctx, n_lat)

    st_s[...] = jnp.zeros_like(st_s)

    steps = _largest_divisor(n_chunks, (4, 2, 1))

    def phase_b(jj, carry):
        states = [st_s[0], st_s[1]]
        for u in range(steps):
            j = jj * steps + u
            for d in range(2):
                r0 = pl.multiple_of(_scan_row(j, d == 1, n_ctx_chunks, n_chunks), CHUNK)
                rows = pl.ds(r0, CHUNK)
                chunk = r0 // CHUNK
                inter = _dot_nt(qd_s[d, rows, :], states[d].astype(BF16))
                dec = dec_s[d, pl.ds(pl.multiple_of(chunk * 8, 8), 1), :]
                y_s[d, rows, :] = y_s[d, rows, :] + inter
                states[d] = states[d] * dec + u_s[d, pl.ds(pl.multiple_of(chunk * HEAD, HEAD), HEAD), :]
        st_s[0] = states[0]
        st_s[1] = states[1]
        return carry

    lax.fori_loop(0, n_chunks // steps, phase_b, 0)

    def readout(gate_ref, o_ref, off, n):
        par = _largest_divisor(n // CHUNK, (4, 2, 1))

        def body(c, carry):
            for u in range(par):
                ci = c * par + u
                rows = pl.ds(pl.multiple_of(ci * CHUNK, CHUNK), CHUNK)
                srows = pl.ds(pl.multiple_of(off + ci * CHUNK, CHUNK), CHUNK)
                y = y_s[0, srows, :] + y_s[1, srows, :]
                ms = jnp.mean(y * y, axis=-1, keepdims=True)
                o_ref[0, rows, :] = (y * lax.rsqrt(ms + EPS) * _silu(gate_ref[0, rows, :])).astype(o_ref.dtype)
            return carry
        lax.fori_loop(0, n // CHUNK // par, body, 0)

    readout(gc_ref, oc_ref, 0, n_ctx)
    readout(gl_ref, ol_ref, n_ctx, n_lat)


def _hgrn2(p_ctx, p_lat, lb_logits, consts, layer):
    bsz, n_ctx, _ = p_ctx.shape
    n_lat = p_lat.shape[1]
    t = n_ctx + n_lat
    lmat, mk = consts
    depth = lb_logits.shape[0]
    lbl = lb_logits.reshape(depth, 2, A_HEADS, 1, HEAD).transpose(0, 1, 3, 2, 4).reshape(depth, 2, 1, A_HEADS * HEAD)

    def col(n, c0):
        return pl.BlockSpec((1, n, HEAD), lambda b, h, c0=c0: (b, 0, c0 + h))

    cols = (COL_A_Q, COL_A_FF, COL_A_FB, COL_A_I, COL_A_GATE)
    return pl.pallas_call(
        functools.partial(_hgrn2_kernel, layer=layer),
        grid=(bsz, A_HEADS),
        in_specs=[
            pl.BlockSpec((depth, 2, 1, HEAD), lambda b, h: (0, 0, 0, h)),
            pl.BlockSpec(lmat.shape, lambda b, h: (0, 0, 0)),
            pl.BlockSpec(mk.shape, lambda b, h: (0, 0, 0, 0)),
        ] + [col(n_ctx, c) for c in cols] + [col(n_lat, c) for c in cols],
        out_specs=[
            pl.BlockSpec((1, n_ctx, HEAD), lambda b, h: (b, 0, h)),
            pl.BlockSpec((1, n_lat, HEAD), lambda b, h: (b, 0, h)),
        ],
        out_shape=[
            jax.ShapeDtypeStruct((bsz, n_ctx, A_HEADS * HEAD), BF16),
            jax.ShapeDtypeStruct((bsz, n_lat, A_HEADS * HEAD), BF16),
        ],
        scratch_shapes=[
            pltpu.VMEM((2, t, HEAD), BF16),
            pltpu.VMEM((2, t, HEAD), F32),
            pltpu.VMEM((2, (t // CHUNK) * HEAD, HEAD), F32),
            pltpu.VMEM((2, (t // CHUNK) * 8, HEAD), F32),
            pltpu.VMEM((2, HEAD, HEAD), F32),
        ],
        compiler_params=_params(("arbitrary", "arbitrary")),
        name="hgrn2",
    )(lbl, lmat, mk, *([p_ctx] * 5), *([p_lat] * 5))


def _mlstm_constants():
    n = CHUNK
    a = np.arange(n)[:, None]
    b = np.arange(n)[None, :]
    lt, rhs, causal = [], [], []
    for reverse in (False, True):
        upto = (b >= a) if reverse else (b <= a)
        after = (b < a) if reverse else (b > a)
        lt.append(np.stack([upto, after]))
        m = after.T
        ext = np.concatenate([m, np.zeros((n, HEAD - n)), np.ones((n, HEAD))], axis=1)
        rhs.append(np.concatenate([ext, ext], axis=0))
        causal.append(upto)
    return (jnp.asarray(np.stack(lt), F32), jnp.asarray(np.stack(rhs), BF16),
            jnp.asarray(np.stack(causal), F32))


def _mlstm_kernel(lt_ref, rhs_ref, causal_ref, eye_ref, cw_ref, bias_ref,
                  qc_ref, kc_ref, vc_ref, oc_ref, zc_ref, gc_ref,
                  ql_ref, kl_ref, vl_ref, ol_ref, zl_ref, gl_ref,
                  outc_ref, outl_ref,
                  q_s, s_s, cols_s, u_s, sc_s, st_s):
    h = pl.program_id(1)
    n_ctx = qc_ref.shape[1]
    n_lat = ql_ref.shape[1]
    n_ctx_chunks = n_ctx // CHUNK
    n_chunks = (n_ctx + n_lat) // CHUNK
    row_id = lax.broadcasted_iota(jnp.int32, (CHUNK, 1), 0)
    lane = lax.broadcasted_iota(jnp.int32, (1, HEAD), 1)
    sub8 = lax.broadcasted_iota(jnp.int32, (8, 1), 0)
    gate_row = lax.broadcasted_iota(jnp.int32, (N_GATES, 1), 0)
    eye_c = (row_id == lax.broadcasted_iota(jnp.int32, (1, CHUNK), 1)).astype(F32)
    ones_b = jnp.ones((CHUNK, HEAD), BF16)

    def conv_silu(src, ci, n, w):
        s = pl.multiple_of(ci * CHUNK, CHUNK)
        cur = src[0, pl.ds(s, CHUNK), :]
        up = src[0, pl.ds(pl.multiple_of(jnp.maximum(s - 8, 0), 8), 8), :]
        dn = src[0, pl.ds(pl.multiple_of(jnp.minimum(s + CHUNK, n - 8), 8), 8), :]
        prev_row = jnp.where(s > 0, up[7:8, :], 0.0)
        next_row = jnp.where(s + CHUNK < n, dn[0:1, :], 0.0)
        prev = jnp.where(row_id == 0, prev_row, pltpu.roll(cur, 1, 0))
        nxt = jnp.where(row_id == CHUNK - 1, next_row, pltpu.roll(cur, CHUNK - 1, 0))
        return _silu(w[0:1, :] * prev + w[1:2, :] * cur + w[2:3, :] * nxt)

    def pick_row(a, idx):
        return jnp.sum(jnp.where(gate_row == idx, a, 0.0), axis=0, keepdims=True)

    def phase_a(q_ref, k_ref, v_ref, g_ref, off, n):
        par = _largest_divisor(n // CHUNK, (4, 2, 1))

        def body(c, carry):
            chunks = []
            for u in range(par):
                ci = c * par + u
                rows = pl.ds(pl.multiple_of(ci * CHUNK, CHUNK), CHUNK)
                srows = pl.ds(pl.multiple_of(off + ci * CHUNK, CHUNK), CHUNK)
                gt = _dot3_nt(eye_ref[...], g_ref[0, rows, :]) + bias_ref[:, 0:CHUNK]
                qb = (conv_silu(q_ref, ci, n, cw_ref[0]) * (HEAD ** -0.5)).astype(BF16)
                k = conv_silu(k_ref, ci, n, cw_ref[1])
                q_s[srows, :] = qb
                vaug = jnp.concatenate([v_ref[0, rows, :].astype(BF16), ones_b], axis=1)
                chunks.append(dict(k=k, qk=_dot_nt(qb, k.astype(BF16)), vaug=vaug, gt=gt, srows=srows,
                                   chunk=off // CHUNK + ci))
            chains = []
            for ck in chunks:
                for d in range(2):
                    gi = 2 * B_HEADS * d + h
                    li_row = pick_row(ck["gt"], gi)
                    lf_row = _log_sigmoid(pick_row(ck["gt"], gi + B_HEADS))
                    lhs = jnp.concatenate([lt_ref[d, 0] * lf_row,
                                           lt_ref[d, 1] * lf_row + eye_c * li_row], axis=0)
                    hi = lhs.astype(BF16)
                    mid = (lhs - hi.astype(F32)).astype(BF16)
                    out = _dot(jnp.concatenate([hi, mid], axis=1), rhs_ref[d])
                    chains.append(dict(ck, d=d, li_row=li_row, out=out))
            for ch in chains:
                d, out = ch["d"], ch["out"]
                cum_rep = out[0:CHUNK, HEAD:]
                logw_rep = out[CHUNK:, HEAD:]
                log_d = jnp.where(causal_ref[d] > 0.0, out[0:CHUNK, 0:CHUNK] + ch["li_row"], NEG)
                a_loc = jnp.max(log_d, axis=1, keepdims=True)
                s_s[d, ch["srows"], :] = (ch["qk"] * jnp.exp(log_d - a_loc)).astype(BF16)
                cols_s[d, ch["srows"], :] = jnp.where(lane < CHUNK, a_loc, cum_rep)
                total = cum_rep[CHUNK - 1:CHUNK, :] if d == 0 else cum_rep[0:1, :]
                m_loc = jnp.max(logw_rep, axis=0, keepdims=True)
                kw = (ch["k"] * jnp.exp(logw_rep - m_loc)).astype(BF16)
                ch["u"] = _dot_tn(kw, ch["vaug"])
                sc_s[d, pl.ds(pl.multiple_of(ch["chunk"] * 8, 8), 8), :] = jnp.where(
                    sub8 == 0, total, jnp.where(sub8 == 1, m_loc, 0.0))
            for ch in chains:
                u_s[ch["d"], pl.ds(pl.multiple_of(ch["chunk"] * HEAD, HEAD), HEAD), :] = ch["u"].astype(BF16)
            return carry
        lax.fori_loop(0, n // CHUNK // par, body, 0)

    phase_a(qc_ref, kc_ref, vc_ref, gc_ref, 0, n_ctx)
    phase_a(ql_ref, kl_ref, vl_ref, gl_ref, n_ctx, n_lat)

    st_s[...] = jnp.zeros_like(st_s)
    steps = _largest_divisor(n_chunks, (4, 2, 1))
    for d in range(2):
        def phase_b(jj, m, d=d):
            for u in range(steps):
                chunk = _scan_row(jj * steps + u, d == 1, n_ctx_chunks, n_chunks) // CHUNK
                srow = pl.ds(pl.multiple_of(chunk * 8, 8), 8)
                urow = pl.ds(pl.multiple_of(chunk * HEAD, HEAD), HEAD)
                sc = sc_s[d, srow, :]
                total, m_loc = sc[0:1, :], sc[1:2, :]
                m_new = jnp.maximum(total + m, m_loc)
                state = st_s[d]
                inc = u_s[d, urow, :].astype(F32)
                u_s[d, urow, :] = state.astype(BF16)
                sc_s[d, srow, :] = jnp.where(sub8 == 2, m, sc)
                keep = jnp.exp(total + m - m_new)
                gain = jnp.exp(m_loc - m_new)
                st_s[d] = (jnp.concatenate([keep, keep], axis=1) * state
                           + jnp.concatenate([gain, gain], axis=1) * inc)
                m = m_new
            return m
        lax.fori_loop(0, n_chunks // steps, phase_b, jnp.zeros((1, HEAD), F32))

    def phase_c(v_ref, o_ref, z_ref, out_ref, off, n):
        par = _largest_divisor(n // CHUNK, (4, 2, 1))

        def body(c, carry):
            chains = []
            for u in range(par):
                ci = c * par + u
                rows = pl.ds(pl.multiple_of(ci * CHUNK, CHUNK), CHUNK)
                srows = pl.ds(pl.multiple_of(off + ci * CHUNK, CHUNK), CHUNK)
                chunk = off // CHUNK + ci
                qb = q_s[srows, :]
                vaug = jnp.concatenate([v_ref[0, rows, :].astype(BF16), ones_b], axis=1)
                for d in range(2):
                    intra = _dot(s_s[d, srows, :], vaug)
                    inter = _dot(qb, u_s[d, pl.ds(pl.multiple_of(chunk * HEAD, HEAD), HEAD), :])
                    chains.append(dict(d=d, u=u, rows=rows, srows=srows, chunk=chunk, intra=intra, inter=inter))
            ys = [None] * par
            for ch in chains:
                d = ch["d"]
                cols = cols_s[d, ch["srows"], :]
                swapped = pltpu.roll(cols, CHUNK, 1)
                a_rep = jnp.where(lane < CHUNK, cols, swapped)
                cum_rep = jnp.where(lane < CHUNK, swapped, cols)
                m = sc_s[d, pl.ds(pl.multiple_of(ch["chunk"] * 8, 8), 8), :][2:3, :]
                inter_l = cum_rep + m
                m_t = jnp.maximum(a_rep, inter_l)
                wi = jnp.exp(a_rep - m_t)
                we = jnp.exp(inter_l - m_t)
                num = wi * ch["intra"][:, :HEAD] + we * ch["inter"][:, :HEAD]
                den = wi * ch["intra"][:, HEAD:] + we * ch["inter"][:, HEAD:]
                hd = num / jnp.maximum(jnp.abs(den), jnp.exp(-m_t))
                ys[ch["u"]] = hd if ys[ch["u"]] is None else ys[ch["u"]] + hd
                if d == 1:
                    rows = ch["rows"]
                    hh = jax.nn.sigmoid(o_ref[0, rows, :]) * ys[ch["u"]]
                    ms = jnp.mean(hh * hh, axis=-1, keepdims=True)
                    out_ref[0, rows, :] = (hh * lax.rsqrt(ms + EPS) * _silu(z_ref[0, rows, :])).astype(out_ref.dtype)
            return carry
        lax.fori_loop(0, n // CHUNK // par, body, 0)

    phase_c(vc_ref, oc_ref, zc_ref, outc_ref, 0, n_ctx)
    phase_c(vl_ref, ol_ref, zl_ref, outl_ref, n_ctx, n_lat)


def _mlstm(p_ctx, p_lat, g_ctx, g_lat, conv_w, gate_bias, consts):
    bsz, n_ctx, _ = p_ctx.shape
    n_lat = p_lat.shape[1]
    t = n_ctx + n_lat
    n_chunks = t // CHUNK
    lt, rhs, causal = consts
    eye = jnp.eye(N_GATES, HEAD, dtype=BF16)
    cw = conv_w.reshape(3, 2, B_HEADS * HEAD).transpose(1, 0, 2)
    bias = jnp.broadcast_to(gate_bias.reshape(N_GATES, 1), (N_GATES, HEAD))

    def col(n, c0):
        return pl.BlockSpec((1, n, HEAD), lambda b, h, c0=c0: (b, 0, c0 + h))

    def gate_spec(n):
        return pl.BlockSpec((1, n, HEAD), lambda b, h: (b, 0, 0))

    cols = (COL_B_Q, COL_B_K, COL_B_V, COL_B_O, COL_B_Z)
    return pl.pallas_call(
        _mlstm_kernel,
        grid=(bsz, B_HEADS),
        in_specs=[
            pl.BlockSpec(lt.shape, lambda b, h: (0, 0, 0, 0)),
            pl.BlockSpec(rhs.shape, lambda b, h: (0, 0, 0)),
            pl.BlockSpec(causal.shape, lambda b, h: (0, 0, 0)),
            pl.BlockSpec(eye.shape, lambda b, h: (0, 0)),
            pl.BlockSpec((2, 3, HEAD), lambda b, h: (0, 0, h)),
            pl.BlockSpec((N_GATES, HEAD), lambda b, h: (0, 0)),
        ] + [col(n_ctx, c) for c in cols] + [gate_spec(n_ctx)]
          + [col(n_lat, c) for c in cols] + [gate_spec(n_lat)],
        out_specs=[
            pl.BlockSpec((1, n_ctx, HEAD), lambda b, h: (b, 0, h)),
            pl.BlockSpec((1, n_lat, HEAD), lambda b, h: (b, 0, h)),
        ],
        out_shape=[
            jax.ShapeDtypeStruct((bsz, n_ctx, B_HEADS * HEAD), BF16),
            jax.ShapeDtypeStruct((bsz, n_lat, B_HEADS * HEAD), BF16),
        ],
        scratch_shapes=[
            pltpu.VMEM((t, HEAD), BF16),
            pltpu.VMEM((2, t, CHUNK), BF16),
            pltpu.VMEM((2, t, HEAD), F32),
            pltpu.VMEM((2, n_chunks * HEAD, 2 * HEAD), BF16),
            pltpu.VMEM((2, n_chunks * 8, HEAD), F32),
            pltpu.VMEM((2, HEAD, 2 * HEAD), F32),
        ],
        compiler_params=_params(("arbitrary", "arbitrary")),
        name="mlstm",
    )(lt, rhs, causal, eye, cw, bias, *([p_ctx] * 5), g_ctx, *([p_lat] * 5), g_lat)


def _rope(x, cos, sin_lo, sin_hi):
    return x * cos + pltpu.roll(x, HEAD - 32, 1) * sin_lo + pltpu.roll(x, 32, 1) * sin_hi


LOG2E = 1.4426950408889634


def _softmax_attend(sink_col, scores, values):
    m = jnp.maximum(sink_col, jnp.max(scores, axis=1, keepdims=True))
    p = jnp.exp2(scores - m)
    den = jnp.exp2(sink_col - m) + jnp.sum(p, axis=1, keepdims=True)
    return _dot(p.astype(BF16), values) / den


def _sink_column(sink_ref, kv, rows_per_head):
    row = lax.broadcasted_iota(jnp.int32, (C_GROUP * rows_per_head, 1), 0)
    col = jnp.zeros((C_GROUP * rows_per_head, 1), F32)
    for g in range(C_GROUP):
        col = jnp.where(row // rows_per_head == g, sink_ref[kv * C_GROUP + g] * LOG2E, col)
    return col


def _window_masks():
    r = np.arange(ATTN_BLOCK)[:, None]
    c = np.arange(3 * ATTN_BLOCK)[None, :]
    return jnp.asarray(np.stack([np.abs(c - p * ATTN_BLOCK - r) <= WINDOW for p in range(3)]), F32)


def _attn_kernel(sink_ref, mask_ref, cos_ref, slo_ref, shi_ref, q_ref, k_ref, v_ref, kc_ref, vc_ref, z_ref,
                 o_ref, kr_s):
    kv = pl.program_id(1)
    i = pl.program_id(2)
    n = k_ref.shape[1]
    span = 3 * ATTN_BLOCK

    @pl.when(i == 0)
    def _():
        def body(c, carry):
            rows = pl.ds(pl.multiple_of(c * ATTN_BLOCK, ATTN_BLOCK), ATTN_BLOCK)
            kr_s[rows, :] = _rope(k_ref[0, rows, :], cos_ref[rows, :], slo_ref[rows, :],
                                  shi_ref[rows, :]).astype(BF16)
            return carry
        lax.fori_loop(0, n // ATTN_BLOCK, body, 0)

    qrows = pl.ds(pl.multiple_of(i * ATTN_BLOCK, ATTN_BLOCK), ATTN_BLOCK)
    cos = cos_ref[qrows, :]
    slo = slo_ref[qrows, :]
    shi = shi_ref[qrows, :]
    scale = HEAD ** -0.5 * LOG2E
    start = pl.multiple_of(jnp.clip((i - 1) * ATTN_BLOCK, 0, n - span), ATTN_BLOCK)
    keys = jnp.concatenate([kr_s[pl.ds(start, span), :], kc_ref[0].astype(BF16)], axis=0)
    values = jnp.concatenate([v_ref[0, pl.ds(start, span), :].astype(BF16), vc_ref[0].astype(BF16)], axis=0)
    valid = mask_ref[(i * ATTN_BLOCK - start) // ATTN_BLOCK] > 0.0
    scores = []
    for g in range(C_GROUP):
        qg = (_rope(q_ref[0, :, g * HEAD:(g + 1) * HEAD], cos, slo, shi) * scale).astype(BF16)
        s = _dot_nt(qg, keys)
        scores.append(jnp.concatenate([jnp.where(valid, s[:, :span], NEG), s[:, span:]], axis=1))
    for g in range(C_GROUP):
        sink = jnp.full((ATTN_BLOCK, 1), sink_ref[kv * C_GROUP + g] * LOG2E, F32)
        out = _softmax_attend(sink, scores[g], values)
        zg = z_ref[0, :, g * HEAD:(g + 1) * HEAD]
        o_ref[0, :, g * HEAD:(g + 1) * HEAD] = (out * _silu(zg)).astype(o_ref.dtype)


def _attention(p_ctx, p_lat, sink, rope):
    bsz, n_ctx, _ = p_ctx.shape
    n = p_lat.shape[1]
    cos, slo, shi = rope
    masks = _window_masks()
    gw = C_GROUP * HEAD
    gblk = C_GROUP

    full = pl.BlockSpec((n, HEAD), lambda b, kv, i: (0, 0))
    return pl.pallas_call(
        _attn_kernel,
        grid=(bsz, C_KV_HEADS, n // ATTN_BLOCK),
        in_specs=[
            pl.BlockSpec(memory_space=pltpu.SMEM),
            pl.BlockSpec(masks.shape, lambda b, kv, i: (0, 0, 0)),
            full, full, full,
            pl.BlockSpec((1, ATTN_BLOCK, gw), lambda b, kv, i: (b, i, COL_C_Q // gblk + kv)),
            pl.BlockSpec((1, n, HEAD), lambda b, kv, i: (b, 0, COL_C_K + kv)),
            pl.BlockSpec((1, n, HEAD), lambda b, kv, i: (b, 0, COL_C_V + kv)),
            pl.BlockSpec((1, n_ctx, HEAD), lambda b, kv, i: (b, 0, COL_C_K + kv)),
            pl.BlockSpec((1, n_ctx, HEAD), lambda b, kv, i: (b, 0, COL_C_V + kv)),
            pl.BlockSpec((1, ATTN_BLOCK, gw), lambda b, kv, i: (b, i, COL_C_Z // gblk + kv)),
        ],
        out_specs=pl.BlockSpec((1, ATTN_BLOCK, gw), lambda b, kv, i: (b, i, kv)),
        out_shape=jax.ShapeDtypeStruct((bsz, n, C_Q_HEADS * HEAD), BF16),
        scratch_shapes=[pltpu.VMEM((n, HEAD), BF16)],
        compiler_params=_params(("arbitrary", "arbitrary", "arbitrary")),
        name="window_attention",
    )(sink, masks, cos, slo, shi, p_lat, p_lat, p_lat, p_ctx, p_ctx, p_lat)


def _ctx_attn_kernel(sink_ref, q_ref, k_ref, v_ref, z_ref, o_ref):
    kv = pl.program_id(1)
    n_ctx = q_ref.shape[1]
    scale = HEAD ** -0.5 * LOG2E
    q4 = jnp.concatenate([q_ref[0, :, g * HEAD:(g + 1) * HEAD] * scale for g in range(C_GROUP)],
                         axis=0).astype(BF16)
    s = _dot_nt(q4, k_ref[0].astype(BF16))
    out = _softmax_attend(_sink_column(sink_ref, kv, n_ctx), s, v_ref[0].astype(BF16))
    for g in range(C_GROUP):
        zg = z_ref[0, :, g * HEAD:(g + 1) * HEAD]
        o_ref[0, :, g * HEAD:(g + 1) * HEAD] = (out[g * n_ctx:(g + 1) * n_ctx] * _silu(zg)).astype(o_ref.dtype)


def _context_attention(p_ctx, sink):
    bsz, n_ctx, _ = p_ctx.shape
    gw = C_GROUP * HEAD
    gblk = C_GROUP
    return pl.pallas_call(
        _ctx_attn_kernel,
        grid=(bsz, C_KV_HEADS),
        in_specs=[
            pl.BlockSpec(memory_space=pltpu.SMEM),
            pl.BlockSpec((1, n_ctx, gw), lambda b, kv: (b, 0, COL_C_Q // gblk + kv)),
            pl.BlockSpec((1, n_ctx, HEAD), lambda b, kv: (b, 0, COL_C_K + kv)),
            pl.BlockSpec((1, n_ctx, HEAD), lambda b, kv: (b, 0, COL_C_V + kv)),
            pl.BlockSpec((1, n_ctx, gw), lambda b, kv: (b, 0, COL_C_Z // gblk + kv)),
        ],
        out_specs=pl.BlockSpec((1, n_ctx, gw), lambda b, kv: (b, 0, kv)),
        out_shape=jax.ShapeDtypeStruct((bsz, n_ctx, C_Q_HEADS * HEAD), BF16),
        compiler_params=_params(("arbitrary", "arbitrary")),
        name="context_attention",
    )(sink, p_ctx, p_ctx, p_ctx, p_ctx)


def _rope_tables(n):
    axis_dim = HEAD // 2
    rows = n // GRID_W
    row = jnp.repeat(jnp.arange(rows, dtype=F32), GRID_W)
    col = jnp.tile(jnp.arange(GRID_W, dtype=F32), rows)
    inv_freq = ROPE_BASE ** (-jnp.arange(0, axis_dim, 2, dtype=F32) / axis_dim)
    ang_r = row[:, None] * inv_freq[None, :]
    ang_c = col[:, None] * inv_freq[None, :]
    zero = jnp.zeros_like(ang_r)
    cos = jnp.concatenate([jnp.cos(ang_r)] * 2 + [jnp.cos(ang_c)] * 2, axis=-1)
    sin_lo = jnp.concatenate([-jnp.sin(ang_r), zero, -jnp.sin(ang_c), zero], axis=-1)
    sin_hi = jnp.concatenate([zero, jnp.sin(ang_r), zero, jnp.sin(ang_c)], axis=-1)
    return cos, sin_lo, sin_hi


def _pick_tile(n, candidates):
    for c in candidates:
        if n % c == 0:
            return c
    raise ValueError(f"no tile for {n}")


def kernel(x, c, ctx, c_ctx, w_mod, b_mod, g_pre, g_post, w_in, hgrn_lb_logits, mlstm_conv_w,
           mlstm_gate_bias, attn_sink, w_out):
    bsz, n_lat, d = x.shape
    n_ctx = ctx.shape[1]
    depth = w_mod.shape[0]
    assert n_ctx % CHUNK == 0 and n_lat % ATTN_BLOCK == 0 and n_lat >= 3 * ATTN_BLOCK

    cond = jnp.concatenate([c, c_ctx[None, :]], axis=0)
    pad = (-cond.shape[0]) % 8
    cond = jnp.pad(cond, ((0, pad), (0, 0)))
    mod = _modulation(cond, w_mod, b_mod)

    consts = _scan_constants()
    mlstm_consts = _mlstm_constants()
    rope = _rope_tables(n_lat)

    tm_lat = _pick_tile(n_lat, (1024, 512, 256, 128))
    tm_ctx = _pick_tile(bsz * n_ctx, (1024, 512, 256, 128))
    tn = _pick_tile(MAIN_WIDTH, (1280, 768, 512))
    to_lat = _pick_tile(n_lat, (512, 256, 128))
    to_ctx = _pick_tile(bsz * n_ctx, (512, 256, 128))

    x2d = x.reshape(bsz * n_lat, d)
    h2d = ctx.reshape(bsz * n_ctx, d)
    for layer in range(depth):
        with_ctx = layer < depth - 1
        shift_l = mod[layer, :bsz, 0:d].reshape(bsz, 1, d)
        scale_l = mod[layer, :bsz, d:2 * d].reshape(bsz, 1, d)
        gate_l = mod[layer, :bsz, 2 * d:].reshape(bsz, 1, d)
        shift_c = mod[layer, bsz:bsz + 1, 0:d].reshape(1, 1, d)
        scale_c = mod[layer, bsz:bsz + 1, d:2 * d].reshape(1, 1, d)
        gate_c = mod[layer, bsz:bsz + 1, 2 * d:].reshape(1, 1, d)

        w = w_in[layer]
        w_main = jnp.concatenate([w[:, :GATES_OFFSET], w[:, GATES_OFFSET + N_GATES:]], axis=1).astype(BF16)
        w_gates = jnp.pad(w[:, GATES_OFFSET:GATES_OFFSET + N_GATES], ((0, 0), (0, HEAD - N_GATES))).astype(BF16)
        gain_pre = g_pre[layer].reshape(1, d)
        gain_post = g_post[layer].reshape(1, d)
        wo = w_out[layer].astype(BF16)

        p_lat, g_lat = _in_projection(x2d, shift_l, scale_l, gain_pre, w_main, w_gates, n_lat, tm_lat, tn)
        p_ctx, g_ctx = _in_projection(h2d, shift_c, scale_c, gain_pre, w_main, w_gates, bsz * n_ctx, tm_ctx, tn)
        p_lat = p_lat.reshape(bsz, n_lat, MAIN_WIDTH)
        p_ctx = p_ctx.reshape(bsz, n_ctx, MAIN_WIDTH)
        g_lat = g_lat.reshape(bsz, n_lat, HEAD)
        g_ctx = g_ctx.reshape(bsz, n_ctx, HEAD)

        a_c, a_l = _hgrn2(p_ctx, p_lat, hgrn_lb_logits, consts, layer)
        b_c, b_l = _mlstm(p_ctx, p_lat, g_ctx, g_lat, mlstm_conv_w[layer], mlstm_gate_bias[layer],
                          mlstm_consts)
        c_l = _attention(p_ctx, p_lat, attn_sink[layer], rope)

        x2d = _out_projection(a_l.reshape(bsz * n_lat, -1), b_l.reshape(bsz * n_lat, -1),
                              c_l.reshape(bsz * n_lat, -1), wo, x2d, gate_l, gain_post, n_lat, to_lat)
        if with_ctx:
            c_c = _context_attention(p_ctx, attn_sink[layer])
            h2d = _out_projection(a_c.reshape(bsz * n_ctx, -1), b_c.reshape(bsz * n_ctx, -1),
                                  c_c.reshape(bsz * n_ctx, -1), wo, h2d, gate_c, gain_post,
                                  bsz * n_ctx, to_ctx)
    return x2d.reshape(bsz, n_lat, d)
```

```python
import functools

import numpy as np
import jax
import jax.numpy as jnp
from jax import lax
from jax.experimental import pallas as pl
from jax.experimental.pallas import tpu as pltpu

F32 = jnp.float32
BF16 = jnp.bfloat16

EPS = 1e-6
NEG = -1e30
HEAD = 128
A_HEADS = 4
B_HEADS = 4
C_Q_HEADS = 8
C_KV_HEADS = 2
C_GROUP = C_Q_HEADS // C_KV_HEADS
CHUNK = 64
ATTN_BLOCK = 128
WINDOW = 128
GRID_W = 64
ROPE_BASE = 10000.0
N_GATES = 4 * B_HEADS

COL_A_Q, COL_A_FF, COL_A_FB, COL_A_I, COL_A_GATE = 0, 4, 8, 12, 16
COL_B_Q, COL_B_K, COL_B_V, COL_B_O, COL_B_Z = 20, 24, 28, 32, 36
COL_C_Q, COL_C_K, COL_C_V, COL_C_Z = 40, 48, 50, 52
MAIN_BLOCKS = 60
MAIN_WIDTH = MAIN_BLOCKS * HEAD
GATES_OFFSET = 9 * 4 * HEAD

VMEM_LIMIT = 56 * 1024 * 1024


def _params(sem):
    return pltpu.CompilerParams(dimension_semantics=sem, vmem_limit_bytes=VMEM_LIMIT)


def _silu(x):
    return x * jax.nn.sigmoid(x)


def _log_sigmoid(x):
    return jnp.minimum(x, 0.0) - jnp.log(1.0 + jnp.exp(-jnp.abs(x)))


def _dot(a, b):
    return jnp.dot(a, b, preferred_element_type=F32)


def _dot_nt(a, b):
    return lax.dot_general(a, b, (((1,), (1,)), ((), ())), preferred_element_type=F32)


def _dot_tn(a, b):
    return lax.dot_general(a, b, (((0,), (0,)), ((), ())), preferred_element_type=F32)


def _split3(x):
    hi = x.astype(BF16)
    r = x - hi.astype(F32)
    mid = r.astype(BF16)
    lo = (r - mid.astype(F32)).astype(BF16)
    return hi, mid, lo


def _dot3_nt(w, x):
    hi, mid, lo = _split3(x)
    return _dot_nt(w, hi) + _dot_nt(w, mid) + _dot_nt(w, lo)


def _mod_kernel(c_ref, w_ref, b_ref, o_ref):
    a = _silu(c_ref[...])
    o_ref[0] = jnp.dot(a, w_ref[0], preferred_element_type=F32,
                       precision=lax.Precision.HIGHEST) + b_ref[0]


def _modulation(cond, w_mod, b_mod):
    depth, d, n = w_mod.shape
    rows = cond.shape[0]
    tn = 768
    assert n % tn == 0
    return pl.pallas_call(
        _mod_kernel,
        grid=(depth, n // tn),
        in_specs=[
            pl.BlockSpec((rows, d), lambda l, j: (0, 0)),
            pl.BlockSpec((1, d, tn), lambda l, j: (l, 0, j)),
            pl.BlockSpec((1, 1, tn), lambda l, j: (l, 0, j)),
        ],
        out_specs=pl.BlockSpec((1, rows, tn), lambda l, j: (l, 0, j)),
        out_shape=jax.ShapeDtypeStruct((depth, rows, n), F32),
        compiler_params=_params(("arbitrary", "arbitrary")),
        name="modulation",
    )(cond, w_mod, b_mod.reshape(depth, 1, n))


NORM_ROWS = 32
EPILOGUE_ROWS = 8


def _inproj_kernel(x_ref, shift_ref, scale_ref, g_ref, w_ref, wg_ref, p_ref, gates_ref, xn_ref, gm_ref):
    tm = x_ref.shape[0]

    @pl.when(pl.program_id(1) == 0)
    def _():
        gm_ref[...] = g_ref[...] * (1.0 + scale_ref[0])

        def body(r, carry):
            for u in range(2):
                rows = pl.ds(pl.multiple_of((2 * r + u) * NORM_ROWS, NORM_ROWS), NORM_ROWS)
                xr = x_ref[rows, :]
                ms = jnp.mean(xr * xr, axis=-1, keepdims=True)
                xn_ref[rows, :] = (xr * (lax.rsqrt(ms + EPS) * gm_ref[...]) + shift_ref[0]).astype(BF16)
            return carry

        lax.fori_loop(0, tm // NORM_ROWS // 2, body, 0)
        gates_ref[...] = _dot(xn_ref[...], wg_ref[...])

    res = _dot(xn_ref[...], w_ref[...])
    for t in range(p_ref.shape[0]):
        p_ref[t] = res[:, t * HEAD:(t + 1) * HEAD]


def _weight_prep_kernel(w_ref, main_ref, gates_ref):
    w = w_ref[0]
    main_ref[0, :, 0:GATES_OFFSET] = w[:, 0:GATES_OFFSET].astype(BF16)
    main_ref[0, :, GATES_OFFSET:] = w[:, GATES_OFFSET + N_GATES:].astype(BF16)
    gates = w[:, GATES_OFFSET:GATES_OFFSET + N_GATES]
    gates_ref[0] = jnp.concatenate(
        [gates, jnp.zeros((gates.shape[0], HEAD - N_GATES), F32)], axis=1).astype(BF16)


def _weight_prep(w_in):
    depth, d, n = w_in.shape
    tr = _pick_tile(d, (256, 128, 64, 32, 16))
    return pl.pallas_call(
        _weight_prep_kernel,
        grid=(depth, d // tr),
        in_specs=[pl.BlockSpec((1, tr, n), lambda l, i: (l, i, 0))],
        out_specs=[
            pl.BlockSpec((1, tr, MAIN_WIDTH), lambda l, i: (l, i, 0)),
            pl.BlockSpec((1, tr, HEAD), lambda l, i: (l, i, 0)),
        ],
        out_shape=[
            jax.ShapeDtypeStruct((depth, d, MAIN_WIDTH), BF16),
            jax.ShapeDtypeStruct((depth, d, HEAD), BF16),
        ],
        compiler_params=_params(("arbitrary", "arbitrary")),
        name="weight_prep",
    )(w_in)


def _in_projection(x2d, shift, scale, gain, w_main, w_gates, layer, rows_per_mod, tm, tn):
    r, d = x2d.shape
    n = w_main.shape[2]
    assert r % tm == 0 and n % tn == 0 and rows_per_mod % tm == 0
    per = rows_per_mod // tm
    return pl.pallas_call(
        _inproj_kernel,
        grid=(r // tm, n // tn),
        in_specs=[
            pl.BlockSpec((tm, d), lambda i, j: (i, 0)),
            pl.BlockSpec((1, 1, d), lambda i, j: (i // per, 0, 0)),
            pl.BlockSpec((1, 1, d), lambda i, j: (i // per, 0, 0)),
            pl.BlockSpec((1, d), lambda i, j: (0, 0)),
            pl.BlockSpec((None, d, tn), lambda i, j: (layer, 0, j)),
            pl.BlockSpec((None, d, HEAD), lambda i, j: (layer, 0, 0)),
        ],
        out_specs=[
            pl.BlockSpec((tn // HEAD, tm, HEAD), lambda i, j: (j, i, 0)),
            pl.BlockSpec((tm, HEAD), lambda i, j: (i, 0)),
        ],
        out_shape=[
            jax.ShapeDtypeStruct((n // HEAD, r, HEAD), F32),
            jax.ShapeDtypeStruct((r, HEAD), F32),
        ],
        scratch_shapes=[pltpu.VMEM((tm, d), BF16), pltpu.VMEM((1, d), F32)],
        compiler_params=_params(("arbitrary", "arbitrary")),
        name="in_projection",
    )(x2d, shift, scale, gain, w_main, w_gates)


def _outproj_kernel(a_ref, b_ref, c_ref, w_ref, x_ref, gate_ref, g_ref, o_ref, y_s):
    i = pl.program_id(0)
    tm = x_ref.shape[0]
    slot = i % 2

    @pl.when(i == 0)
    def _():
        y_s[1] = jnp.zeros(y_s.shape[1:], F32)

    for r in range(tm // EPILOGUE_ROWS):
        rows = slice(r * EPILOGUE_ROWS, (r + 1) * EPILOGUE_ROWS)
        y = y_s[1 - slot, rows, :]
        ms = jnp.mean(y * y, axis=-1, keepdims=True)
        o_ref[rows, :] = x_ref[rows, :] + y * (lax.rsqrt(ms + EPS) * (gate_ref[0] * g_ref[...]))
    y_s[slot] = _dot(jnp.concatenate([a_ref[...], b_ref[...], c_ref[...]], axis=1), w_ref[...])


def _out_projection(a, b, c, w_out, layer, x2d, gate, gain, rows_per_mod, tm):
    r, d = x2d.shape
    assert r % tm == 0 and rows_per_mod % tm == 0
    per = rows_per_mod // tm
    n = r // tm
    cur = lambda i: (jnp.minimum(i, n - 1), 0)
    prev = lambda i: (jnp.maximum(i - 1, 0), 0)
    return pl.pallas_call(
        _outproj_kernel,
        grid=(n + 1,),
        in_specs=[
            pl.BlockSpec((tm, a.shape[1]), cur),
            pl.BlockSpec((tm, b.shape[1]), cur),
            pl.BlockSpec((tm, c.shape[1]), cur),
            pl.BlockSpec((None,) + w_out.shape[1:], lambda i: (layer, 0, 0)),
            pl.BlockSpec((tm, d), prev),
            pl.BlockSpec((1, 1, d), lambda i: (jnp.maximum(i - 1, 0) // per, 0, 0)),
            pl.BlockSpec((1, d), lambda i: (0, 0)),
        ],
        out_specs=pl.BlockSpec((tm, d), prev),
        out_shape=jax.ShapeDtypeStruct((r, d), F32),
        scratch_shapes=[pltpu.VMEM((2, tm, d), F32)],
        compiler_params=_params(("arbitrary",)),
        name="out_projection",
    )(a, b, c, w_out, x2d, gate, gain)


N_LEVELS = 7


def _scan_matrices(reverse):
    n = CHUNK
    t = np.arange(n)[:, None]
    u = np.arange(n)[None, :]
    cum = (u >= t) if reverse else (u <= t)
    masks = [t == u]
    size = 2
    while size <= n:
        half = size // 2
        same = (t // size) == (u // size)
        if reverse:
            masks.append(same & ((t % size) < half) & ((u % size) >= half))
        else:
            masks.append(same & ((t % size) >= half) & ((u % size) < half))
        size *= 2
    return cum.astype(np.float32), np.stack(masks).astype(np.float32)


def _scan_constants():
    cf, mf = _scan_matrices(False)
    cb, mb = _scan_matrices(True)
    return (jnp.asarray(np.stack([cf, cb]), BF16), jnp.asarray(np.stack([mf, mb]), F32))


def _largest_divisor(n, candidates):
    return next(c for c in candidates if n % c == 0)


def _scan_row(j, reverse, n_ctx_chunks, n_chunks):
    if not reverse:
        return j * CHUNK
    ctx_part = (n_ctx_chunks - 1 - j) * CHUNK
    lat_part = (n_ctx_chunks + (n_chunks - 1 - j)) * CHUNK
    return jnp.where(j < n_ctx_chunks, ctx_part, lat_part)


def _hgrn2_kernel(lbl_ref, lmat_ref, mk_ref,
                  qc_ref, ffc_ref, fbc_ref, ic_ref, gc_ref,
                  ql_ref, ffl_ref, fbl_ref, il_ref, gl_ref,
                  oc_ref, ol_ref,
                  qd_s, y_s, u_s, dec_s, st_s, *, layer):
    n_ctx = qc_ref.shape[1]
    n_lat = ql_ref.shape[1]
    n_ctx_chunks = n_ctx // CHUNK
    n_chunks = (n_ctx + n_lat) // CHUNK

    logits = lbl_ref[...]
    e = jnp.exp(logits - jnp.max(logits, axis=0, keepdims=True))
    sm = e / jnp.sum(e, axis=0, keepdims=True)
    lbs = jnp.sum(sm[:layer + 1], axis=0) - sm[0]

    row = lax.broadcasted_iota(jnp.int32, (CHUNK, 1), 0)
    r4 = row % 4

    def level_factor(size, d, f, cum):
        half = size // 2
        if size == 2:
            return jnp.where((row % 2 == 1) if d == 0 else (row % 2 == 0), f, 1.0)
        if size == 4:
            f_next = pltpu.roll(f, CHUNK - 1, 0)
            f_prev = pltpu.roll(f, 1, 0)
            if d == 0:
                return jnp.where(r4 == 0, f_next, jnp.where(r4 == 1, 1.0, jnp.where(r4 == 2, f, f * f_prev)))
            return jnp.where(r4 == 0, f * f_next, jnp.where(r4 == 1, f, jnp.where(r4 == 2, 1.0, f_prev)))
        mids = [b * size + (half - 1 if d == 0 else half) for b in range(CHUNK // size)]
        ref = jnp.concatenate([jnp.broadcast_to(cum[m:m + 1, :], (size, HEAD)) for m in mids], axis=0)
        return jnp.exp(-jnp.abs(cum - ref))

    key_lane = lax.broadcasted_iota(jnp.int32, (1, CHUNK), 1)
    fine_levels = 4

    def assemble_scores(d, dots):
        groups = []
        for g in range(CHUNK // 8):
            r = slice(8 * g, 8 * g + 8)
            acc = mk_ref[d, 0, r, :] * dots[0][r]
            for lvl in range(1, fine_levels):
                acc = acc + mk_ref[d, lvl, r, :] * dots[lvl][r]
            for lvl in range(fine_levels, N_LEVELS):
                size = 2 ** lvl
                half = size // 2
                base = (8 * g // size) * size
                on_query_side = (8 * g - base >= half) if d == 0 else (8 * g - base < half)
                if on_query_side:
                    lo = base if d == 0 else base + half
                    acc = jnp.where((key_lane >= lo) & (key_lane < lo + half), dots[lvl][r], acc)
            groups.append(acc)
        return jnp.concatenate(groups, axis=0)

    def phase_a(q_ref, f_refs, i_ref, off, n):
        par = _largest_divisor(n // CHUNK, (4, 2, 1))

        def body(c, carry):
            chains = []
            for u in range(par):
                ci = c * par + u
                rows = pl.ds(pl.multiple_of(ci * CHUNK, CHUNK), CHUNK)
                srows = pl.ds(pl.multiple_of(off + ci * CHUNK, CHUNK), CHUNK)
                q = _silu(q_ref[0, rows, :])
                v = i_ref[0, rows, :].astype(BF16)
                for d in range(2):
                    lb = lbs[d]
                    open_part = (1.0 - lb) * jax.nn.sigmoid(f_refs[d][0, rows, :])
                    f = lb + open_part
                    k = (1.0 - lb) - open_part
                    logf = jnp.log(f)
                    hi = logf.astype(BF16)
                    mid = (logf - hi.astype(F32)).astype(BF16)
                    res = _dot(lmat_ref[d], jnp.concatenate([hi, mid], axis=1))
                    chains.append(dict(d=d, q=q, v=v, k=k, f=f, res=res, srows=srows,
                                       chunk=off // CHUNK + ci))
            for ch in chains:
                d, q, k = ch["d"], ch["q"], ch["k"]
                cum = ch["res"][:, :HEAD] + ch["res"][:, HEAD:]
                total = cum[CHUNK - 1:CHUNK, :] if d == 0 else cum[0:1, :]
                qb, kb = q.astype(BF16), k.astype(BF16)
                dots = [_dot_nt(qb, kb)]
                for lvl in range(1, N_LEVELS):
                    xb = level_factor(2 ** lvl, d, ch["f"], cum).astype(BF16)
                    dots.append(_dot_nt(qb * xb, kb * xb))
                ch["dots"] = dots
                qd_s[d, ch["srows"], :] = (q * jnp.exp(cum)).astype(BF16)
                ch["u"] = _dot_tn(ch["v"], (k * jnp.exp(total - cum)).astype(BF16))
                drow = pl.multiple_of(ch["chunk"] * 8, 8)
                dec_s[d, pl.ds(drow, 8), :] = jnp.broadcast_to(jnp.exp(total), (8, HEAD))
            for ch in chains:
                d = ch["d"]
                ch["y"] = _dot(assemble_scores(d, ch["dots"]).astype(BF16), ch["v"])
                u_s[d, pl.ds(pl.multiple_of(ch["chunk"] * HEAD, HEAD), HEAD), :] = ch["u"]
            for ch in chains:
                y_s[ch["d"], ch["srows"], :] = ch["y"]
            return carry
        lax.fori_loop(0, n // CHUNK // par, body, 0)

    phase_a(qc_ref, (ffc_ref, fbc_ref), ic_ref, 0, n_ctx)
    phase_a(ql_ref, (ffl_ref, fbl_ref), il_ref, n_ctx, n_lat)

    st_s[...] = jnp.zeros_like(st_s)

    steps = _largest_divisor(n_chunks, (4, 2, 1))

    def phase_b(jj, carry):
        states = [st_s[0], st_s[1]]
        for u in range(steps):
            j = jj * steps + u
            for d in range(2):
                r0 = pl.multiple_of(_scan_row(j, d == 1, n_ctx_chunks, n_chunks), CHUNK)
                rows = pl.ds(r0, CHUNK)
                chunk = r0 // CHUNK
                inter = _dot_nt(qd_s[d, rows, :], states[d].astype(BF16))
                dec = dec_s[d, pl.ds(pl.multiple_of(chunk * 8, 8), 1), :]
                y_s[d, rows, :] = y_s[d, rows, :] + inter
                states[d] = states[d] * dec + u_s[d, pl.ds(pl.multiple_of(chunk * HEAD, HEAD), HEAD), :]
        st_s[0] = states[0]
        st_s[1] = states[1]
        return carry

    lax.fori_loop(0, n_chunks // steps, phase_b, 0)

    def readout(gate_ref, o_ref, off, n):
        par = _largest_divisor(n // CHUNK, (4, 2, 1))

        def body(c, carry):
            for u in range(par):
                ci = c * par + u
                rows = pl.ds(pl.multiple_of(ci * CHUNK, CHUNK), CHUNK)
                srows = pl.ds(pl.multiple_of(off + ci * CHUNK, CHUNK), CHUNK)
                y = y_s[0, srows, :] + y_s[1, srows, :]
                ms = jnp.mean(y * y, axis=-1, keepdims=True)
                o_ref[0, rows, :] = (y * lax.rsqrt(ms + EPS) * _silu(gate_ref[0, rows, :])).astype(o_ref.dtype)
            return carry
        lax.fori_loop(0, n // CHUNK // par, body, 0)

    readout(gc_ref, oc_ref, 0, n_ctx)
    readout(gl_ref, ol_ref, n_ctx, n_lat)


def _hgrn2(p_ctx, p_lat, lb_logits, consts, layer):
    _, bsz, n_ctx, _ = p_ctx.shape
    n_lat = p_lat.shape[2]
    t = n_ctx + n_lat
    lmat, mk = consts
    depth = lb_logits.shape[0]
    lbl = lb_logits.reshape(depth, 2, A_HEADS, 1, HEAD).transpose(0, 1, 3, 2, 4).reshape(depth, 2, 1, A_HEADS * HEAD)

    def col(n, c0):
        return pl.BlockSpec((None, 1, n, HEAD), lambda b, h, c0=c0: (c0 + h, b, 0, 0))

    cols = (COL_A_Q, COL_A_FF, COL_A_FB, COL_A_I, COL_A_GATE)
    return pl.pallas_call(
        functools.partial(_hgrn2_kernel, layer=layer),
        grid=(bsz, A_HEADS),
        in_specs=[
            pl.BlockSpec((depth, 2, 1, HEAD), lambda b, h: (0, 0, 0, h)),
            pl.BlockSpec(lmat.shape, lambda b, h: (0, 0, 0)),
            pl.BlockSpec(mk.shape, lambda b, h: (0, 0, 0, 0)),
        ] + [col(n_ctx, c) for c in cols] + [col(n_lat, c) for c in cols],
        out_specs=[
            pl.BlockSpec((1, n_ctx, HEAD), lambda b, h: (b, 0, h)),
            pl.BlockSpec((1, n_lat, HEAD), lambda b, h: (b, 0, h)),
        ],
        out_shape=[
            jax.ShapeDtypeStruct((bsz, n_ctx, A_HEADS * HEAD), BF16),
            jax.ShapeDtypeStruct((bsz, n_lat, A_HEADS * HEAD), BF16),
        ],
        scratch_shapes=[
            pltpu.VMEM((2, t, HEAD), BF16),
            pltpu.VMEM((2, t, HEAD), F32),
            pltpu.VMEM((2, (t // CHUNK) * HEAD, HEAD), F32),
            pltpu.VMEM((2, (t // CHUNK) * 8, HEAD), F32),
            pltpu.VMEM((2, HEAD, HEAD), F32),
        ],
        compiler_params=_params(("arbitrary", "arbitrary")),
        name="hgrn2",
    )(lbl, lmat, mk, *([p_ctx] * 5), *([p_lat] * 5))


def _mlstm_constants():
    n = CHUNK
    a = np.arange(n)[:, None]
    b = np.arange(n)[None, :]
    lt, rhs, causal = [], [], []
    for reverse in (False, True):
        upto = (b >= a) if reverse else (b <= a)
        after = (b < a) if reverse else (b > a)
        lt.append(np.stack([upto, after]))
        m = after.T
        ext = np.concatenate([m, np.zeros((n, HEAD - n)), np.ones((n, HEAD))], axis=1)
        rhs.append(np.concatenate([ext, ext], axis=0))
        causal.append(upto)
    return (jnp.asarray(np.stack(lt), F32), jnp.asarray(np.stack(rhs), BF16),
            jnp.asarray(np.stack(causal), F32))


def _mlstm_kernel(lt_ref, rhs_ref, causal_ref, eye_ref, cw_ref, bias_ref,
                  qc_ref, kc_ref, vc_ref, oc_ref, zc_ref, gc_ref,
                  ql_ref, kl_ref, vl_ref, ol_ref, zl_ref, gl_ref,
                  outc_ref, outl_ref,
                  q_s, s_s, cols_s, u_s, sc_s, st_s):
    h = pl.program_id(1)
    n_ctx = qc_ref.shape[1]
    n_lat = ql_ref.shape[1]
    n_ctx_chunks = n_ctx // CHUNK
    n_chunks = (n_ctx + n_lat) // CHUNK
    row_id = lax.broadcasted_iota(jnp.int32, (CHUNK, 1), 0)
    lane = lax.broadcasted_iota(jnp.int32, (1, HEAD), 1)
    sub8 = lax.broadcasted_iota(jnp.int32, (8, 1), 0)
    gate_row = lax.broadcasted_iota(jnp.int32, (N_GATES, 1), 0)
    eye_c = (row_id == lax.broadcasted_iota(jnp.int32, (1, CHUNK), 1)).astype(F32)
    ones_b = jnp.ones((CHUNK, HEAD), BF16)

    def conv_silu(src, ci, n, w):
        s = pl.multiple_of(ci * CHUNK, CHUNK)
        cur = src[0, pl.ds(s, CHUNK), :]
        up = src[0, pl.ds(pl.multiple_of(jnp.maximum(s - 8, 0), 8), 8), :]
        dn = src[0, pl.ds(pl.multiple_of(jnp.minimum(s + CHUNK, n - 8), 8), 8), :]
        prev_row = jnp.where(s > 0, up[7:8, :], 0.0)
        next_row = jnp.where(s + CHUNK < n, dn[0:1, :], 0.0)
        prev = jnp.where(row_id == 0, prev_row, pltpu.roll(cur, 1, 0))
        nxt = jnp.where(row_id == CHUNK - 1, next_row, pltpu.roll(cur, CHUNK - 1, 0))
        return _silu(w[0:1, :] * prev + w[1:2, :] * cur + w[2:3, :] * nxt)

    def pick_row(a, idx):
        return jnp.sum(jnp.where(gate_row == idx, a, 0.0), axis=0, keepdims=True)

    def phase_a(q_ref, k_ref, v_ref, g_ref, off, n):
        par = _largest_divisor(n // CHUNK, (4, 2, 1))

        def body(c, carry):
            chunks = []
            for u in range(par):
                ci = c * par + u
                rows = pl.ds(pl.multiple_of(ci * CHUNK, CHUNK), CHUNK)
                srows = pl.ds(pl.multiple_of(off + ci * CHUNK, CHUNK), CHUNK)
                gt = _dot3_nt(eye_ref[...], g_ref[0, rows, :]) + bias_ref[:, 0:CHUNK]
                qb = (conv_silu(q_ref, ci, n, cw_ref[0]) * (HEAD ** -0.5)).astype(BF16)
                k = conv_silu(k_ref, ci, n, cw_ref[1])
                q_s[srows, :] = qb
                vaug = jnp.concatenate([v_ref[0, rows, :].astype(BF16), ones_b], axis=1)
                chunks.append(dict(k=k, qk=_dot_nt(qb, k.astype(BF16)), vaug=vaug, gt=gt, srows=srows,
                                   chunk=off // CHUNK + ci))
            chains = []
            for ck in chunks:
                for d in range(2):
                    gi = 2 * B_HEADS * d + h
                    li_row = pick_row(ck["gt"], gi)
                    lf_row = _log_sigmoid(pick_row(ck["gt"], gi + B_HEADS))
                    lhs = jnp.concatenate([lt_ref[d, 0] * lf_row,
                                           lt_ref[d, 1] * lf_row + eye_c * li_row], axis=0)
                    hi = lhs.astype(BF16)
                    mid = (lhs - hi.astype(F32)).astype(BF16)
                    out = _dot(jnp.concatenate([hi, mid], axis=1), rhs_ref[d])
                    chains.append(dict(ck, d=d, li_row=li_row, out=out))
            for ch in chains:
                d, out = ch["d"], ch["out"]
                cum_rep = out[0:CHUNK, HEAD:]
                logw_rep = out[CHUNK:, HEAD:]
                log_d = jnp.where(causal_ref[d] > 0.0, out[0:CHUNK, 0:CHUNK] + ch["li_row"], NEG)
                a_loc = jnp.max(log_d, axis=1, keepdims=True)
                s_s[d, ch["srows"], :] = (ch["qk"] * jnp.exp(log_d - a_loc)).astype(BF16)
                cols_s[d, ch["srows"], :] = jnp.where(lane < CHUNK, a_loc, cum_rep)
                total = cum_rep[CHUNK - 1:CHUNK, :] if d == 0 else cum_rep[0:1, :]
                m_loc = jnp.max(logw_rep, axis=0, keepdims=True)
                kw = (ch["k"] * jnp.exp(logw_rep - m_loc)).astype(BF16)
                ch["u"] = _dot_tn(kw, ch["vaug"])
                sc_s[d, pl.ds(pl.multiple_of(ch["chunk"] * 8, 8), 8), :] = jnp.where(
                    sub8 == 0, total, jnp.where(sub8 == 1, m_loc, 0.0))
            for ch in chains:
                u_s[ch["d"], pl.ds(pl.multiple_of(ch["chunk"] * HEAD, HEAD), HEAD), :] = ch["u"].astype(BF16)
            return carry
        lax.fori_loop(0, n // CHUNK // par, body, 0)

    phase_a(qc_ref, kc_ref, vc_ref, gc_ref, 0, n_ctx)
    phase_a(ql_ref, kl_ref, vl_ref, gl_ref, n_ctx, n_lat)

    st_s[...] = jnp.zeros_like(st_s)
    steps = _largest_divisor(n_chunks, (4, 2, 1))
    for d in range(2):
        def phase_b(jj, m, d=d):
            for u in range(steps):
                chunk = _scan_row(jj * steps + u, d == 1, n_ctx_chunks, n_chunks) // CHUNK
                srow = pl.ds(pl.multiple_of(chunk * 8, 8), 8)
                urow = pl.ds(pl.multiple_of(chunk * HEAD, HEAD), HEAD)
                sc = sc_s[d, srow, :]
                total, m_loc = sc[0:1, :], sc[1:2, :]
                m_new = jnp.maximum(total + m, m_loc)
                state = st_s[d]
                inc = u_s[d, urow, :].astype(F32)
                u_s[d, urow, :] = state.astype(BF16)
                sc_s[d, srow, :] = jnp.where(sub8 == 2, m, sc)
                keep = jnp.exp(total + m - m_new)
                gain = jnp.exp(m_loc - m_new)
                st_s[d] = (jnp.concatenate([keep, keep], axis=1) * state
                           + jnp.concatenate([gain, gain], axis=1) * inc)
                m = m_new
            return m
        lax.fori_loop(0, n_chunks // steps, phase_b, jnp.zeros((1, HEAD), F32))

    def phase_c(v_ref, o_ref, z_ref, out_ref, off, n):
        par = _largest_divisor(n // CHUNK, (4, 2, 1))

        def body(c, carry):
            chains = []
            for u in range(par):
                ci = c * par + u
                rows = pl.ds(pl.multiple_of(ci * CHUNK, CHUNK), CHUNK)
                srows = pl.ds(pl.multiple_of(off + ci * CHUNK, CHUNK), CHUNK)
                chunk = off // CHUNK + ci
                qb = q_s[srows, :]
                vaug = jnp.concatenate([v_ref[0, rows, :].astype(BF16), ones_b], axis=1)
                for d in range(2):
                    intra = _dot(s_s[d, srows, :], vaug)
                    inter = _dot(qb, u_s[d, pl.ds(pl.multiple_of(chunk * HEAD, HEAD), HEAD), :])
                    chains.append(dict(d=d, u=u, rows=rows, srows=srows, chunk=chunk, intra=intra, inter=inter))
            ys = [None] * par
            for ch in chains:
                d = ch["d"]
                cols = cols_s[d, ch["srows"], :]
                swapped = pltpu.roll(cols, CHUNK, 1)
                a_rep = jnp.where(lane < CHUNK, cols, swapped)
                cum_rep = jnp.where(lane < CHUNK, swapped, cols)
                m = sc_s[d, pl.ds(pl.multiple_of(ch["chunk"] * 8, 8), 8), :][2:3, :]
                inter_l = cum_rep + m
                m_t = jnp.maximum(a_rep, inter_l)
                wi = jnp.exp(a_rep - m_t)
                we = jnp.exp(inter_l - m_t)
                num = wi * ch["intra"][:, :HEAD] + we * ch["inter"][:, :HEAD]
                den = wi * ch["intra"][:, HEAD:] + we * ch["inter"][:, HEAD:]
                hd = num / jnp.maximum(jnp.abs(den), jnp.exp(-m_t))
                ys[ch["u"]] = hd if ys[ch["u"]] is None else ys[ch["u"]] + hd
                if d == 1:
                    rows = ch["rows"]
                    hh = jax.nn.sigmoid(o_ref[0, rows, :]) * ys[ch["u"]]
                    ms = jnp.mean(hh * hh, axis=-1, keepdims=True)
                    out_ref[0, rows, :] = (hh * lax.rsqrt(ms + EPS) * _silu(z_ref[0, rows, :])).astype(out_ref.dtype)
            return carry
        lax.fori_loop(0, n // CHUNK // par, body, 0)

    phase_c(vc_ref, oc_ref, zc_ref, outc_ref, 0, n_ctx)
    phase_c(vl_ref, ol_ref, zl_ref, outl_ref, n_ctx, n_lat)


def _mlstm(p_ctx, p_lat, g_ctx, g_lat, conv_w, gate_bias, consts):
    _, bsz, n_ctx, _ = p_ctx.shape
    n_lat = p_lat.shape[2]
    t = n_ctx + n_lat
    n_chunks = t // CHUNK
    lt, rhs, causal = consts
    eye = jnp.eye(N_GATES, HEAD, dtype=BF16)
    cw = conv_w.reshape(3, 2, B_HEADS * HEAD).transpose(1, 0, 2)
    bias = jnp.broadcast_to(gate_bias.reshape(N_GATES, 1), (N_GATES, HEAD))

    def col(n, c0):
        return pl.BlockSpec((None, 1, n, HEAD), lambda b, h, c0=c0: (c0 + h, b, 0, 0))

    def gate_spec(n):
        return pl.BlockSpec((1, n, HEAD), lambda b, h: (b, 0, 0))

    cols = (COL_B_Q, COL_B_K, COL_B_V, COL_B_O, COL_B_Z)
    return pl.pallas_call(
        _mlstm_kernel,
        grid=(bsz, B_HEADS),
        in_specs=[
            pl.BlockSpec(lt.shape, lambda b, h: (0, 0, 0, 0)),
            pl.BlockSpec(rhs.shape, lambda b, h: (0, 0, 0)),
            pl.BlockSpec(causal.shape, lambda b, h: (0, 0, 0)),
            pl.BlockSpec(eye.shape, lambda b, h: (0, 0)),
            pl.BlockSpec((2, 3, HEAD), lambda b, h: (0, 0, h)),
            pl.BlockSpec((N_GATES, HEAD), lambda b, h: (0, 0)),
        ] + [col(n_ctx, c) for c in cols] + [gate_spec(n_ctx)]
          + [col(n_lat, c) for c in cols] + [gate_spec(n_lat)],
        out_specs=[
            pl.BlockSpec((1, n_ctx, HEAD), lambda b, h: (b, 0, h)),
            pl.BlockSpec((1, n_lat, HEAD), lambda b, h: (b, 0, h)),
        ],
        out_shape=[
            jax.ShapeDtypeStruct((bsz, n_ctx, B_HEADS * HEAD), BF16),
            jax.ShapeDtypeStruct((bsz, n_lat, B_HEADS * HEAD), BF16),
        ],
        scratch_shapes=[
            pltpu.VMEM((t, HEAD), BF16),
            pltpu.VMEM((2, t, CHUNK), BF16),
            pltpu.VMEM((2, t, HEAD), F32),
            pltpu.VMEM((2, n_chunks * HEAD, 2 * HEAD), BF16),
            pltpu.VMEM((2, n_chunks * 8, HEAD), F32),
            pltpu.VMEM((2, HEAD, 2 * HEAD), F32),
        ],
        compiler_params=_params(("arbitrary", "arbitrary")),
        name="mlstm",
    )(lt, rhs, causal, eye, cw, bias, *([p_ctx] * 5), g_ctx, *([p_lat] * 5), g_lat)


def _rope(x, cos, sin_lo, sin_hi):
    return x * cos + pltpu.roll(x, HEAD - 32, 1) * sin_lo + pltpu.roll(x, 32, 1) * sin_hi


LOG2E = 1.4426950408889634


def _softmax_attend(sink_col, scores, values):
    m = jnp.maximum(sink_col, jnp.max(scores, axis=1, keepdims=True))
    p = jnp.exp2(scores - m)
    den = jnp.exp2(sink_col - m) + jnp.sum(p, axis=1, keepdims=True)
    return _dot(p.astype(BF16), values) / den


def _sink_column(sink_ref, kv, rows_per_head):
    row = lax.broadcasted_iota(jnp.int32, (C_GROUP * rows_per_head, 1), 0)
    col = jnp.zeros((C_GROUP * rows_per_head, 1), F32)
    for g in range(C_GROUP):
        col = jnp.where(row // rows_per_head == g, sink_ref[kv * C_GROUP + g] * LOG2E, col)
    return col


def _window_masks():
    r = np.arange(ATTN_BLOCK)[:, None]
    c = np.arange(3 * ATTN_BLOCK)[None, :]
    return jnp.asarray(np.stack([np.abs(c - p * ATTN_BLOCK - r) <= WINDOW for p in range(3)]), F32)


def _attn_kernel(sink_ref, mask_ref, cos_ref, slo_ref, shi_ref, q_ref, k_ref, v_ref, kc_ref, vc_ref, z_ref,
                 o_ref, kr_s):
    kv = pl.program_id(1)
    i = pl.program_id(2)
    n = k_ref.shape[1]
    span = 3 * ATTN_BLOCK

    @pl.when(i == 0)
    def _():
        def body(c, carry):
            rows = pl.ds(pl.multiple_of(c * ATTN_BLOCK, ATTN_BLOCK), ATTN_BLOCK)
            kr_s[rows, :] = _rope(k_ref[0, rows, :], cos_ref[rows, :], slo_ref[rows, :],
                                  shi_ref[rows, :]).astype(BF16)
            return carry
        lax.fori_loop(0, n // ATTN_BLOCK, body, 0)

    qrows = pl.ds(pl.multiple_of(i * ATTN_BLOCK, ATTN_BLOCK), ATTN_BLOCK)
    cos = cos_ref[qrows, :]
    slo = slo_ref[qrows, :]
    shi = shi_ref[qrows, :]
    scale = HEAD ** -0.5 * LOG2E
    start = pl.multiple_of(jnp.clip((i - 1) * ATTN_BLOCK, 0, n - span), ATTN_BLOCK)
    keys = jnp.concatenate([kr_s[pl.ds(start, span), :], kc_ref[0].astype(BF16)], axis=0)
    values = jnp.concatenate([v_ref[0, pl.ds(start, span), :].astype(BF16), vc_ref[0].astype(BF16)], axis=0)
    valid = mask_ref[(i * ATTN_BLOCK - start) // ATTN_BLOCK] > 0.0
    scores = []
    for g in range(C_GROUP):
        qg = (_rope(q_ref[g], cos, slo, shi) * scale).astype(BF16)
        s = _dot_nt(qg, keys)
        scores.append(jnp.concatenate([jnp.where(valid, s[:, :span], NEG), s[:, span:]], axis=1))
    for g in range(C_GROUP):
        sink = jnp.full((ATTN_BLOCK, 1), sink_ref[kv * C_GROUP + g] * LOG2E, F32)
        out = _softmax_attend(sink, scores[g], values)
        zg = z_ref[g]
        o_ref[0, :, g * HEAD:(g + 1) * HEAD] = (out * _silu(zg)).astype(o_ref.dtype)


def _attention(p_ctx, p_lat, sink, rope):
    _, bsz, n_ctx, _ = p_ctx.shape
    n = p_lat.shape[2]
    cos, slo, shi = rope
    masks = _window_masks()
    gw = C_GROUP * HEAD
    gblk = C_GROUP

    full = pl.BlockSpec((n, HEAD), lambda b, kv, i: (0, 0))
    return pl.pallas_call(
        _attn_kernel,
        grid=(bsz, C_KV_HEADS, n // ATTN_BLOCK),
        in_specs=[
            pl.BlockSpec(memory_space=pltpu.SMEM),
            pl.BlockSpec(masks.shape, lambda b, kv, i: (0, 0, 0)),
            full, full, full,
            pl.BlockSpec((C_GROUP, None, ATTN_BLOCK, HEAD), lambda b, kv, i: (COL_C_Q // gblk + kv, b, i, 0)),
            pl.BlockSpec((None, 1, n, HEAD), lambda b, kv, i: (COL_C_K + kv, b, 0, 0)),
            pl.BlockSpec((None, 1, n, HEAD), lambda b, kv, i: (COL_C_V + kv, b, 0, 0)),
            pl.BlockSpec((None, 1, n_ctx, HEAD), lambda b, kv, i: (COL_C_K + kv, b, 0, 0)),
            pl.BlockSpec((None, 1, n_ctx, HEAD), lambda b, kv, i: (COL_C_V + kv, b, 0, 0)),
            pl.BlockSpec((C_GROUP, None, ATTN_BLOCK, HEAD), lambda b, kv, i: (COL_C_Z // gblk + kv, b, i, 0)),
        ],
        out_specs=pl.BlockSpec((1, ATTN_BLOCK, gw), lambda b, kv, i: (b, i, kv)),
        out_shape=jax.ShapeDtypeStruct((bsz, n, C_Q_HEADS * HEAD), BF16),
        scratch_shapes=[pltpu.VMEM((n, HEAD), BF16)],
        compiler_params=_params(("arbitrary", "arbitrary", "arbitrary")),
        name="window_attention",
    )(sink, masks, cos, slo, shi, p_lat, p_lat, p_lat, p_ctx, p_ctx, p_lat)


def _ctx_attn_kernel(sink_ref, q_ref, k_ref, v_ref, z_ref, o_ref):
    kv = pl.program_id(1)
    n_ctx = q_ref.shape[1]
    scale = HEAD ** -0.5 * LOG2E
    q4 = jnp.concatenate([q_ref[g] * scale for g in range(C_GROUP)],
                         axis=0).astype(BF16)
    s = _dot_nt(q4, k_ref[0].astype(BF16))
    out = _softmax_attend(_sink_column(sink_ref, kv, n_ctx), s, v_ref[0].astype(BF16))
    for g in range(C_GROUP):
        zg = z_ref[g]
        o_ref[0, :, g * HEAD:(g + 1) * HEAD] = (out[g * n_ctx:(g + 1) * n_ctx] * _silu(zg)).astype(o_ref.dtype)


def _context_attention(p_ctx, sink):
    _, bsz, n_ctx, _ = p_ctx.shape
    gw = C_GROUP * HEAD
    gblk = C_GROUP
    return pl.pallas_call(
        _ctx_attn_kernel,
        grid=(bsz, C_KV_HEADS),
        in_specs=[
            pl.BlockSpec(memory_space=pltpu.SMEM),
            pl.BlockSpec((C_GROUP, None, n_ctx, HEAD), lambda b, kv: (COL_C_Q // gblk + kv, b, 0, 0)),
            pl.BlockSpec((None, 1, n_ctx, HEAD), lambda b, kv: (COL_C_K + kv, b, 0, 0)),
            pl.BlockSpec((None, 1, n_ctx, HEAD), lambda b, kv: (COL_C_V + kv, b, 0, 0)),
            pl.BlockSpec((C_GROUP, None, n_ctx, HEAD), lambda b, kv: (COL_C_Z // gblk + kv, b, 0, 0)),
        ],
        out_specs=pl.BlockSpec((1, n_ctx, gw), lambda b, kv: (b, 0, kv)),
        out_shape=jax.ShapeDtypeStruct((bsz, n_ctx, C_Q_HEADS * HEAD), BF16),
        compiler_params=_params(("arbitrary", "arbitrary")),
        name="context_attention",
    )(sink, p_ctx, p_ctx, p_ctx, p_ctx)


def _rope_tables(n):
    axis_dim = HEAD // 2
    rows = n // GRID_W
    row = jnp.repeat(jnp.arange(rows, dtype=F32), GRID_W)
    col = jnp.tile(jnp.arange(GRID_W, dtype=F32), rows)
    inv_freq = ROPE_BASE ** (-jnp.arange(0, axis_dim, 2, dtype=F32) / axis_dim)
    ang_r = row[:, None] * inv_freq[None, :]
    ang_c = col[:, None] * inv_freq[None, :]
    zero = jnp.zeros_like(ang_r)
    cos = jnp.concatenate([jnp.cos(ang_r)] * 2 + [jnp.cos(ang_c)] * 2, axis=-1)
    sin_lo = jnp.concatenate([-jnp.sin(ang_r), zero, -jnp.sin(ang_c), zero], axis=-1)
    sin_hi = jnp.concatenate([zero, jnp.sin(ang_r), zero, jnp.sin(ang_c)], axis=-1)
    return cos, sin_lo, sin_hi


def _pick_tile(n, candidates):
    for c in candidates:
        if n % c == 0:
            return c
    raise ValueError(f"no tile for {n}")


def kernel(x, c, ctx, c_ctx, w_mod, b_mod, g_pre, g_post, w_in, hgrn_lb_logits, mlstm_conv_w,
           mlstm_gate_bias, attn_sink, w_out):
    bsz, n_lat, d = x.shape
    n_ctx = ctx.shape[1]
    depth = w_mod.shape[0]
    assert n_ctx % CHUNK == 0 and n_lat % ATTN_BLOCK == 0 and n_lat >= 3 * ATTN_BLOCK

    cond = jnp.concatenate([c, c_ctx[None, :]], axis=0)
    pad = (-cond.shape[0]) % 8
    cond = jnp.pad(cond, ((0, pad), (0, 0)))
    mod = _modulation(cond, w_mod, b_mod)

    w_main, w_gates = _weight_prep(w_in)
    wo = w_out.astype(BF16)
    consts = _scan_constants()
    mlstm_consts = _mlstm_constants()
    rope = _rope_tables(n_lat)

    tm_lat = _pick_tile(n_lat, (1024, 512, 256, 128))
    tm_ctx = _pick_tile(bsz * n_ctx, (1024, 512, 256, 128))
    tn = _pick_tile(MAIN_WIDTH, (1280, 768, 512))
    to_lat = _pick_tile(n_lat, (512, 256, 128))
    to_ctx = _pick_tile(bsz * n_ctx, (512, 256, 128))

    x2d = x.reshape(bsz * n_lat, d)
    h2d = ctx.reshape(bsz * n_ctx, d)
    for layer in range(depth):
        with_ctx = layer < depth - 1
        shift_l = mod[layer, :bsz, 0:d].reshape(bsz, 1, d)
        scale_l = mod[layer, :bsz, d:2 * d].reshape(bsz, 1, d)
        gate_l = mod[layer, :bsz, 2 * d:].reshape(bsz, 1, d)
        shift_c = mod[layer, bsz:bsz + 1, 0:d].reshape(1, 1, d)
        scale_c = mod[layer, bsz:bsz + 1, d:2 * d].reshape(1, 1, d)
        gate_c = mod[layer, bsz:bsz + 1, 2 * d:].reshape(1, 1, d)

        gain_pre = g_pre[layer].reshape(1, d)
        gain_post = g_post[layer].reshape(1, d)

        p_lat, g_lat = _in_projection(x2d, shift_l, scale_l, gain_pre, w_main, w_gates, layer, n_lat,
                                      tm_lat, tn)
        p_ctx, g_ctx = _in_projection(h2d, shift_c, scale_c, gain_pre, w_main, w_gates, layer,
                                      bsz * n_ctx, tm_ctx, tn)
        p_lat = p_lat.reshape(MAIN_BLOCKS, bsz, n_lat, HEAD)
        p_ctx = p_ctx.reshape(MAIN_BLOCKS, bsz, n_ctx, HEAD)
        g_lat = g_lat.reshape(bsz, n_lat, HEAD)
        g_ctx = g_ctx.reshape(bsz, n_ctx, HEAD)

        a_c, a_l = _hgrn2(p_ctx, p_lat, hgrn_lb_logits, consts, layer)
        b_c, b_l = _mlstm(p_ctx, p_lat, g_ctx, g_lat, mlstm_conv_w[layer], mlstm_gate_bias[layer],
                          mlstm_consts)
        c_l = _attention(p_ctx, p_lat, attn_sink[layer], rope)

        x2d = _out_projection(a_l.reshape(bsz * n_lat, -1), b_l.reshape(bsz * n_lat, -1),
                              c_l.reshape(bsz * n_lat, -1), wo, layer, x2d, gate_l, gain_post, n_lat, to_lat)
        if with_ctx:
            c_c = _context_attention(p_ctx, attn_sink[layer])
            h2d = _out_projection(a_c.reshape(bsz * n_ctx, -1), b_c.reshape(bsz * n_ctx, -1),
                                  c_c.reshape(bsz * n_ctx, -1), wo, layer, h2d, gate_c, gain_post,
                                  bsz * n_ctx, to_ctx)
    return x2d.reshape(bsz, n_lat, d)
```

```python
import functools

import numpy as np
import jax
import jax.numpy as jnp
from jax import lax
from jax.experimental import pallas as pl
from jax.experimental.pallas import tpu as pltpu

F32 = jnp.float32
BF16 = jnp.bfloat16

EPS = 1e-6
NEG = -1e30
HEAD = 128
A_HEADS = 4
B_HEADS = 4
C_Q_HEADS = 8
C_KV_HEADS = 2
C_GROUP = C_Q_HEADS // C_KV_HEADS
CHUNK = 64
ATTN_BLOCK = 128
WINDOW = 128
GRID_W = 64
ROPE_BASE = 10000.0
N_GATES = 4 * B_HEADS

COL_A_Q, COL_A_FF, COL_A_FB, COL_A_I, COL_A_GATE = 0, 4, 8, 12, 16
COL_B_Q, COL_B_K, COL_B_V, COL_B_O, COL_B_Z = 20, 24, 28, 32, 36
COL_C_Q, COL_C_K, COL_C_V, COL_C_Z = 40, 48, 50, 52
MAIN_BLOCKS = 60
MAIN_WIDTH = MAIN_BLOCKS * HEAD
GATES_OFFSET = 9 * 4 * HEAD

VMEM_LIMIT = 56 * 1024 * 1024


def _params(sem):
    return pltpu.CompilerParams(dimension_semantics=sem, vmem_limit_bytes=VMEM_LIMIT)


def _silu(x):
    return x * jax.nn.sigmoid(x)


def _log_sigmoid(x):
    return jnp.minimum(x, 0.0) - jnp.log(1.0 + jnp.exp(-jnp.abs(x)))


def _dot(a, b):
    return jnp.dot(a, b, preferred_element_type=F32)


def _dot_nt(a, b):
    return lax.dot_general(a, b, (((1,), (1,)), ((), ())), preferred_element_type=F32)


def _dot_tn(a, b):
    return lax.dot_general(a, b, (((0,), (0,)), ((), ())), preferred_element_type=F32)


def _split3(x):
    hi = x.astype(BF16)
    r = x - hi.astype(F32)
    mid = r.astype(BF16)
    lo = (r - mid.astype(F32)).astype(BF16)
    return hi, mid, lo


def _dot3_nt(w, x):
    hi, mid, lo = _split3(x)
    return _dot_nt(w, hi) + _dot_nt(w, mid) + _dot_nt(w, lo)


def _mod_kernel(c_ref, w_ref, b_ref, o_ref):
    a = _silu(c_ref[...])
    o_ref[0] = jnp.dot(a, w_ref[0], preferred_element_type=F32,
                       precision=lax.Precision.HIGHEST) + b_ref[0]


def _modulation(cond, w_mod, b_mod):
    depth, d, n = w_mod.shape
    rows = cond.shape[0]
    tn = 768
    assert n % tn == 0
    return pl.pallas_call(
        _mod_kernel,
        grid=(depth, n // tn),
        in_specs=[
            pl.BlockSpec((rows, d), lambda l, j: (0, 0)),
            pl.BlockSpec((1, d, tn), lambda l, j: (l, 0, j)),
            pl.BlockSpec((1, 1, tn), lambda l, j: (l, 0, j)),
        ],
        out_specs=pl.BlockSpec((1, rows, tn), lambda l, j: (l, 0, j)),
        out_shape=jax.ShapeDtypeStruct((depth, rows, n), F32),
        compiler_params=_params(("arbitrary", "arbitrary")),
        name="modulation",
    )(cond, w_mod, b_mod.reshape(depth, 1, n))


NORM_ROWS = 32
EPILOGUE_ROWS = 8


def _inproj_kernel(x_ref, shift_ref, scale_ref, g_ref, w_ref, wg_ref, p_ref, gates_ref, xn_ref, gm_ref):
    tm = x_ref.shape[0]

    @pl.when(pl.program_id(1) == 0)
    def _():
        gm_ref[...] = g_ref[...] * (1.0 + scale_ref[0])

        def body(r, carry):
            for u in range(2):
                rows = pl.ds(pl.multiple_of((2 * r + u) * NORM_ROWS, NORM_ROWS), NORM_ROWS)
                xr = x_ref[rows, :]
                ms = jnp.mean(xr * xr, axis=-1, keepdims=True)
                xn_ref[rows, :] = (xr * (lax.rsqrt(ms + EPS) * gm_ref[...]) + shift_ref[0]).astype(BF16)
            return carry

        lax.fori_loop(0, tm // NORM_ROWS // 2, body, 0)
        gates_ref[...] = _dot_nt(xn_ref[...], wg_ref[...])

    res = _dot_nt(xn_ref[...], w_ref[...])
    for t in range(p_ref.shape[0]):
        p_ref[t] = res[:, t * HEAD:(t + 1) * HEAD]


GATES_BLOCK = GATES_OFFSET // HEAD


def _weight_prep_kernel(w_ref, o_ref):
    c = pl.program_id(1)
    row = lax.broadcasted_iota(jnp.int32, (HEAD, 1), 0)
    keep = jnp.logical_or(c < MAIN_BLOCKS, row < N_GATES)
    o_ref[...] = jnp.where(keep, w_ref[0], 0.0).astype(BF16)


def _weight_prep(w_in_t):
    depth, n, d = w_in_t.shape

    def src_row(c):
        after = jnp.where(c >= GATES_BLOCK, N_GATES, 0)
        return pl.multiple_of(jnp.where(c < MAIN_BLOCKS, c * HEAD + after, GATES_OFFSET), N_GATES)

    return pl.pallas_call(
        _weight_prep_kernel,
        grid=(depth, MAIN_BLOCKS + 1),
        in_specs=[pl.BlockSpec((pl.Element(1), pl.Element(HEAD), pl.Element(d)),
                               lambda l, c: (l, src_row(c), 0))],
        out_specs=pl.BlockSpec((None, HEAD, d), lambda l, c: (l, c, 0)),
        out_shape=jax.ShapeDtypeStruct((depth, (MAIN_BLOCKS + 1) * HEAD, d), BF16),
        compiler_params=_params(("arbitrary", "arbitrary")),
        name="weight_prep",
    )(w_in_t)


def _in_projection(x2d, shift, scale, gain, w_t, layer, rows_per_mod, tm, tn):
    r, d = x2d.shape
    n = MAIN_WIDTH
    assert r % tm == 0 and n % tn == 0 and rows_per_mod % tm == 0
    per = rows_per_mod // tm
    return pl.pallas_call(
        _inproj_kernel,
        grid=(r // tm, n // tn),
        in_specs=[
            pl.BlockSpec((tm, d), lambda i, j: (i, 0)),
            pl.BlockSpec((1, 1, d), lambda i, j: (i // per, 0, 0)),
            pl.BlockSpec((1, 1, d), lambda i, j: (i // per, 0, 0)),
            pl.BlockSpec((1, d), lambda i, j: (0, 0)),
            pl.BlockSpec((None, tn, d), lambda i, j: (layer, j, 0)),
            pl.BlockSpec((None, HEAD, d), lambda i, j: (layer, MAIN_BLOCKS, 0)),
        ],
        out_specs=[
            pl.BlockSpec((tn // HEAD, tm, HEAD), lambda i, j: (j, i, 0)),
            pl.BlockSpec((tm, HEAD), lambda i, j: (i, 0)),
        ],
        out_shape=[
            jax.ShapeDtypeStruct((n // HEAD, r, HEAD), F32),
            jax.ShapeDtypeStruct((r, HEAD), F32),
        ],
        scratch_shapes=[pltpu.VMEM((tm, d), BF16), pltpu.VMEM((1, d), F32)],
        compiler_params=_params(("arbitrary", "arbitrary")),
        name="in_projection",
    )(x2d, shift, scale, gain, w_t, w_t)


def _outproj_kernel(a_ref, b_ref, c_ref, w_ref, x_ref, gate_ref, g_ref, o_ref, y_s):
    i = pl.program_id(0)
    tm = x_ref.shape[0]
    slot = i % 2

    @pl.when(i == 0)
    def _():
        y_s[1] = jnp.zeros(y_s.shape[1:], F32)

    for r in range(tm // EPILOGUE_ROWS):
        rows = slice(r * EPILOGUE_ROWS, (r + 1) * EPILOGUE_ROWS)
        y = y_s[1 - slot, rows, :]
        ms = jnp.mean(y * y, axis=-1, keepdims=True)
        o_ref[rows, :] = x_ref[rows, :] + y * (lax.rsqrt(ms + EPS) * (gate_ref[0] * g_ref[...]))
    y_s[slot] = _dot(jnp.concatenate([a_ref[...], b_ref[...], c_ref[...]], axis=1), w_ref[...])


def _out_projection(a, b, c, w_out, layer, x2d, gate, gain, rows_per_mod, tm):
    r, d = x2d.shape
    assert r % tm == 0 and rows_per_mod % tm == 0
    per = rows_per_mod // tm
    n = r // tm
    cur = lambda i: (jnp.minimum(i, n - 1), 0)
    prev = lambda i: (jnp.maximum(i - 1, 0), 0)
    return pl.pallas_call(
        _outproj_kernel,
        grid=(n + 1,),
        in_specs=[
            pl.BlockSpec((tm, a.shape[1]), cur),
            pl.BlockSpec((tm, b.shape[1]), cur),
            pl.BlockSpec((tm, c.shape[1]), cur),
            pl.BlockSpec((None,) + w_out.shape[1:], lambda i: (layer, 0, 0)),
            pl.BlockSpec((tm, d), prev),
            pl.BlockSpec((1, 1, d), lambda i: (jnp.maximum(i - 1, 0) // per, 0, 0)),
            pl.BlockSpec((1, d), lambda i: (0, 0)),
        ],
        out_specs=pl.BlockSpec((tm, d), prev),
        out_shape=jax.ShapeDtypeStruct((r, d), F32),
        scratch_shapes=[pltpu.VMEM((2, tm, d), F32)],
        compiler_params=_params(("arbitrary",)),
        name="out_projection",
    )(a, b, c, w_out, x2d, gate, gain)


N_LEVELS = 7


def _scan_matrices(reverse):
    n = CHUNK
    t = np.arange(n)[:, None]
    u = np.arange(n)[None, :]
    cum = (u >= t) if reverse else (u <= t)
    masks = [t == u]
    size = 2
    while size <= n:
        half = size // 2
        same = (t // size) == (u // size)
        if reverse:
            masks.append(same & ((t % size) < half) & ((u % size) >= half))
        else:
            masks.append(same & ((t % size) >= half) & ((u % size) < half))
        size *= 2
    return cum.astype(np.float32), np.stack(masks).astype(np.float32)


def _scan_constants():
    cf, mf = _scan_matrices(False)
    cb, mb = _scan_matrices(True)
    return (jnp.asarray(np.stack([cf, cb]), BF16), jnp.asarray(np.stack([mf, mb]), F32))


def _largest_divisor(n, candidates):
    return next(c for c in candidates if n % c == 0)


def _scan_row(j, reverse, n_ctx_chunks, n_chunks):
    if not reverse:
        return j * CHUNK
    ctx_part = (n_ctx_chunks - 1 - j) * CHUNK
    lat_part = (n_ctx_chunks + (n_chunks - 1 - j)) * CHUNK
    return jnp.where(j < n_ctx_chunks, ctx_part, lat_part)


def _hgrn2_kernel(lbl_ref, lmat_ref, mk_ref,
                  qc_ref, ffc_ref, fbc_ref, ic_ref, gc_ref,
                  ql_ref, ffl_ref, fbl_ref, il_ref, gl_ref,
                  oc_ref, ol_ref,
                  qd_s, y_s, u_s, dec_s, st_s, *, layer):
    n_ctx = qc_ref.shape[1]
    n_lat = ql_ref.shape[1]
    n_ctx_chunks = n_ctx // CHUNK
    n_chunks = (n_ctx + n_lat) // CHUNK

    logits = lbl_ref[...]
    e = jnp.exp(logits - jnp.max(logits, axis=0, keepdims=True))
    sm = e / jnp.sum(e, axis=0, keepdims=True)
    lbs = jnp.sum(sm[:layer + 1], axis=0) - sm[0]

    row = lax.broadcasted_iota(jnp.int32, (CHUNK, 1), 0)
    r4 = row % 4

    def level_factor(size, d, f, cum):
        half = size // 2
        if size == 2:
            return jnp.where((row % 2 == 1) if d == 0 else (row % 2 == 0), f, 1.0)
        if size == 4:
            f_next = pltpu.roll(f, CHUNK - 1, 0)
            f_prev = pltpu.roll(f, 1, 0)
            if d == 0:
                return jnp.where(r4 == 0, f_next, jnp.where(r4 == 1, 1.0, jnp.where(r4 == 2, f, f * f_prev)))
            return jnp.where(r4 == 0, f * f_next, jnp.where(r4 == 1, f, jnp.where(r4 == 2, 1.0, f_prev)))
        mids = [b * size + (half - 1 if d == 0 else half) for b in range(CHUNK // size)]
        ref = jnp.concatenate([jnp.broadcast_to(cum[m:m + 1, :], (size, HEAD)) for m in mids], axis=0)
        return jnp.exp(-jnp.abs(cum - ref))

    key_lane = lax.broadcasted_iota(jnp.int32, (1, CHUNK), 1)
    fine_levels = 4

    def assemble_scores(d, dots):
        groups = []
        for g in range(CHUNK // 8):
            r = slice(8 * g, 8 * g + 8)
            acc = mk_ref[d, 0, r, :] * dots[0][r]
            for lvl in range(1, fine_levels):
                acc = acc + mk_ref[d, lvl, r, :] * dots[lvl][r]
            for lvl in range(fine_levels, N_LEVELS):
                size = 2 ** lvl
                half = size // 2
                base = (8 * g // size) * size
                on_query_side = (8 * g - base >= half) if d == 0 else (8 * g - base < half)
                if on_query_side:
                    lo = base if d == 0 else base + half
                    acc = jnp.where((key_lane >= lo) & (key_lane < lo + half), dots[lvl][r], acc)
            groups.append(acc)
        return jnp.concatenate(groups, axis=0)

    def phase_a(q_ref, f_refs, i_ref, off, n):
        par = _largest_divisor(n // CHUNK, (4, 2, 1))

        def body(c, carry):
            chains = []
            for u in range(par):
                ci = c * par + u
                rows = pl.ds(pl.multiple_of(ci * CHUNK, CHUNK), CHUNK)
                srows = pl.ds(pl.multiple_of(off + ci * CHUNK, CHUNK), CHUNK)
                q = _silu(q_ref[0, rows, :])
                v = i_ref[0, rows, :].astype(BF16)
                for d in range(2):
                    lb = lbs[d]
                    open_part = (1.0 - lb) * jax.nn.sigmoid(f_refs[d][0, rows, :])
                    f = lb + open_part
                    k = (1.0 - lb) - open_part
                    logf = jnp.log(f)
                    hi = logf.astype(BF16)
                    mid = (logf - hi.astype(F32)).astype(BF16)
                    res = _dot(lmat_ref[d], jnp.concatenate([hi, mid], axis=1))
                    chains.append(dict(d=d, q=q, v=v, k=k, f=f, res=res, srows=srows,
                                       chunk=off // CHUNK + ci))
            for ch in chains:
                d, q, k = ch["d"], ch["q"], ch["k"]
                cum = ch["res"][:, :HEAD] + ch["res"][:, HEAD:]
                total = cum[CHUNK - 1:CHUNK, :] if d == 0 else cum[0:1, :]
                qb, kb = q.astype(BF16), k.astype(BF16)
                dots = [_dot_nt(qb, kb)]
                for lvl in range(1, N_LEVELS):
                    xb = level_factor(2 ** lvl, d, ch["f"], cum).astype(BF16)
                    dots.append(_dot_nt(qb * xb, kb * xb))
                ch["dots"] = dots
                qd_s[d, ch["srows"], :] = (q * jnp.exp(cum)).astype(BF16)
                ch["u"] = _dot_tn(ch["v"], (k * jnp.exp(total - cum)).astype(BF16))
                drow = pl.multiple_of(ch["chunk"] * 8, 8)
                dec_s[d, pl.ds(drow, 8), :] = jnp.broadcast_to(jnp.exp(total), (8, HEAD))
            for ch in chains:
                d = ch["d"]
                ch["y"] = _dot(assemble_scores(d, ch["dots"]).astype(BF16), ch["v"])
                u_s[d, pl.ds(pl.multiple_of(ch["chunk"] * HEAD, HEAD), HEAD), :] = ch["u"]
            for ch in chains:
                y_s[ch["d"], ch["srows"], :] = ch["y"]
            return carry
        lax.fori_loop(0, n // CHUNK // par, body, 0)

    phase_a(qc_ref, (ffc_ref, fbc_ref), ic_ref, 0, n_ctx)
    phase_a(ql_ref, (ffl_ref, fbl_ref), il_ref, n_ctx, n_lat)

    st_s[...] = jnp.zeros_like(st_s)

    steps = _largest_divisor(n_chunks, (4, 2, 1))

    def phase_b(jj, carry):
        states = [st_s[0], st_s[1]]
        for u in range(steps):
            j = jj * steps + u
            for d in range(2):
                r0 = pl.multiple_of(_scan_row(j, d == 1, n_ctx_chunks, n_chunks), CHUNK)
                rows = pl.ds(r0, CHUNK)
                chunk = r0 // CHUNK
                inter = _dot_nt(qd_s[d, rows, :], states[d].astype(BF16))
                dec = dec_s[d, pl.ds(pl.multiple_of(chunk * 8, 8), 1), :]
                y_s[d, rows, :] = y_s[d, rows, :] + inter
                states[d] = states[d] * dec + u_s[d, pl.ds(pl.multiple_of(chunk * HEAD, HEAD), HEAD), :]
        st_s[0] = states[0]
        st_s[1] = states[1]
        return carry

    lax.fori_loop(0, n_chunks // steps, phase_b, 0)

    def readout(gate_ref, o_ref, off, n):
        par = _largest_divisor(n // CHUNK, (4, 2, 1))

        def body(c, carry):
            for u in range(par):
                ci = c * par + u
                rows = pl.ds(pl.multiple_of(ci * CHUNK, CHUNK), CHUNK)
                srows = pl.ds(pl.multiple_of(off + ci * CHUNK, CHUNK), CHUNK)
                y = y_s[0, srows, :] + y_s[1, srows, :]
                ms = jnp.mean(y * y, axis=-1, keepdims=True)
                o_ref[0, rows, :] = (y * lax.rsqrt(ms + EPS) * _silu(gate_ref[0, rows, :])).astype(o_ref.dtype)
            return carry
        lax.fori_loop(0, n // CHUNK // par, body, 0)

    readout(gc_ref, oc_ref, 0, n_ctx)
    readout(gl_ref, ol_ref, n_ctx, n_lat)


def _hgrn2(p_ctx, p_lat, lb_logits, consts, layer):
    _, bsz, n_ctx, _ = p_ctx.shape
    n_lat = p_lat.shape[2]
    t = n_ctx + n_lat
    lmat, mk = consts
    depth = lb_logits.shape[0]
    lbl = lb_logits.reshape(depth, 2, A_HEADS, 1, HEAD).transpose(0, 1, 3, 2, 4).reshape(depth, 2, 1, A_HEADS * HEAD)

    def col(n, c0):
        return pl.BlockSpec((None, 1, n, HEAD), lambda b, h, c0=c0: (c0 + h, b, 0, 0))

    cols = (COL_A_Q, COL_A_FF, COL_A_FB, COL_A_I, COL_A_GATE)
    return pl.pallas_call(
        functools.partial(_hgrn2_kernel, layer=layer),
        grid=(bsz, A_HEADS),
        in_specs=[
            pl.BlockSpec((depth, 2, 1, HEAD), lambda b, h: (0, 0, 0, h)),
            pl.BlockSpec(lmat.shape, lambda b, h: (0, 0, 0)),
            pl.BlockSpec(mk.shape, lambda b, h: (0, 0, 0, 0)),
        ] + [col(n_ctx, c) for c in cols] + [col(n_lat, c) for c in cols],
        out_specs=[
            pl.BlockSpec((1, n_ctx, HEAD), lambda b, h: (b, 0, h)),
            pl.BlockSpec((1, n_lat, HEAD), lambda b, h: (b, 0, h)),
        ],
        out_shape=[
            jax.ShapeDtypeStruct((bsz, n_ctx, A_HEADS * HEAD), BF16),
            jax.ShapeDtypeStruct((bsz, n_lat, A_HEADS * HEAD), BF16),
        ],
        scratch_shapes=[
            pltpu.VMEM((2, t, HEAD), BF16),
            pltpu.VMEM((2, t, HEAD), F32),
            pltpu.VMEM((2, (t // CHUNK) * HEAD, HEAD), F32),
            pltpu.VMEM((2, (t // CHUNK) * 8, HEAD), F32),
            pltpu.VMEM((2, HEAD, HEAD), F32),
        ],
        compiler_params=_params(("arbitrary", "arbitrary")),
        name="hgrn2",
    )(lbl, lmat, mk, *([p_ctx] * 5), *([p_lat] * 5))


def _mlstm_constants():
    n = CHUNK
    a = np.arange(n)[:, None]
    b = np.arange(n)[None, :]
    lt, rhs, causal = [], [], []
    for reverse in (False, True):
        upto = (b >= a) if reverse else (b <= a)
        after = (b < a) if reverse else (b > a)
        lt.append(np.stack([upto, after]))
        m = after.T
        ext = np.concatenate([m, np.zeros((n, HEAD - n)), np.ones((n, HEAD))], axis=1)
        rhs.append(np.concatenate([ext, ext], axis=0))
        causal.append(upto)
    return (jnp.asarray(np.stack(lt), F32), jnp.asarray(np.stack(rhs), BF16),
            jnp.asarray(np.stack(causal), F32))


def _mlstm_kernel(lt_ref, rhs_ref, causal_ref, eye_ref, cw_ref, bias_ref,
                  qc_ref, kc_ref, vc_ref, oc_ref, zc_ref, gc_ref,
                  ql_ref, kl_ref, vl_ref, ol_ref, zl_ref, gl_ref,
                  outc_ref, outl_ref,
                  q_s, s_s, cols_s, u_s, sc_s, st_s):
    h = pl.program_id(1)
    n_ctx = qc_ref.shape[1]
    n_lat = ql_ref.shape[1]
    n_ctx_chunks = n_ctx // CHUNK
    n_chunks = (n_ctx + n_lat) // CHUNK
    row_id = lax.broadcasted_iota(jnp.int32, (CHUNK, 1), 0)
    lane = lax.broadcasted_iota(jnp.int32, (1, HEAD), 1)
    sub8 = lax.broadcasted_iota(jnp.int32, (8, 1), 0)
    gate_row = lax.broadcasted_iota(jnp.int32, (N_GATES, 1), 0)
    eye_c = (row_id == lax.broadcasted_iota(jnp.int32, (1, CHUNK), 1)).astype(F32)
    ones_b = jnp.ones((CHUNK, HEAD), BF16)

    def conv_silu(src, ci, n, w):
        s = pl.multiple_of(ci * CHUNK, CHUNK)
        cur = src[0, pl.ds(s, CHUNK), :]
        up = src[0, pl.ds(pl.multiple_of(jnp.maximum(s - 8, 0), 8), 8), :]
        dn = src[0, pl.ds(pl.multiple_of(jnp.minimum(s + CHUNK, n - 8), 8), 8), :]
        prev_row = jnp.where(s > 0, up[7:8, :], 0.0)
        next_row = jnp.where(s + CHUNK < n, dn[0:1, :], 0.0)
        prev = jnp.where(row_id == 0, prev_row, pltpu.roll(cur, 1, 0))
        nxt = jnp.where(row_id == CHUNK - 1, next_row, pltpu.roll(cur, CHUNK - 1, 0))
        return _silu(w[0:1, :] * prev + w[1:2, :] * cur + w[2:3, :] * nxt)

    def pick_row(a, idx):
        return jnp.sum(jnp.where(gate_row == idx, a, 0.0), axis=0, keepdims=True)

    def phase_a(q_ref, k_ref, v_ref, g_ref, off, n):
        par = _largest_divisor(n // CHUNK, (4, 2, 1))

        def body(c, carry):
            chunks = []
            for u in range(par):
                ci = c * par + u
                rows = pl.ds(pl.multiple_of(ci * CHUNK, CHUNK), CHUNK)
                srows = pl.ds(pl.multiple_of(off + ci * CHUNK, CHUNK), CHUNK)
                gt = _dot3_nt(eye_ref[...], g_ref[0, rows, :]) + bias_ref[:, 0:CHUNK]
                qb = (conv_silu(q_ref, ci, n, cw_ref[0]) * (HEAD ** -0.5)).astype(BF16)
                k = conv_silu(k_ref, ci, n, cw_ref[1])
                q_s[srows, :] = qb
                vaug = jnp.concatenate([v_ref[0, rows, :].astype(BF16), ones_b], axis=1)
                chunks.append(dict(k=k, qk=_dot_nt(qb, k.astype(BF16)), vaug=vaug, gt=gt, srows=srows,
                                   chunk=off // CHUNK + ci))
            chains = []
            for ck in chunks:
                for d in range(2):
                    gi = 2 * B_HEADS * d + h
                    li_row = pick_row(ck["gt"], gi)
                    lf_row = _log_sigmoid(pick_row(ck["gt"], gi + B_HEADS))
                    lhs = jnp.concatenate([lt_ref[d, 0] * lf_row,
                                           lt_ref[d, 1] * lf_row + eye_c * li_row], axis=0)
                    hi = lhs.astype(BF16)
                    mid = (lhs - hi.astype(F32)).astype(BF16)
                    out = _dot(jnp.concatenate([hi, mid], axis=1), rhs_ref[d])
                    chains.append(dict(ck, d=d, li_row=li_row, out=out))
            for ch in chains:
                d, out = ch["d"], ch["out"]
                cum_rep = out[0:CHUNK, HEAD:]
                logw_rep = out[CHUNK:, HEAD:]
                log_d = jnp.where(causal_ref[d] > 0.0, out[0:CHUNK, 0:CHUNK] + ch["li_row"], NEG)
                a_loc = jnp.max(log_d, axis=1, keepdims=True)
                s_s[d, ch["srows"], :] = (ch["qk"] * jnp.exp(log_d - a_loc)).astype(BF16)
                cols_s[d, ch["srows"], :] = jnp.where(lane < CHUNK, a_loc, cum_rep)
                total = cum_rep[CHUNK - 1:CHUNK, :] if d == 0 else cum_rep[0:1, :]
                m_loc = jnp.max(logw_rep, axis=0, keepdims=True)
                kw = (ch["k"] * jnp.exp(logw_rep - m_loc)).astype(BF16)
                ch["u"] = _dot_tn(kw, ch["vaug"])
                sc_s[d, pl.ds(pl.multiple_of(ch["chunk"] * 8, 8), 8), :] = jnp.where(
                    sub8 == 0, total, jnp.where(sub8 == 1, m_loc, 0.0))
            for ch in chains:
                u_s[ch["d"], pl.ds(pl.multiple_of(ch["chunk"] * HEAD, HEAD), HEAD), :] = ch["u"].astype(BF16)
            return carry
        lax.fori_loop(0, n // CHUNK // par, body, 0)

    phase_a(qc_ref, kc_ref, vc_ref, gc_ref, 0, n_ctx)
    phase_a(ql_ref, kl_ref, vl_ref, gl_ref, n_ctx, n_lat)

    st_s[...] = jnp.zeros_like(st_s)
    steps = _largest_divisor(n_chunks, (4, 2, 1))
    for d in range(2):
        def phase_b(jj, m, d=d):
            for u in range(steps):
                chunk = _scan_row(jj * steps + u, d == 1, n_ctx_chunks, n_chunks) // CHUNK
                srow = pl.ds(pl.multiple_of(chunk * 8, 8), 8)
                urow = pl.ds(pl.multiple_of(chunk * HEAD, HEAD), HEAD)
                sc = sc_s[d, srow, :]
                total, m_loc = sc[0:1, :], sc[1:2, :]
                m_new = jnp.maximum(total + m, m_loc)
                state = st_s[d]
                inc = u_s[d, urow, :].astype(F32)
                u_s[d, urow, :] = state.astype(BF16)
                sc_s[d, srow, :] = jnp.where(sub8 == 2, m, sc)
                keep = jnp.exp(total + m - m_new)
                gain = jnp.exp(m_loc - m_new)
                st_s[d] = (jnp.concatenate([keep, keep], axis=1) * state
                           + jnp.concatenate([gain, gain], axis=1) * inc)
                m = m_new
            return m
        lax.fori_loop(0, n_chunks // steps, phase_b, jnp.zeros((1, HEAD), F32))

    def phase_c(v_ref, o_ref, z_ref, out_ref, off, n):
        par = _largest_divisor(n // CHUNK, (4, 2, 1))

        def body(c, carry):
            chains = []
            for u in range(par):
                ci = c * par + u
                rows = pl.ds(pl.multiple_of(ci * CHUNK, CHUNK), CHUNK)
                srows = pl.ds(pl.multiple_of(off + ci * CHUNK, CHUNK), CHUNK)
                chunk = off // CHUNK + ci
                qb = q_s[srows, :]
                vaug = jnp.concatenate([v_ref[0, rows, :].astype(BF16), ones_b], axis=1)
                for d in range(2):
                    intra = _dot(s_s[d, srows, :], vaug)
                    inter = _dot(qb, u_s[d, pl.ds(pl.multiple_of(chunk * HEAD, HEAD), HEAD), :])
                    chains.append(dict(d=d, u=u, rows=rows, srows=srows, chunk=chunk, intra=intra, inter=inter))
            ys = [None] * par
            for ch in chains:
                d = ch["d"]
                cols = cols_s[d, ch["srows"], :]
                swapped = pltpu.roll(cols, CHUNK, 1)
                a_rep = jnp.where(lane < CHUNK, cols, swapped)
                cum_rep = jnp.where(lane < CHUNK, swapped, cols)
                m = sc_s[d, pl.ds(pl.multiple_of(ch["chunk"] * 8, 8), 8), :][2:3, :]
                inter_l = cum_rep + m
                m_t = jnp.maximum(a_rep, inter_l)
                wi = jnp.exp(a_rep - m_t)
                we = jnp.exp(inter_l - m_t)
                num = wi * ch["intra"][:, :HEAD] + we * ch["inter"][:, :HEAD]
                den = wi * ch["intra"][:, HEAD:] + we * ch["inter"][:, HEAD:]
                hd = num / jnp.maximum(jnp.abs(den), jnp.exp(-m_t))
                ys[ch["u"]] = hd if ys[ch["u"]] is None else ys[ch["u"]] + hd
                if d == 1:
                    rows = ch["rows"]
                    hh = jax.nn.sigmoid(o_ref[0, rows, :]) * ys[ch["u"]]
                    ms = jnp.mean(hh * hh, axis=-1, keepdims=True)
                    out_ref[0, rows, :] = (hh * lax.rsqrt(ms + EPS) * _silu(z_ref[0, rows, :])).astype(out_ref.dtype)
            return carry
        lax.fori_loop(0, n // CHUNK // par, body, 0)

    phase_c(vc_ref, oc_ref, zc_ref, outc_ref, 0, n_ctx)
    phase_c(vl_ref, ol_ref, zl_ref, outl_ref, n_ctx, n_lat)


def _mlstm(p_ctx, p_lat, g_ctx, g_lat, conv_w, gate_bias, consts):
    _, bsz, n_ctx, _ = p_ctx.shape
    n_lat = p_lat.shape[2]
    t = n_ctx + n_lat
    n_chunks = t // CHUNK
    lt, rhs, causal = consts
    eye = jnp.eye(N_GATES, HEAD, dtype=BF16)
    cw = conv_w.reshape(3, 2, B_HEADS * HEAD).transpose(1, 0, 2)
    bias = jnp.broadcast_to(gate_bias.reshape(N_GATES, 1), (N_GATES, HEAD))

    def col(n, c0):
        return pl.BlockSpec((None, 1, n, HEAD), lambda b, h, c0=c0: (c0 + h, b, 0, 0))

    def gate_spec(n):
        return pl.BlockSpec((1, n, HEAD), lambda b, h: (b, 0, 0))

    cols = (COL_B_Q, COL_B_K, COL_B_V, COL_B_O, COL_B_Z)
    return pl.pallas_call(
        _mlstm_kernel,
        grid=(bsz, B_HEADS),
        in_specs=[
            pl.BlockSpec(lt.shape, lambda b, h: (0, 0, 0, 0)),
            pl.BlockSpec(rhs.shape, lambda b, h: (0, 0, 0)),
            pl.BlockSpec(causal.shape, lambda b, h: (0, 0, 0)),
            pl.BlockSpec(eye.shape, lambda b, h: (0, 0)),
            pl.BlockSpec((2, 3, HEAD), lambda b, h: (0, 0, h)),
            pl.BlockSpec((N_GATES, HEAD), lambda b, h: (0, 0)),
        ] + [col(n_ctx, c) for c in cols] + [gate_spec(n_ctx)]
          + [col(n_lat, c) for c in cols] + [gate_spec(n_lat)],
        out_specs=[
            pl.BlockSpec((1, n_ctx, HEAD), lambda b, h: (b, 0, h)),
            pl.BlockSpec((1, n_lat, HEAD), lambda b, h: (b, 0, h)),
        ],
        out_shape=[
            jax.ShapeDtypeStruct((bsz, n_ctx, B_HEADS * HEAD), BF16),
            jax.ShapeDtypeStruct((bsz, n_lat, B_HEADS * HEAD), BF16),
        ],
        scratch_shapes=[
            pltpu.VMEM((t, HEAD), BF16),
            pltpu.VMEM((2, t, CHUNK), BF16),
            pltpu.VMEM((2, t, HEAD), F32),
            pltpu.VMEM((2, n_chunks * HEAD, 2 * HEAD), BF16),
            pltpu.VMEM((2, n_chunks * 8, HEAD), F32),
            pltpu.VMEM((2, HEAD, 2 * HEAD), F32),
        ],
        compiler_params=_params(("arbitrary", "arbitrary")),
        name="mlstm",
    )(lt, rhs, causal, eye, cw, bias, *([p_ctx] * 5), g_ctx, *([p_lat] * 5), g_lat)


def _rope(x, cos, sin_lo, sin_hi):
    return x * cos + pltpu.roll(x, HEAD - 32, 1) * sin_lo + pltpu.roll(x, 32, 1) * sin_hi


LOG2E = 1.4426950408889634


def _softmax_attend(sink_col, scores, values):
    m = jnp.maximum(sink_col, jnp.max(scores, axis=1, keepdims=True))
    p = jnp.exp2(scores - m)
    den = jnp.exp2(sink_col - m) + jnp.sum(p, axis=1, keepdims=True)
    return _dot(p.astype(BF16), values) / den


def _sink_column(sink_ref, kv, rows_per_head):
    row = lax.broadcasted_iota(jnp.int32, (C_GROUP * rows_per_head, 1), 0)
    col = jnp.zeros((C_GROUP * rows_per_head, 1), F32)
    for g in range(C_GROUP):
        col = jnp.where(row // rows_per_head == g, sink_ref[kv * C_GROUP + g] * LOG2E, col)
    return col


def _window_masks():
    r = np.arange(ATTN_BLOCK)[:, None]
    c = np.arange(3 * ATTN_BLOCK)[None, :]
    return jnp.asarray(np.stack([np.abs(c - p * ATTN_BLOCK - r) <= WINDOW for p in range(3)]), F32)


def _attn_kernel(sink_ref, mask_ref, cos_ref, slo_ref, shi_ref, q_ref, k_ref, v_ref, kc_ref, vc_ref, z_ref,
                 o_ref, kr_s):
    kv = pl.program_id(1)
    i = pl.program_id(2)
    n = k_ref.shape[1]
    span = 3 * ATTN_BLOCK

    @pl.when(i == 0)
    def _():
        def body(c, carry):
            rows = pl.ds(pl.multiple_of(c * ATTN_BLOCK, ATTN_BLOCK), ATTN_BLOCK)
            kr_s[rows, :] = _rope(k_ref[0, rows, :], cos_ref[rows, :], slo_ref[rows, :],
                                  shi_ref[rows, :]).astype(BF16)
            return carry
        lax.fori_loop(0, n // ATTN_BLOCK, body, 0)

    scale = HEAD ** -0.5 * LOG2E
    blocks_per_step = q_ref.shape[1] // ATTN_BLOCK

    def one_block(sub, carry):
        blk = i * blocks_per_step + sub
        local = pl.ds(pl.multiple_of(sub * ATTN_BLOCK, ATTN_BLOCK), ATTN_BLOCK)
        qrows = pl.ds(pl.multiple_of(blk * ATTN_BLOCK, ATTN_BLOCK), ATTN_BLOCK)
        cos = cos_ref[qrows, :]
        slo = slo_ref[qrows, :]
        shi = shi_ref[qrows, :]
        start = pl.multiple_of(jnp.clip((blk - 1) * ATTN_BLOCK, 0, n - span), ATTN_BLOCK)
        keys = jnp.concatenate([kr_s[pl.ds(start, span), :], kc_ref[0].astype(BF16)], axis=0)
        values = jnp.concatenate([v_ref[0, pl.ds(start, span), :].astype(BF16), vc_ref[0].astype(BF16)],
                                 axis=0)
        valid = mask_ref[(blk * ATTN_BLOCK - start) // ATTN_BLOCK] > 0.0
        scores = []
        for g in range(C_GROUP):
            qg = (_rope(q_ref[g, local, :], cos, slo, shi) * scale).astype(BF16)
            s = _dot_nt(qg, keys)
            scores.append(jnp.concatenate([jnp.where(valid, s[:, :span], NEG), s[:, span:]], axis=1))
        for g in range(C_GROUP):
            sink = jnp.full((ATTN_BLOCK, 1), sink_ref[kv * C_GROUP + g] * LOG2E, F32)
            out = _softmax_attend(sink, scores[g], values)
            o_ref[0, local, g * HEAD:(g + 1) * HEAD] = (out * _silu(z_ref[g, local, :])).astype(o_ref.dtype)
        return carry

    lax.fori_loop(0, blocks_per_step, one_block, 0)


def _attention(p_ctx, p_lat, sink, rope):
    _, bsz, n_ctx, _ = p_ctx.shape
    n = p_lat.shape[2]
    cos, slo, shi = rope
    masks = _window_masks()
    qrows = ATTN_BLOCK * _largest_divisor(n // ATTN_BLOCK, (4, 2, 1))
    gw = C_GROUP * HEAD
    gblk = C_GROUP

    full = pl.BlockSpec((n, HEAD), lambda b, kv, i: (0, 0))
    return pl.pallas_call(
        _attn_kernel,
        grid=(bsz, C_KV_HEADS, n // qrows),
        in_specs=[
            pl.BlockSpec(memory_space=pltpu.SMEM),
            pl.BlockSpec(masks.shape, lambda b, kv, i: (0, 0, 0)),
            full, full, full,
            pl.BlockSpec((C_GROUP, None, qrows, HEAD), lambda b, kv, i: (COL_C_Q // gblk + kv, b, i, 0)),
            pl.BlockSpec((None, 1, n, HEAD), lambda b, kv, i: (COL_C_K + kv, b, 0, 0)),
            pl.BlockSpec((None, 1, n, HEAD), lambda b, kv, i: (COL_C_V + kv, b, 0, 0)),
            pl.BlockSpec((None, 1, n_ctx, HEAD), lambda b, kv, i: (COL_C_K + kv, b, 0, 0)),
            pl.BlockSpec((None, 1, n_ctx, HEAD), lambda b, kv, i: (COL_C_V + kv, b, 0, 0)),
            pl.BlockSpec((C_GROUP, None, qrows, HEAD), lambda b, kv, i: (COL_C_Z // gblk + kv, b, i, 0)),
        ],
        out_specs=pl.BlockSpec((1, qrows, gw), lambda b, kv, i: (b, i, kv)),
        out_shape=jax.ShapeDtypeStruct((bsz, n, C_Q_HEADS * HEAD), BF16),
        scratch_shapes=[pltpu.VMEM((n, HEAD), BF16)],
        compiler_params=_params(("arbitrary", "arbitrary", "arbitrary")),
        name="window_attention",
    )(sink, masks, cos, slo, shi, p_lat, p_lat, p_lat, p_ctx, p_ctx, p_lat)


def _ctx_attn_kernel(sink_ref, q_ref, k_ref, v_ref, z_ref, o_ref):
    kv = pl.program_id(1)
    n_ctx = q_ref.shape[1]
    scale = HEAD ** -0.5 * LOG2E
    q4 = jnp.concatenate([q_ref[g] * scale for g in range(C_GROUP)],
                         axis=0).astype(BF16)
    s = _dot_nt(q4, k_ref[0].astype(BF16))
    out = _softmax_attend(_sink_column(sink_ref, kv, n_ctx), s, v_ref[0].astype(BF16))
    for g in range(C_GROUP):
        zg = z_ref[g]
        o_ref[0, :, g * HEAD:(g + 1) * HEAD] = (out[g * n_ctx:(g + 1) * n_ctx] * _silu(zg)).astype(o_ref.dtype)


def _context_attention(p_ctx, sink):
    _, bsz, n_ctx, _ = p_ctx.shape
    gw = C_GROUP * HEAD
    gblk = C_GROUP
    return pl.pallas_call(
        _ctx_attn_kernel,
        grid=(bsz, C_KV_HEADS),
        in_specs=[
            pl.BlockSpec(memory_space=pltpu.SMEM),
            pl.BlockSpec((C_GROUP, None, n_ctx, HEAD), lambda b, kv: (COL_C_Q // gblk + kv, b, 0, 0)),
            pl.BlockSpec((None, 1, n_ctx, HEAD), lambda b, kv: (COL_C_K + kv, b, 0, 0)),
            pl.BlockSpec((None, 1, n_ctx, HEAD), lambda b, kv: (COL_C_V + kv, b, 0, 0)),
            pl.BlockSpec((C_GROUP, None, n_ctx, HEAD), lambda b, kv: (COL_C_Z // gblk + kv, b, 0, 0)),
        ],
        out_specs=pl.BlockSpec((1, n_ctx, gw), lambda b, kv: (b, 0, kv)),
        out_shape=jax.ShapeDtypeStruct((bsz, n_ctx, C_Q_HEADS * HEAD), BF16),
        compiler_params=_params(("arbitrary", "arbitrary")),
        name="context_attention",
    )(sink, p_ctx, p_ctx, p_ctx, p_ctx)


def _rope_tables(n):
    axis_dim = HEAD // 2
    rows = n // GRID_W
    row = jnp.repeat(jnp.arange(rows, dtype=F32), GRID_W)
    col = jnp.tile(jnp.arange(GRID_W, dtype=F32), rows)
    inv_freq = ROPE_BASE ** (-jnp.arange(0, axis_dim, 2, dtype=F32) / axis_dim)
    ang_r = row[:, None] * inv_freq[None, :]
    ang_c = col[:, None] * inv_freq[None, :]
    zero = jnp.zeros_like(ang_r)
    cos = jnp.concatenate([jnp.cos(ang_r)] * 2 + [jnp.cos(ang_c)] * 2, axis=-1)
    sin_lo = jnp.concatenate([-jnp.sin(ang_r), zero, -jnp.sin(ang_c), zero], axis=-1)
    sin_hi = jnp.concatenate([zero, jnp.sin(ang_r), zero, jnp.sin(ang_c)], axis=-1)
    return cos, sin_lo, sin_hi


def _pick_tile(n, candidates):
    for c in candidates:
        if n % c == 0:
            return c
    raise ValueError(f"no tile for {n}")


def kernel(x, c, ctx, c_ctx, w_mod, b_mod, g_pre, g_post, w_in, hgrn_lb_logits, mlstm_conv_w,
           mlstm_gate_bias, attn_sink, w_out):
    bsz, n_lat, d = x.shape
    n_ctx = ctx.shape[1]
    depth = w_mod.shape[0]
    assert n_ctx % CHUNK == 0 and n_lat % ATTN_BLOCK == 0 and n_lat >= 3 * ATTN_BLOCK

    cond = jnp.concatenate([c, c_ctx[None, :]], axis=0)
    pad = (-cond.shape[0]) % 8
    cond = jnp.pad(cond, ((0, pad), (0, 0)))
    mod = _modulation(cond, w_mod, b_mod)

    w_t = _weight_prep(jnp.swapaxes(w_in, 1, 2))
    wo = w_out.astype(BF16)
    consts = _scan_constants()
    mlstm_consts = _mlstm_constants()
    rope = _rope_tables(n_lat)

    tm_lat = _pick_tile(n_lat, (1024, 512, 256, 128))
    tm_ctx = _pick_tile(bsz * n_ctx, (1024, 512, 256, 128))
    tn = _pick_tile(MAIN_WIDTH, (1280, 768, 512))
    to_lat = _pick_tile(n_lat, (512, 256, 128))
    to_ctx = _pick_tile(bsz * n_ctx, (512, 256, 128))

    x2d = x.reshape(bsz * n_lat, d)
    h2d = ctx.reshape(bsz * n_ctx, d)
    for layer in range(depth):
        with_ctx = layer < depth - 1
        shift_l = mod[layer, :bsz, 0:d].reshape(bsz, 1, d)
        scale_l = mod[layer, :bsz, d:2 * d].reshape(bsz, 1, d)
        gate_l = mod[layer, :bsz, 2 * d:].reshape(bsz, 1, d)
        shift_c = mod[layer, bsz:bsz + 1, 0:d].reshape(1, 1, d)
        scale_c = mod[layer, bsz:bsz + 1, d:2 * d].reshape(1, 1, d)
        gate_c = mod[layer, bsz:bsz + 1, 2 * d:].reshape(1, 1, d)

        gain_pre = g_pre[layer].reshape(1, d)
        gain_post = g_post[layer].reshape(1, d)

        p_lat, g_lat = _in_projection(x2d, shift_l, scale_l, gain_pre, w_t, layer, n_lat, tm_lat, tn)
        p_ctx, g_ctx = _in_projection(h2d, shift_c, scale_c, gain_pre, w_t, layer, bsz * n_ctx, tm_ctx, tn)
        p_lat = p_lat.reshape(MAIN_BLOCKS, bsz, n_lat, HEAD)
        p_ctx = p_ctx.reshape(MAIN_BLOCKS, bsz, n_ctx, HEAD)
        g_lat = g_lat.reshape(bsz, n_lat, HEAD)
        g_ctx = g_ctx.reshape(bsz, n_ctx, HEAD)

        a_c, a_l = _hgrn2(p_ctx, p_lat, hgrn_lb_logits, consts, layer)
        b_c, b_l = _mlstm(p_ctx, p_lat, g_ctx, g_lat, mlstm_conv_w[layer], mlstm_gate_bias[layer],
                          mlstm_consts)
        c_l = _attention(p_ctx, p_lat, attn_sink[layer], rope)

        x2d = _out_projection(a_l.reshape(bsz * n_lat, -1), b_l.reshape(bsz * n_lat, -1),
                              c_l.reshape(bsz * n_lat, -1), wo, layer, x2d, gate_l, gain_post, n_lat, to_lat)
        if with_ctx:
            c_c = _context_attention(p_ctx, attn_sink[layer])
            h2d = _out_projection(a_c.reshape(bsz * n_ctx, -1), b_c.reshape(bsz * n_ctx, -1),
                                  c_c.reshape(bsz * n_ctx, -1), wo, layer, h2d, gate_c, gain_post,
                                  bsz * n_ctx, to_ctx)
    return x2d.reshape(bsz, n_lat, d)
```

```python
import functools

import numpy as np
import jax
import jax.numpy as jnp
from jax import lax
from jax.experimental import pallas as pl
from jax.experimental.pallas import tpu as pltpu

F32 = jnp.float32
BF16 = jnp.bfloat16

EPS = 1e-6
NEG = -1e30
HEAD = 128
A_HEADS = 4
B_HEADS = 4
C_Q_HEADS = 8
C_KV_HEADS = 2
C_GROUP = C_Q_HEADS // C_KV_HEADS
CHUNK = 64
ATTN_BLOCK = 128
WINDOW = 128
GRID_W = 64
ROPE_BASE = 10000.0
N_GATES = 4 * B_HEADS

COL_A_Q, COL_A_FF, COL_A_FB, COL_A_I, COL_A_GATE = 0, 4, 8, 12, 16
COL_B_Q, COL_B_K, COL_B_V, COL_B_O, COL_B_Z = 20, 24, 28, 32, 36
COL_C_Q, COL_C_K, COL_C_V, COL_C_Z = 40, 48, 50, 52
MAIN_BLOCKS = 60
MAIN_WIDTH = MAIN_BLOCKS * HEAD
GATES_OFFSET = 9 * 4 * HEAD

VMEM_LIMIT = 56 * 1024 * 1024


def _params(sem):
    return pltpu.CompilerParams(dimension_semantics=sem, vmem_limit_bytes=VMEM_LIMIT)


def _silu(x):
    return x * jax.nn.sigmoid(x)


def _log_sigmoid(x):
    return jnp.minimum(x, 0.0) - jnp.log(1.0 + jnp.exp(-jnp.abs(x)))


def _dot(a, b):
    return jnp.dot(a, b, preferred_element_type=F32)


def _dot_nt(a, b):
    return lax.dot_general(a, b, (((1,), (1,)), ((), ())), preferred_element_type=F32)


def _dot_tn(a, b):
    return lax.dot_general(a, b, (((0,), (0,)), ((), ())), preferred_element_type=F32)


def _split3(x):
    hi = x.astype(BF16)
    r = x - hi.astype(F32)
    mid = r.astype(BF16)
    lo = (r - mid.astype(F32)).astype(BF16)
    return hi, mid, lo


def _dot3_nt(w, x):
    hi, mid, lo = _split3(x)
    return _dot_nt(w, hi) + _dot_nt(w, mid) + _dot_nt(w, lo)


def _mod_kernel(c_ref, w_ref, b_ref, o_ref):
    a = _silu(c_ref[...])
    o_ref[0] = jnp.dot(a, w_ref[0], preferred_element_type=F32,
                       precision=lax.Precision.HIGHEST) + b_ref[0]


def _modulation(cond, w_mod, b_mod):
    depth, d, n = w_mod.shape
    rows = cond.shape[0]
    tn = 768
    assert n % tn == 0
    return pl.pallas_call(
        _mod_kernel,
        grid=(depth, n // tn),
        in_specs=[
            pl.BlockSpec((rows, d), lambda l, j: (0, 0)),
            pl.BlockSpec((1, d, tn), lambda l, j: (l, 0, j)),
            pl.BlockSpec((1, 1, tn), lambda l, j: (l, 0, j)),
        ],
        out_specs=pl.BlockSpec((1, rows, tn), lambda l, j: (l, 0, j)),
        out_shape=jax.ShapeDtypeStruct((depth, rows, n), F32),
        compiler_params=_params(("arbitrary", "arbitrary")),
        name="modulation",
    )(cond, w_mod, b_mod.reshape(depth, 1, n))


NORM_ROWS = 32
EPILOGUE_ROWS = 8


def _inproj_kernel(x_ref, shift_ref, scale_ref, g_ref, w_ref, wg_ref, p_ref, gates_ref, xn_ref, gm_ref):
    tm = x_ref.shape[0]

    @pl.when(pl.program_id(1) == 0)
    def _():
        gm_ref[...] = g_ref[...] * (1.0 + scale_ref[0])

        def body(r, carry):
            for u in range(2):
                rows = pl.ds(pl.multiple_of((2 * r + u) * NORM_ROWS, NORM_ROWS), NORM_ROWS)
                xr = x_ref[rows, :]
                ms = jnp.mean(xr * xr, axis=-1, keepdims=True)
                xn_ref[rows, :] = (xr * (lax.rsqrt(ms + EPS) * gm_ref[...]) + shift_ref[0]).astype(BF16)
            return carry

        lax.fori_loop(0, tm // NORM_ROWS // 2, body, 0)
        gates_ref[...] = _dot_nt(xn_ref[...], wg_ref[...])

    res = _dot_nt(xn_ref[...], w_ref[...])
    for t in range(p_ref.shape[0]):
        p_ref[t] = res[:, t * HEAD:(t + 1) * HEAD]


PREP_ROWS = 512
MAIN_PREP_BLOCKS = MAIN_WIDTH // PREP_ROWS
assert GATES_OFFSET % PREP_ROWS == 0 and MAIN_WIDTH % PREP_ROWS == 0


def _weight_prep_kernel(w_ref, o_ref):
    c = pl.program_id(1)
    row = lax.broadcasted_iota(jnp.int32, (PREP_ROWS, 1), 0)
    keep = jnp.logical_or(c < MAIN_PREP_BLOCKS, row < N_GATES)
    o_ref[...] = jnp.where(keep, w_ref[0], 0.0).astype(BF16)


def _weight_prep(w_in_t):
    depth, n, d = w_in_t.shape

    def src_row(c):
        after = jnp.where(c * PREP_ROWS >= GATES_OFFSET, N_GATES, 0)
        return pl.multiple_of(jnp.where(c < MAIN_PREP_BLOCKS, c * PREP_ROWS + after, GATES_OFFSET), N_GATES)

    return pl.pallas_call(
        _weight_prep_kernel,
        grid=(depth, MAIN_PREP_BLOCKS + 1),
        in_specs=[pl.BlockSpec((pl.Element(1), pl.Element(PREP_ROWS), pl.Element(d)),
                               lambda l, c: (l, src_row(c), 0))],
        out_specs=pl.BlockSpec((None, PREP_ROWS, d), lambda l, c: (l, c, 0)),
        out_shape=jax.ShapeDtypeStruct((depth, MAIN_WIDTH + PREP_ROWS, d), BF16),
        compiler_params=_params(("arbitrary", "arbitrary")),
        name="weight_prep",
    )(w_in_t)


def _in_projection(x2d, shift, scale, gain, w_t, layer, rows_per_mod, tm, tn):
    r, d = x2d.shape
    n = MAIN_WIDTH
    assert r % tm == 0 and n % tn == 0 and rows_per_mod % tm == 0
    per = rows_per_mod // tm
    return pl.pallas_call(
        _inproj_kernel,
        grid=(r // tm, n // tn),
        in_specs=[
            pl.BlockSpec((tm, d), lambda i, j: (i, 0)),
            pl.BlockSpec((1, 1, d), lambda i, j: (i // per, 0, 0)),
            pl.BlockSpec((1, 1, d), lambda i, j: (i // per, 0, 0)),
            pl.BlockSpec((1, d), lambda i, j: (0, 0)),
            pl.BlockSpec((None, tn, d), lambda i, j: (layer, j, 0)),
            pl.BlockSpec((None, HEAD, d), lambda i, j: (layer, MAIN_BLOCKS, 0)),
        ],
        out_specs=[
            pl.BlockSpec((tn // HEAD, tm, HEAD), lambda i, j: (j, i, 0)),
            pl.BlockSpec((tm, HEAD), lambda i, j: (i, 0)),
        ],
        out_shape=[
            jax.ShapeDtypeStruct((n // HEAD, r, HEAD), F32),
            jax.ShapeDtypeStruct((r, HEAD), F32),
        ],
        scratch_shapes=[pltpu.VMEM((tm, d), BF16), pltpu.VMEM((1, d), F32)],
        compiler_params=_params(("arbitrary", "arbitrary")),
        name="in_projection",
    )(x2d, shift, scale, gain, w_t, w_t)


def _outproj_kernel(a_ref, b_ref, c_ref, w_ref, x_ref, gate_ref, g_ref, o_ref, y_s):
    i = pl.program_id(0)
    tm = x_ref.shape[0]
    slot = i % 2

    @pl.when(i == 0)
    def _():
        y_s[1] = jnp.zeros(y_s.shape[1:], F32)

    for r in range(tm // EPILOGUE_ROWS):
        rows = slice(r * EPILOGUE_ROWS, (r + 1) * EPILOGUE_ROWS)
        y = y_s[1 - slot, rows, :]
        ms = jnp.mean(y * y, axis=-1, keepdims=True)
        o_ref[rows, :] = x_ref[rows, :] + y * (lax.rsqrt(ms + EPS) * (gate_ref[0] * g_ref[...]))
    y_s[slot] = _dot(jnp.concatenate([a_ref[...], b_ref[...], c_ref[...]], axis=1), w_ref[...])


def _out_projection(a, b, c, w_out, layer, x2d, gate, gain, rows_per_mod, tm):
    r, d = x2d.shape
    assert r % tm == 0 and rows_per_mod % tm == 0
    per = rows_per_mod // tm
    n = r // tm
    cur = lambda i: (jnp.minimum(i, n - 1), 0)
    prev = lambda i: (jnp.maximum(i - 1, 0), 0)
    return pl.pallas_call(
        _outproj_kernel,
        grid=(n + 1,),
        in_specs=[
            pl.BlockSpec((tm, a.shape[1]), cur),
            pl.BlockSpec((tm, b.shape[1]), cur),
            pl.BlockSpec((tm, c.shape[1]), cur),
            pl.BlockSpec((None,) + w_out.shape[1:], lambda i: (layer, 0, 0)),
            pl.BlockSpec((tm, d), prev),
            pl.BlockSpec((1, 1, d), lambda i: (jnp.maximum(i - 1, 0) // per, 0, 0)),
            pl.BlockSpec((1, d), lambda i: (0, 0)),
        ],
        out_specs=pl.BlockSpec((tm, d), prev),
        out_shape=jax.ShapeDtypeStruct((r, d), F32),
        scratch_shapes=[pltpu.VMEM((2, tm, d), F32)],
        compiler_params=_params(("arbitrary",)),
        name="out_projection",
    )(a, b, c, w_out, x2d, gate, gain)


N_LEVELS = 7


def _scan_matrices(reverse):
    n = CHUNK
    t = np.arange(n)[:, None]
    u = np.arange(n)[None, :]
    cum = (u >= t) if reverse else (u <= t)
    masks = [t == u]
    size = 2
    while size <= n:
        half = size // 2
        same = (t // size) == (u // size)
        if reverse:
            masks.append(same & ((t % size) < half) & ((u % size) >= half))
        else:
            masks.append(same & ((t % size) >= half) & ((u % size) < half))
        size *= 2
    return cum.astype(np.float32), np.stack(masks).astype(np.float32)


def _scan_constants():
    cf, mf = _scan_matrices(False)
    cb, mb = _scan_matrices(True)
    return (jnp.asarray(np.stack([cf, cb]), BF16), jnp.asarray(np.stack([mf, mb]), F32))


def _largest_divisor(n, candidates):
    return next(c for c in candidates if n % c == 0)


def _scan_row(j, reverse, n_ctx_chunks, n_chunks):
    if not reverse:
        return j * CHUNK
    ctx_part = (n_ctx_chunks - 1 - j) * CHUNK
    lat_part = (n_ctx_chunks + (n_chunks - 1 - j)) * CHUNK
    return jnp.where(j < n_ctx_chunks, ctx_part, lat_part)


def _hgrn2_kernel(lbl_ref, lmat_ref, mk_ref,
                  qc_ref, ffc_ref, fbc_ref, ic_ref, gc_ref,
                  ql_ref, ffl_ref, fbl_ref, il_ref, gl_ref,
                  oc_ref, ol_ref,
                  qd_s, y_s, u_s, dec_s, st_s, *, layer):
    n_ctx = qc_ref.shape[1]
    n_lat = ql_ref.shape[1]
    n_ctx_chunks = n_ctx // CHUNK
    n_chunks = (n_ctx + n_lat) // CHUNK

    logits = lbl_ref[...]
    e = jnp.exp(logits - jnp.max(logits, axis=0, keepdims=True))
    sm = e / jnp.sum(e, axis=0, keepdims=True)
    lbs = jnp.sum(sm[:layer + 1], axis=0) - sm[0]

    row = lax.broadcasted_iota(jnp.int32, (CHUNK, 1), 0)
    r4 = row % 4

    def level_factor(size, d, f, cum):
        half = size // 2
        if size == 2:
            return jnp.where((row % 2 == 1) if d == 0 else (row % 2 == 0), f, 1.0)
        if size == 4:
            f_next = pltpu.roll(f, CHUNK - 1, 0)
            f_prev = pltpu.roll(f, 1, 0)
            if d == 0:
                return jnp.where(r4 == 0, f_next, jnp.where(r4 == 1, 1.0, jnp.where(r4 == 2, f, f * f_prev)))
            return jnp.where(r4 == 0, f * f_next, jnp.where(r4 == 1, f, jnp.where(r4 == 2, 1.0, f_prev)))
        mids = [b * size + (half - 1 if d == 0 else half) for b in range(CHUNK // size)]
        ref = jnp.concatenate([jnp.broadcast_to(cum[m:m + 1, :], (size, HEAD)) for m in mids], axis=0)
        return jnp.exp(-jnp.abs(cum - ref))

    key_lane = lax.broadcasted_iota(jnp.int32, (1, CHUNK), 1)
    fine_levels = 4

    def assemble_scores(d, dots):
        groups = []
        for g in range(CHUNK // 8):
            r = slice(8 * g, 8 * g + 8)
            acc = mk_ref[d, 0, r, :] * dots[0][r]
            for lvl in range(1, fine_levels):
                acc = acc + mk_ref[d, lvl, r, :] * dots[lvl][r]
            for lvl in range(fine_levels, N_LEVELS):
                size = 2 ** lvl
                half = size // 2
                base = (8 * g // size) * size
                on_query_side = (8 * g - base >= half) if d == 0 else (8 * g - base < half)
                if on_query_side:
                    lo = base if d == 0 else base + half
                    acc = jnp.where((key_lane >= lo) & (key_lane < lo + half), dots[lvl][r], acc)
            groups.append(acc)
        return jnp.concatenate(groups, axis=0)

    def phase_a(q_ref, f_refs, i_ref, off, n):
        par = _largest_divisor(n // CHUNK, (8, 4, 2, 1))

        def body(c, carry):
            chains = []
            for u in range(par):
                ci = c * par + u
                rows = pl.ds(pl.multiple_of(ci * CHUNK, CHUNK), CHUNK)
                srows = pl.ds(pl.multiple_of(off + ci * CHUNK, CHUNK), CHUNK)
                q = _silu(q_ref[0, rows, :])
                v = i_ref[0, rows, :].astype(BF16)
                for d in range(2):
                    lb = lbs[d]
                    open_part = (1.0 - lb) * jax.nn.sigmoid(f_refs[d][0, rows, :])
                    f = lb + open_part
                    k = (1.0 - lb) - open_part
                    logf = jnp.log(f)
                    hi = logf.astype(BF16)
                    mid = (logf - hi.astype(F32)).astype(BF16)
                    res = _dot(lmat_ref[d], jnp.concatenate([hi, mid], axis=1))
                    chains.append(dict(d=d, q=q, v=v, k=k, f=f, res=res, srows=srows,
                                       chunk=off // CHUNK + ci))
            for ch in chains:
                d, q, k = ch["d"], ch["q"], ch["k"]
                cum = ch["res"][:, :HEAD] + ch["res"][:, HEAD:]
                total = cum[CHUNK - 1:CHUNK, :] if d == 0 else cum[0:1, :]
                qb, kb = q.astype(BF16), k.astype(BF16)
                dots = [_dot_nt(qb, kb)]
                for lvl in range(1, N_LEVELS):
                    xb = level_factor(2 ** lvl, d, ch["f"], cum).astype(BF16)
                    dots.append(_dot_nt(qb * xb, kb * xb))
                ch["dots"] = dots
                qd_s[d, ch["srows"], :] = (q * jnp.exp(cum)).astype(BF16)
                ch["u"] = _dot_tn(ch["v"], (k * jnp.exp(total - cum)).astype(BF16))
                drow = pl.multiple_of(ch["chunk"] * 8, 8)
                dec_s[d, pl.ds(drow, 8), :] = jnp.broadcast_to(jnp.exp(total), (8, HEAD))
            for ch in chains:
                d = ch["d"]
                ch["y"] = _dot(assemble_scores(d, ch["dots"]).astype(BF16), ch["v"])
                u_s[d, pl.ds(pl.multiple_of(ch["chunk"] * HEAD, HEAD), HEAD), :] = ch["u"]
            for ch in chains:
                y_s[ch["d"], ch["srows"], :] = ch["y"]
            return carry
        lax.fori_loop(0, n // CHUNK // par, body, 0)

    phase_a(qc_ref, (ffc_ref, fbc_ref), ic_ref, 0, n_ctx)
    phase_a(ql_ref, (ffl_ref, fbl_ref), il_ref, n_ctx, n_lat)

    st_s[...] = jnp.zeros_like(st_s)

    steps = _largest_divisor(n_chunks, (17, 16, 8, 4, 2, 1))

    def phase_b(jj, carry):
        states = [st_s[0], st_s[1]]
        for u in range(steps):
            j = jj * steps + u
            for d in range(2):
                r0 = pl.multiple_of(_scan_row(j, d == 1, n_ctx_chunks, n_chunks), CHUNK)
                rows = pl.ds(r0, CHUNK)
                chunk = r0 // CHUNK
                inter = _dot_nt(qd_s[d, rows, :], states[d].astype(BF16))
                dec = dec_s[d, pl.ds(pl.multiple_of(chunk * 8, 8), 1), :]
                y_s[d, rows, :] = y_s[d, rows, :] + inter
                states[d] = states[d] * dec + u_s[d, pl.ds(pl.multiple_of(chunk * HEAD, HEAD), HEAD), :]
        st_s[0] = states[0]
        st_s[1] = states[1]
        return carry

    lax.fori_loop(0, n_chunks // steps, phase_b, 0)

    def readout(gate_ref, o_ref, off, n):
        par = _largest_divisor(n // CHUNK, (4, 2, 1))

        def body(c, carry):
            for u in range(par):
                ci = c * par + u
                rows = pl.ds(pl.multiple_of(ci * CHUNK, CHUNK), CHUNK)
                srows = pl.ds(pl.multiple_of(off + ci * CHUNK, CHUNK), CHUNK)
                y = y_s[0, srows, :] + y_s[1, srows, :]
                ms = jnp.mean(y * y, axis=-1, keepdims=True)
                o_ref[0, rows, :] = (y * lax.rsqrt(ms + EPS) * _silu(gate_ref[0, rows, :])).astype(o_ref.dtype)
            return carry
        lax.fori_loop(0, n // CHUNK // par, body, 0)

    readout(gc_ref, oc_ref, 0, n_ctx)
    readout(gl_ref, ol_ref, n_ctx, n_lat)


def _hgrn2(p_ctx, p_lat, lb_logits, consts, layer):
    _, bsz, n_ctx, _ = p_ctx.shape
    n_lat = p_lat.shape[2]
    t = n_ctx + n_lat
    lmat, mk = consts
    depth = lb_logits.shape[0]
    lbl = lb_logits.reshape(depth, 2, A_HEADS, 1, HEAD).transpose(0, 1, 3, 2, 4).reshape(depth, 2, 1, A_HEADS * HEAD)

    def col(n, c0):
        return pl.BlockSpec((None, 1, n, HEAD), lambda b, h, c0=c0: (c0 + h, b, 0, 0))

    cols = (COL_A_Q, COL_A_FF, COL_A_FB, COL_A_I, COL_A_GATE)
    return pl.pallas_call(
        functools.partial(_hgrn2_kernel, layer=layer),
        grid=(bsz, A_HEADS),
        in_specs=[
            pl.BlockSpec((depth, 2, 1, HEAD), lambda b, h: (0, 0, 0, h)),
            pl.BlockSpec(lmat.shape, lambda b, h: (0, 0, 0)),
            pl.BlockSpec(mk.shape, lambda b, h: (0, 0, 0, 0)),
        ] + [col(n_ctx, c) for c in cols] + [col(n_lat, c) for c in cols],
        out_specs=[
            pl.BlockSpec((1, n_ctx, HEAD), lambda b, h: (b, 0, h)),
            pl.BlockSpec((1, n_lat, HEAD), lambda b, h: (b, 0, h)),
        ],
        out_shape=[
            jax.ShapeDtypeStruct((bsz, n_ctx, A_HEADS * HEAD), BF16),
            jax.ShapeDtypeStruct((bsz, n_lat, A_HEADS * HEAD), BF16),
        ],
        scratch_shapes=[
            pltpu.VMEM((2, t, HEAD), BF16),
            pltpu.VMEM((2, t, HEAD), F32),
            pltpu.VMEM((2, (t // CHUNK) * HEAD, HEAD), F32),
            pltpu.VMEM((2, (t // CHUNK) * 8, HEAD), F32),
            pltpu.VMEM((2, HEAD, HEAD), F32),
        ],
        compiler_params=_params(("arbitrary", "arbitrary")),
        name="hgrn2",
    )(lbl, lmat, mk, *([p_ctx] * 5), *([p_lat] * 5))


def _mlstm_constants():
    n = CHUNK
    a = np.arange(n)[:, None]
    b = np.arange(n)[None, :]
    lt, rhs, causal = [], [], []
    for reverse in (False, True):
        upto = (b >= a) if reverse else (b <= a)
        after = (b < a) if reverse else (b > a)
        lt.append(np.stack([upto, after]))
        m = after.T
        ext = np.concatenate([m, np.zeros((n, HEAD - n)), np.ones((n, HEAD))], axis=1)
        rhs.append(np.concatenate([ext, ext], axis=0))
        causal.append(upto)
    return (jnp.asarray(np.stack(lt), F32), jnp.asarray(np.stack(rhs), BF16),
            jnp.asarray(np.stack(causal), F32))


def _mlstm_kernel(lt_ref, rhs_ref, causal_ref, eye_ref, cw_ref, bias_ref,
                  qc_ref, kc_ref, vc_ref, oc_ref, zc_ref, gc_ref,
                  ql_ref, kl_ref, vl_ref, ol_ref, zl_ref, gl_ref,
                  outc_ref, outl_ref,
                  q_s, s_s, cols_s, u_s, sc_s, st_s):
    h = pl.program_id(1)
    n_ctx = qc_ref.shape[1]
    n_lat = ql_ref.shape[1]
    n_ctx_chunks = n_ctx // CHUNK
    n_chunks = (n_ctx + n_lat) // CHUNK
    row_id = lax.broadcasted_iota(jnp.int32, (CHUNK, 1), 0)
    lane = lax.broadcasted_iota(jnp.int32, (1, HEAD), 1)
    sub8 = lax.broadcasted_iota(jnp.int32, (8, 1), 0)
    gate_row = lax.broadcasted_iota(jnp.int32, (N_GATES, 1), 0)
    eye_c = (row_id == lax.broadcasted_iota(jnp.int32, (1, CHUNK), 1)).astype(F32)
    ones_b = jnp.ones((CHUNK, HEAD), BF16)

    def conv_silu(src, ci, n, w):
        s = pl.multiple_of(ci * CHUNK, CHUNK)
        cur = src[0, pl.ds(s, CHUNK), :]
        up = src[0, pl.ds(pl.multiple_of(jnp.maximum(s - 8, 0), 8), 8), :]
        dn = src[0, pl.ds(pl.multiple_of(jnp.minimum(s + CHUNK, n - 8), 8), 8), :]
        prev_row = jnp.where(s > 0, up[7:8, :], 0.0)
        next_row = jnp.where(s + CHUNK < n, dn[0:1, :], 0.0)
        prev = jnp.where(row_id == 0, prev_row, pltpu.roll(cur, 1, 0))
        nxt = jnp.where(row_id == CHUNK - 1, next_row, pltpu.roll(cur, CHUNK - 1, 0))
        return _silu(w[0:1, :] * prev + w[1:2, :] * cur + w[2:3, :] * nxt)

    def pick_row(a, idx):
        return jnp.sum(jnp.where(gate_row == idx, a, 0.0), axis=0, keepdims=True)

    def phase_a(q_ref, k_ref, v_ref, g_ref, off, n):
        par = _largest_divisor(n // CHUNK, (8, 4, 2, 1))

        def body(c, carry):
            chunks = []
            for u in range(par):
                ci = c * par + u
                rows = pl.ds(pl.multiple_of(ci * CHUNK, CHUNK), CHUNK)
                srows = pl.ds(pl.multiple_of(off + ci * CHUNK, CHUNK), CHUNK)
                gt = _dot3_nt(eye_ref[...], g_ref[0, rows, :]) + bias_ref[:, 0:CHUNK]
                qb = (conv_silu(q_ref, ci, n, cw_ref[0]) * (HEAD ** -0.5)).astype(BF16)
                k = conv_silu(k_ref, ci, n, cw_ref[1])
                q_s[srows, :] = qb
                vaug = jnp.concatenate([v_ref[0, rows, :].astype(BF16), ones_b], axis=1)
                chunks.append(dict(k=k, qk=_dot_nt(qb, k.astype(BF16)), vaug=vaug, gt=gt, srows=srows,
                                   chunk=off // CHUNK + ci))
            chains = []
            for ck in chunks:
                for d in range(2):
                    gi = 2 * B_HEADS * d + h
                    li_row = pick_row(ck["gt"], gi)
                    lf_row = _log_sigmoid(pick_row(ck["gt"], gi + B_HEADS))
                    lhs = jnp.concatenate([lt_ref[d, 0] * lf_row,
                                           lt_ref[d, 1] * lf_row + eye_c * li_row], axis=0)
                    hi = lhs.astype(BF16)
                    mid = (lhs - hi.astype(F32)).astype(BF16)
                    out = _dot(jnp.concatenate([hi, mid], axis=1), rhs_ref[d])
                    chains.append(dict(ck, d=d, li_row=li_row, out=out))
            for ch in chains:
                d, out = ch["d"], ch["out"]
                cum_rep = out[0:CHUNK, HEAD:]
                logw_rep = out[CHUNK:, HEAD:]
                log_d = jnp.where(causal_ref[d] > 0.0, out[0:CHUNK, 0:CHUNK] + ch["li_row"], NEG)
                a_loc = jnp.max(log_d, axis=1, keepdims=True)
                s_s[d, ch["srows"], :] = (ch["qk"] * jnp.exp(log_d - a_loc)).astype(BF16)
                cols_s[d, ch["srows"], :] = jnp.where(lane < CHUNK, a_loc, cum_rep)
                total = cum_rep[CHUNK - 1:CHUNK, :] if d == 0 else cum_rep[0:1, :]
                m_loc = jnp.max(logw_rep, axis=0, keepdims=True)
                kw = (ch["k"] * jnp.exp(logw_rep - m_loc)).astype(BF16)
                ch["u"] = _dot_tn(kw, ch["vaug"])
                sc_s[d, pl.ds(pl.multiple_of(ch["chunk"] * 8, 8), 8), :] = jnp.where(
                    sub8 == 0, total, jnp.where(sub8 == 1, m_loc, 0.0))
            for ch in chains:
                u_s[ch["d"], pl.ds(pl.multiple_of(ch["chunk"] * HEAD, HEAD), HEAD), :] = ch["u"].astype(BF16)
            return carry
        lax.fori_loop(0, n // CHUNK // par, body, 0)

    phase_a(qc_ref, kc_ref, vc_ref, gc_ref, 0, n_ctx)
    phase_a(ql_ref, kl_ref, vl_ref, gl_ref, n_ctx, n_lat)

    st_s[...] = jnp.zeros_like(st_s)
    steps = _largest_divisor(n_chunks, (4, 2, 1))
    for d in range(2):
        def phase_b(jj, m, d=d):
            for u in range(steps):
                chunk = _scan_row(jj * steps + u, d == 1, n_ctx_chunks, n_chunks) // CHUNK
                srow = pl.ds(pl.multiple_of(chunk * 8, 8), 8)
                urow = pl.ds(pl.multiple_of(chunk * HEAD, HEAD), HEAD)
                sc = sc_s[d, srow, :]
                total, m_loc = sc[0:1, :], sc[1:2, :]
                m_new = jnp.maximum(total + m, m_loc)
                state = st_s[d]
                inc = u_s[d, urow, :].astype(F32)
                u_s[d, urow, :] = state.astype(BF16)
                sc_s[d, srow, :] = jnp.where(sub8 == 2, m, sc)
                keep = jnp.exp(total + m - m_new)
                gain = jnp.exp(m_loc - m_new)
                st_s[d] = (jnp.concatenate([keep, keep], axis=1) * state
                           + jnp.concatenate([gain, gain], axis=1) * inc)
                m = m_new
            return m
        lax.fori_loop(0, n_chunks // steps, phase_b, jnp.zeros((1, HEAD), F32))

    def phase_c(v_ref, o_ref, z_ref, out_ref, off, n):
        par = _largest_divisor(n // CHUNK, (8, 4, 2, 1))

        def body(c, carry):
            chains = []
            for u in range(par):
                ci = c * par + u
                rows = pl.ds(pl.multiple_of(ci * CHUNK, CHUNK), CHUNK)
                srows = pl.ds(pl.multiple_of(off + ci * CHUNK, CHUNK), CHUNK)
                chunk = off // CHUNK + ci
                qb = q_s[srows, :]
                vaug = jnp.concatenate([v_ref[0, rows, :].astype(BF16), ones_b], axis=1)
                for d in range(2):
                    intra = _dot(s_s[d, srows, :], vaug)
                    inter = _dot(qb, u_s[d, pl.ds(pl.multiple_of(chunk * HEAD, HEAD), HEAD), :])
                    chains.append(dict(d=d, u=u, rows=rows, srows=srows, chunk=chunk, intra=intra, inter=inter))
            ys = [None] * par
            for ch in chains:
                d = ch["d"]
                cols = cols_s[d, ch["srows"], :]
                swapped = pltpu.roll(cols, CHUNK, 1)
                a_rep = jnp.where(lane < CHUNK, cols, swapped)
                cum_rep = jnp.where(lane < CHUNK, swapped, cols)
                m = sc_s[d, pl.ds(pl.multiple_of(ch["chunk"] * 8, 8), 8), :][2:3, :]
                inter_l = cum_rep + m
                m_t = jnp.maximum(a_rep, inter_l)
                wi = jnp.exp(a_rep - m_t)
                we = jnp.exp(inter_l - m_t)
                num = wi * ch["intra"][:, :HEAD] + we * ch["inter"][:, :HEAD]
                den = wi * ch["intra"][:, HEAD:] + we * ch["inter"][:, HEAD:]
                hd = num / jnp.maximum(jnp.abs(den), jnp.exp(-m_t))
                ys[ch["u"]] = hd if ys[ch["u"]] is None else ys[ch["u"]] + hd
                if d == 1:
                    rows = ch["rows"]
                    hh = jax.nn.sigmoid(o_ref[0, rows, :]) * ys[ch["u"]]
                    ms = jnp.mean(hh * hh, axis=-1, keepdims=True)
                    out_ref[0, rows, :] = (hh * lax.rsqrt(ms + EPS) * _silu(z_ref[0, rows, :])).astype(out_ref.dtype)
            return carry
        lax.fori_loop(0, n // CHUNK // par, body, 0)

    phase_c(vc_ref, oc_ref, zc_ref, outc_ref, 0, n_ctx)
    phase_c(vl_ref, ol_ref, zl_ref, outl_ref, n_ctx, n_lat)


def _mlstm(p_ctx, p_lat, g_ctx, g_lat, conv_w, gate_bias, consts):
    _, bsz, n_ctx, _ = p_ctx.shape
    n_lat = p_lat.shape[2]
    t = n_ctx + n_lat
    n_chunks = t // CHUNK
    lt, rhs, causal = consts
    eye = jnp.eye(N_GATES, HEAD, dtype=BF16)
    cw = conv_w.reshape(3, 2, B_HEADS * HEAD).transpose(1, 0, 2)
    bias = jnp.broadcast_to(gate_bias.reshape(N_GATES, 1), (N_GATES, HEAD))

    def col(n, c0):
        return pl.BlockSpec((None, 1, n, HEAD), lambda b, h, c0=c0: (c0 + h, b, 0, 0))

    def gate_spec(n):
        return pl.BlockSpec((1, n, HEAD), lambda b, h: (b, 0, 0))

    cols = (COL_B_Q, COL_B_K, COL_B_V, COL_B_O, COL_B_Z)
    return pl.pallas_call(
        _mlstm_kernel,
        grid=(bsz, B_HEADS),
        in_specs=[
            pl.BlockSpec(lt.shape, lambda b, h: (0, 0, 0, 0)),
            pl.BlockSpec(rhs.shape, lambda b, h: (0, 0, 0)),
            pl.BlockSpec(causal.shape, lambda b, h: (0, 0, 0)),
            pl.BlockSpec(eye.shape, lambda b, h: (0, 0)),
            pl.BlockSpec((2, 3, HEAD), lambda b, h: (0, 0, h)),
            pl.BlockSpec((N_GATES, HEAD), lambda b, h: (0, 0)),
        ] + [col(n_ctx, c) for c in cols] + [gate_spec(n_ctx)]
          + [col(n_lat, c) for c in cols] + [gate_spec(n_lat)],
        out_specs=[
            pl.BlockSpec((1, n_ctx, HEAD), lambda b, h: (b, 0, h)),
            pl.BlockSpec((1, n_lat, HEAD), lambda b, h: (b, 0, h)),
        ],
        out_shape=[
            jax.ShapeDtypeStruct((bsz, n_ctx, B_HEADS * HEAD), BF16),
            jax.ShapeDtypeStruct((bsz, n_lat, B_HEADS * HEAD), BF16),
        ],
        scratch_shapes=[
            pltpu.VMEM((t, HEAD), BF16),
            pltpu.VMEM((2, t, CHUNK), BF16),
            pltpu.VMEM((2, t, HEAD), F32),
            pltpu.VMEM((2, n_chunks * HEAD, 2 * HEAD), BF16),
            pltpu.VMEM((2, n_chunks * 8, HEAD), F32),
            pltpu.VMEM((2, HEAD, 2 * HEAD), F32),
        ],
        compiler_params=_params(("arbitrary", "arbitrary")),
        name="mlstm",
    )(lt, rhs, causal, eye, cw, bias, *([p_ctx] * 5), g_ctx, *([p_lat] * 5), g_lat)


def _rope(x, cos, sin_lo, sin_hi):
    return x * cos + pltpu.roll(x, HEAD - 32, 1) * sin_lo + pltpu.roll(x, 32, 1) * sin_hi


LOG2E = 1.4426950408889634


def _softmax_attend(sink_col, scores, values):
    m = jnp.maximum(sink_col, jnp.max(scores, axis=1, keepdims=True))
    p = jnp.exp2(scores - m)
    den = jnp.exp2(sink_col - m) + jnp.sum(p, axis=1, keepdims=True)
    return _dot(p.astype(BF16), values) / den


def _sink_column(sink_ref, kv, rows_per_head):
    row = lax.broadcasted_iota(jnp.int32, (C_GROUP * rows_per_head, 1), 0)
    col = jnp.zeros((C_GROUP * rows_per_head, 1), F32)
    for g in range(C_GROUP):
        col = jnp.where(row // rows_per_head == g, sink_ref[kv * C_GROUP + g] * LOG2E, col)
    return col


def _window_masks():
    r = np.arange(ATTN_BLOCK)[:, None]
    c = np.arange(3 * ATTN_BLOCK)[None, :]
    return jnp.asarray(np.stack([np.abs(c - p * ATTN_BLOCK - r) <= WINDOW for p in range(3)]), F32)


def _attn_kernel(sink_ref, mask_ref, cos_ref, slo_ref, shi_ref, q_ref, k_ref, v_ref, kc_ref, vc_ref, z_ref,
                 o_ref, kr_s):
    kv = pl.program_id(1)
    i = pl.program_id(2)
    n = k_ref.shape[1]
    span = 3 * ATTN_BLOCK

    @pl.when(i == 0)
    def _():
        def body(c, carry):
            rows = pl.ds(pl.multiple_of(c * ATTN_BLOCK, ATTN_BLOCK), ATTN_BLOCK)
            kr_s[rows, :] = _rope(k_ref[0, rows, :], cos_ref[rows, :], slo_ref[rows, :],
                                  shi_ref[rows, :]).astype(BF16)
            return carry
        lax.fori_loop(0, n // ATTN_BLOCK, body, 0)

    scale = HEAD ** -0.5 * LOG2E
    blocks_per_step = q_ref.shape[1] // ATTN_BLOCK

    def one_block(sub, carry):
        blk = i * blocks_per_step + sub
        local = pl.ds(pl.multiple_of(sub * ATTN_BLOCK, ATTN_BLOCK), ATTN_BLOCK)
        qrows = pl.ds(pl.multiple_of(blk * ATTN_BLOCK, ATTN_BLOCK), ATTN_BLOCK)
        cos = cos_ref[qrows, :]
        slo = slo_ref[qrows, :]
        shi = shi_ref[qrows, :]
        start = pl.multiple_of(jnp.clip((blk - 1) * ATTN_BLOCK, 0, n - span), ATTN_BLOCK)
        keys = jnp.concatenate([kr_s[pl.ds(start, span), :], kc_ref[0].astype(BF16)], axis=0)
        values = jnp.concatenate([v_ref[0, pl.ds(start, span), :].astype(BF16), vc_ref[0].astype(BF16)],
                                 axis=0)
        valid = mask_ref[(blk * ATTN_BLOCK - start) // ATTN_BLOCK] > 0.0
        scores = []
        for g in range(C_GROUP):
            qg = (_rope(q_ref[g, local, :], cos, slo, shi) * scale).astype(BF16)
            s = _dot_nt(qg, keys)
            scores.append(jnp.concatenate([jnp.where(valid, s[:, :span], NEG), s[:, span:]], axis=1))
        for g in range(C_GROUP):
            sink = jnp.full((ATTN_BLOCK, 1), sink_ref[kv * C_GROUP + g] * LOG2E, F32)
            out = _softmax_attend(sink, scores[g], values)
            o_ref[0, local, g * HEAD:(g + 1) * HEAD] = (out * _silu(z_ref[g, local, :])).astype(o_ref.dtype)
        return carry

    lax.fori_loop(0, blocks_per_step, one_block, 0)


def _attention(p_ctx, p_lat, sink, rope):
    _, bsz, n_ctx, _ = p_ctx.shape
    n = p_lat.shape[2]
    cos, slo, shi = rope
    masks = _window_masks()
    qrows = ATTN_BLOCK * _largest_divisor(n // ATTN_BLOCK, (4, 2, 1))
    gw = C_GROUP * HEAD
    gblk = C_GROUP

    full = pl.BlockSpec((n, HEAD), lambda b, kv, i: (0, 0))
    return pl.pallas_call(
        _attn_kernel,
        grid=(bsz, C_KV_HEADS, n // qrows),
        in_specs=[
            pl.BlockSpec(memory_space=pltpu.SMEM),
            pl.BlockSpec(masks.shape, lambda b, kv, i: (0, 0, 0)),
            full, full, full,
            pl.BlockSpec((C_GROUP, None, qrows, HEAD), lambda b, kv, i: (COL_C_Q // gblk + kv, b, i, 0)),
            pl.BlockSpec((None, 1, n, HEAD), lambda b, kv, i: (COL_C_K + kv, b, 0, 0)),
            pl.BlockSpec((None, 1, n, HEAD), lambda b, kv, i: (COL_C_V + kv, b, 0, 0)),
            pl.BlockSpec((None, 1, n_ctx, HEAD), lambda b, kv, i: (COL_C_K + kv, b, 0, 0)),
            pl.BlockSpec((None, 1, n_ctx, HEAD), lambda b, kv, i: (COL_C_V + kv, b, 0, 0)),
            pl.BlockSpec((C_GROUP, None, qrows, HEAD), lambda b, kv, i: (COL_C_Z // gblk + kv, b, i, 0)),
        ],
        out_specs=pl.BlockSpec((1, qrows, gw), lambda b, kv, i: (b, i, kv)),
        out_shape=jax.ShapeDtypeStruct((bsz, n, C_Q_HEADS * HEAD), BF16),
        scratch_shapes=[pltpu.VMEM((n, HEAD), BF16)],
        compiler_params=_params(("arbitrary", "arbitrary", "arbitrary")),
        name="window_attention",
    )(sink, masks, cos, slo, shi, p_lat, p_lat, p_lat, p_ctx, p_ctx, p_lat)


def _ctx_attn_kernel(sink_ref, q_ref, k_ref, v_ref, z_ref, o_ref):
    kv = pl.program_id(1)
    n_ctx = q_ref.shape[1]
    scale = HEAD ** -0.5 * LOG2E
    q4 = jnp.concatenate([q_ref[g] * scale for g in range(C_GROUP)],
                         axis=0).astype(BF16)
    s = _dot_nt(q4, k_ref[0].astype(BF16))
    out = _softmax_attend(_sink_column(sink_ref, kv, n_ctx), s, v_ref[0].astype(BF16))
    for g in range(C_GROUP):
        zg = z_ref[g]
        o_ref[0, :, g * HEAD:(g + 1) * HEAD] = (out[g * n_ctx:(g + 1) * n_ctx] * _silu(zg)).astype(o_ref.dtype)


def _context_attention(p_ctx, sink):
    _, bsz, n_ctx, _ = p_ctx.shape
    gw = C_GROUP * HEAD
    gblk = C_GROUP
    return pl.pallas_call(
        _ctx_attn_kernel,
        grid=(bsz, C_KV_HEADS),
        in_specs=[
            pl.BlockSpec(memory_space=pltpu.SMEM),
            pl.BlockSpec((C_GROUP, None, n_ctx, HEAD), lambda b, kv: (COL_C_Q // gblk + kv, b, 0, 0)),
            pl.BlockSpec((None, 1, n_ctx, HEAD), lambda b, kv: (COL_C_K + kv, b, 0, 0)),
            pl.BlockSpec((None, 1, n_ctx, HEAD), lambda b, kv: (COL_C_V + kv, b, 0, 0)),
            pl.BlockSpec((C_GROUP, None, n_ctx, HEAD), lambda b, kv: (COL_C_Z // gblk + kv, b, 0, 0)),
        ],
        out_specs=pl.BlockSpec((1, n_ctx, gw), lambda b, kv: (b, 0, kv)),
        out_shape=jax.ShapeDtypeStruct((bsz, n_ctx, C_Q_HEADS * HEAD), BF16),
        compiler_params=_params(("arbitrary", "arbitrary")),
        name="context_attention",
    )(sink, p_ctx, p_ctx, p_ctx, p_ctx)


def _rope_tables(n):
    axis_dim = HEAD // 2
    rows = n // GRID_W
    row = jnp.repeat(jnp.arange(rows, dtype=F32), GRID_W)
    col = jnp.tile(jnp.arange(GRID_W, dtype=F32), rows)
    inv_freq = ROPE_BASE ** (-jnp.arange(0, axis_dim, 2, dtype=F32) / axis_dim)
    ang_r = row[:, None] * inv_freq[None, :]
    ang_c = col[:, None] * inv_freq[None, :]
    zero = jnp.zeros_like(ang_r)
    cos = jnp.concatenate([jnp.cos(ang_r)] * 2 + [jnp.cos(ang_c)] * 2, axis=-1)
    sin_lo = jnp.concatenate([-jnp.sin(ang_r), zero, -jnp.sin(ang_c), zero], axis=-1)
    sin_hi = jnp.concatenate([zero, jnp.sin(ang_r), zero, jnp.sin(ang_c)], axis=-1)
    return cos, sin_lo, sin_hi


def _pick_tile(n, candidates):
    for c in candidates:
        if n % c == 0:
            return c
    raise ValueError(f"no tile for {n}")


def kernel(x, c, ctx, c_ctx, w_mod, b_mod, g_pre, g_post, w_in, hgrn_lb_logits, mlstm_conv_w,
           mlstm_gate_bias, attn_sink, w_out):
    bsz, n_lat, d = x.shape
    n_ctx = ctx.shape[1]
    depth = w_mod.shape[0]
    assert n_ctx % CHUNK == 0 and n_lat % ATTN_BLOCK == 0 and n_lat >= 3 * ATTN_BLOCK

    cond = jnp.concatenate([c, c_ctx[None, :]], axis=0)
    pad = (-cond.shape[0]) % 8
    cond = jnp.pad(cond, ((0, pad), (0, 0)))
    mod = _modulation(cond, w_mod, b_mod)

    w_t = _weight_prep(jnp.swapaxes(w_in, 1, 2))
    wo = w_out.astype(BF16)
    consts = _scan_constants()
    mlstm_consts = _mlstm_constants()
    rope = _rope_tables(n_lat)

    tm_lat = _pick_tile(n_lat, (1024, 512, 256, 128))
    tm_ctx = _pick_tile(bsz * n_ctx, (1024, 512, 256, 128))
    tn = _pick_tile(MAIN_WIDTH, (1536, 512))
    to_lat = _pick_tile(n_lat, (512, 256, 128))
    to_ctx = _pick_tile(bsz * n_ctx, (512, 256, 128))

    x2d = x.reshape(bsz * n_lat, d)
    h2d = ctx.reshape(bsz * n_ctx, d)
    for layer in range(depth):
        with_ctx = layer < depth - 1
        shift_l = mod[layer, :bsz, 0:d].reshape(bsz, 1, d)
        scale_l = mod[layer, :bsz, d:2 * d].reshape(bsz, 1, d)
        gate_l = mod[layer, :bsz, 2 * d:].reshape(bsz, 1, d)
        shift_c = mod[layer, bsz:bsz + 1, 0:d].reshape(1, 1, d)
        scale_c = mod[layer, bsz:bsz + 1, d:2 * d].reshape(1, 1, d)
        gate_c = mod[layer, bsz:bsz + 1, 2 * d:].reshape(1, 1, d)

        gain_pre = g_pre[layer].reshape(1, d)
        gain_post = g_post[layer].reshape(1, d)

        p_lat, g_lat = _in_projection(x2d, shift_l, scale_l, gain_pre, w_t, layer, n_lat, tm_lat, tn)
        p_ctx, g_ctx = _in_projection(h2d, shift_c, scale_c, gain_pre, w_t, layer, bsz * n_ctx, tm_ctx, tn)
        p_lat = p_lat.reshape(MAIN_BLOCKS, bsz, n_lat, HEAD)
        p_ctx = p_ctx.reshape(MAIN_BLOCKS, bsz, n_ctx, HEAD)
        g_lat = g_lat.reshape(bsz, n_lat, HEAD)
        g_ctx = g_ctx.reshape(bsz, n_ctx, HEAD)

        a_c, a_l = _hgrn2(p_ctx, p_lat, hgrn_lb_logits, consts, layer)
        b_c, b_l = _mlstm(p_ctx, p_lat, g_ctx, g_lat, mlstm_conv_w[layer], mlstm_gate_bias[layer],
                          mlstm_consts)
        c_l = _attention(p_ctx, p_lat, attn_sink[layer], rope)

        x2d = _out_projection(a_l.reshape(bsz * n_lat, -1), b_l.reshape(bsz * n_lat, -1),
                              c_l.reshape(bsz * n_lat, -1), wo, layer, x2d, gate_l, gain_post, n_lat, to_lat)
        if with_ctx:
            c_c = _context_attention(p_ctx, attn_sink[layer])
            h2d = _out_projection(a_c.reshape(bsz * n_ctx, -1), b_c.reshape(bsz * n_ctx, -1),
                                  c_c.reshape(bsz * n_ctx, -1), wo, layer, h2d, gate_c, gain_post,
                                  bsz * n_ctx, to_ctx)
    return x2d.reshape(bsz, n_lat, d)
```

```python
import functools

import numpy as np
import jax
import jax.numpy as jnp
from jax import lax
from jax.experimental import pallas as pl
from jax.experimental.pallas import tpu as pltpu

F32 = jnp.float32
BF16 = jnp.bfloat16

EPS = 1e-6
NEG = -1e30
LOG2E = 1.4426950408889634
HEAD = 128
A_HEADS = 4
B_HEADS = 4
C_Q_HEADS = 8
C_KV_HEADS = 2
C_GROUP = C_Q_HEADS // C_KV_HEADS
CHUNK = 64
ATTN_BLOCK = 128
WINDOW = 128
GRID_W = 64
ROPE_BASE = 10000.0
N_GATES = 4 * B_HEADS

COL_A_Q, COL_A_FF, COL_A_FB, COL_A_I, COL_A_GATE = 0, 4, 8, 12, 16
COL_B_Q, COL_B_K, COL_B_V, COL_B_O, COL_B_Z = 20, 24, 28, 32, 36
COL_C_Q, COL_C_K, COL_C_V, COL_C_Z = 40, 48, 50, 52
MAIN_BLOCKS = 60
MAIN_WIDTH = MAIN_BLOCKS * HEAD
GATES_OFFSET = 9 * 4 * HEAD

VMEM_LIMIT = 56 * 1024 * 1024


def _params(sem):
    return pltpu.CompilerParams(dimension_semantics=sem, vmem_limit_bytes=VMEM_LIMIT)


def _silu(x):
    return x * jax.nn.sigmoid(x)


def _log_sigmoid(x):
    return jnp.minimum(x, 0.0) - jnp.log(1.0 + jnp.exp(-jnp.abs(x)))


def _dot(a, b):
    return jnp.dot(a, b, preferred_element_type=F32)


def _dot_nt(a, b):
    return lax.dot_general(a, b, (((1,), (1,)), ((), ())), preferred_element_type=F32)


def _dot_tn(a, b):
    return lax.dot_general(a, b, (((0,), (0,)), ((), ())), preferred_element_type=F32)


def _split3(x):
    hi = x.astype(BF16)
    r = x - hi.astype(F32)
    mid = r.astype(BF16)
    lo = (r - mid.astype(F32)).astype(BF16)
    return hi, mid, lo


def _split2(x):
    hi = x.astype(BF16)
    return hi, (x - hi.astype(F32)).astype(BF16)


def _dot3_nt(w, x):
    hi, mid, lo = _split3(x)
    return _dot_nt(w, hi) + _dot_nt(w, mid) + _dot_nt(w, lo)


def _mod_kernel(c_ref, w_ref, b_ref, o_ref):
    a = _silu(c_ref[...])
    o_ref[0] = jnp.dot(a, w_ref[0], preferred_element_type=F32,
                       precision=lax.Precision.HIGHEST) + b_ref[0]


def _modulation(cond, w_mod, b_mod):
    depth, d, n = w_mod.shape
    rows = cond.shape[0]
    tn = _pick_tile(n, (1536, 768, 384, 128))
    assert n % tn == 0
    return pl.pallas_call(
        _mod_kernel,
        grid=(depth, n // tn),
        in_specs=[
            pl.BlockSpec((rows, d), lambda l, j: (0, 0)),
            pl.BlockSpec((1, d, tn), lambda l, j: (l, 0, j)),
            pl.BlockSpec((1, 1, tn), lambda l, j: (l, 0, j)),
        ],
        out_specs=pl.BlockSpec((1, rows, tn), lambda l, j: (l, 0, j)),
        out_shape=jax.ShapeDtypeStruct((depth, rows, n), F32),
        compiler_params=_params(("arbitrary", "arbitrary")),
        name="modulation",
    )(cond, w_mod, b_mod.reshape(depth, 1, n))


NORM_ROWS = 32
EPILOGUE_ROWS = 8


def _inproj_kernel(x_ref, shift_ref, scale_ref, g_ref, w_ref, wg_ref, p_ref, gates_ref, xn_ref, gm_ref):
    tm = x_ref.shape[0]

    @pl.when(pl.program_id(1) == 0)
    def _():
        gm_ref[...] = g_ref[...] * (1.0 + scale_ref[0])

        def body(r, carry):
            for u in range(2):
                rows = pl.ds(pl.multiple_of((2 * r + u) * NORM_ROWS, NORM_ROWS), NORM_ROWS)
                xr = x_ref[rows, :]
                ms = jnp.mean(xr * xr, axis=-1, keepdims=True)
                xn_ref[rows, :] = (xr * (lax.rsqrt(ms + EPS) * gm_ref[...]) + shift_ref[0]).astype(BF16)
            return carry

        lax.fori_loop(0, tm // NORM_ROWS // 2, body, 0)
        gates_ref[...] = _dot_nt(xn_ref[...], wg_ref[...])

    res = _dot_nt(xn_ref[...], w_ref[...])
    for t in range(p_ref.shape[0]):
        p_ref[t] = res[:, t * HEAD:(t + 1) * HEAD]


PREP_ROWS = 512
MAIN_PREP_BLOCKS = MAIN_WIDTH // PREP_ROWS
assert GATES_OFFSET % PREP_ROWS == 0 and MAIN_WIDTH % PREP_ROWS == 0


def _weight_prep_kernel(w_ref, o_ref):
    c = pl.program_id(1)
    row = lax.broadcasted_iota(jnp.int32, (PREP_ROWS, 1), 0)
    keep = jnp.logical_or(c < MAIN_PREP_BLOCKS, row < N_GATES)
    o_ref[...] = jnp.where(keep, w_ref[0], 0.0).astype(BF16)


def _weight_prep(w_in_t):
    depth, n, d = w_in_t.shape

    def src_row(c):
        after = jnp.where(c * PREP_ROWS >= GATES_OFFSET, N_GATES, 0)
        return pl.multiple_of(jnp.where(c < MAIN_PREP_BLOCKS, c * PREP_ROWS + after, GATES_OFFSET), N_GATES)

    return pl.pallas_call(
        _weight_prep_kernel,
        grid=(depth, MAIN_PREP_BLOCKS + 1),
        in_specs=[pl.BlockSpec((pl.Element(1), pl.Element(PREP_ROWS), pl.Element(d)),
                               lambda l, c: (l, src_row(c), 0))],
        out_specs=pl.BlockSpec((None, PREP_ROWS, d), lambda l, c: (l, c, 0)),
        out_shape=jax.ShapeDtypeStruct((depth, MAIN_WIDTH + PREP_ROWS, d), BF16),
        compiler_params=_params(("arbitrary", "arbitrary")),
        name="weight_prep",
    )(w_in_t)


def _in_projection(x2d, shift, scale, gain, w_t, layer, rows_per_mod, tm, tn):
    r, d = x2d.shape
    n = MAIN_WIDTH
    assert r % tm == 0 and n % tn == 0 and rows_per_mod % tm == 0
    per = rows_per_mod // tm
    return pl.pallas_call(
        _inproj_kernel,
        grid=(r // tm, n // tn),
        in_specs=[
            pl.BlockSpec((tm, d), lambda i, j: (i, 0)),
            pl.BlockSpec((1, 1, d), lambda i, j: (i // per, 0, 0)),
            pl.BlockSpec((1, 1, d), lambda i, j: (i // per, 0, 0)),
            pl.BlockSpec((1, d), lambda i, j: (0, 0)),
            pl.BlockSpec((None, tn, d), lambda i, j: (layer, j, 0)),
            pl.BlockSpec((None, HEAD, d), lambda i, j: (layer, MAIN_BLOCKS, 0)),
        ],
        out_specs=[
            pl.BlockSpec((tn // HEAD, tm, HEAD), lambda i, j: (j, i, 0)),
            pl.BlockSpec((tm, HEAD), lambda i, j: (i, 0)),
        ],
        out_shape=[
            jax.ShapeDtypeStruct((n // HEAD, r, HEAD), F32),
            jax.ShapeDtypeStruct((r, HEAD), F32),
        ],
        scratch_shapes=[pltpu.VMEM((tm, d), BF16), pltpu.VMEM((1, d), F32)],
        compiler_params=_params(("arbitrary", "arbitrary")),
        name="in_projection",
    )(x2d, shift, scale, gain, w_t, w_t)


def _outproj_kernel(a_ref, b_ref, c_ref, w_ref, x_ref, gate_ref, g_ref, o_ref, y_s):
    i = pl.program_id(0)
    tm = x_ref.shape[0]
    slot = i % 2

    @pl.when(i == 0)
    def _():
        y_s[1] = jnp.zeros(y_s.shape[1:], F32)

    for r in range(tm // EPILOGUE_ROWS):
        rows = slice(r * EPILOGUE_ROWS, (r + 1) * EPILOGUE_ROWS)
        y = y_s[1 - slot, rows, :]
        ms = jnp.mean(y * y, axis=-1, keepdims=True)
        o_ref[rows, :] = x_ref[rows, :] + y * (lax.rsqrt(ms + EPS) * (gate_ref[0] * g_ref[...]))
    y_s[slot] = _dot(jnp.concatenate([a_ref[...], b_ref[...], c_ref[...]], axis=1), w_ref[...])


def _out_projection(a, b, c, w_out, layer, x2d, gate, gain, rows_per_mod, tm):
    r, d = x2d.shape
    assert r % tm == 0 and rows_per_mod % tm == 0
    per = rows_per_mod // tm
    n = r // tm
    cur = lambda i: (jnp.minimum(i, n - 1), 0)
    prev = lambda i: (jnp.maximum(i - 1, 0), 0)
    return pl.pallas_call(
        _outproj_kernel,
        grid=(n + 1,),
        in_specs=[
            pl.BlockSpec((tm, a.shape[1]), cur),
            pl.BlockSpec((tm, b.shape[1]), cur),
            pl.BlockSpec((tm, c.shape[1]), cur),
            pl.BlockSpec((None,) + w_out.shape[1:], lambda i: (layer, 0, 0)),
            pl.BlockSpec((tm, d), prev),
            pl.BlockSpec((1, 1, d), lambda i: (jnp.maximum(i - 1, 0) // per, 0, 0)),
            pl.BlockSpec((1, d), lambda i: (0, 0)),
        ],
        out_specs=pl.BlockSpec((tm, d), prev),
        out_shape=jax.ShapeDtypeStruct((r, d), F32),
        scratch_shapes=[pltpu.VMEM((2, tm, d), F32)],
        compiler_params=_params(("arbitrary",)),
        name="out_projection",
    )(a, b, c, w_out, x2d, gate, gain)


N_LEVELS = 7


def _scan_matrices(reverse):
    n = CHUNK
    t = np.arange(n)[:, None]
    u = np.arange(n)[None, :]
    cum = (u >= t) if reverse else (u <= t)
    masks = [t == u]
    size = 2
    while size <= n:
        half = size // 2
        same = (t // size) == (u // size)
        if reverse:
            masks.append(same & ((t % size) < half) & ((u % size) >= half))
        else:
            masks.append(same & ((t % size) >= half) & ((u % size) < half))
        size *= 2
    return cum.astype(np.float32), np.stack(masks).astype(np.float32)


def _scan_constants():
    cf, mf = _scan_matrices(False)
    cb, mb = _scan_matrices(True)
    return (jnp.asarray(np.stack([cf, cb]), BF16), jnp.asarray(np.stack([mf, mb]), F32))


def _largest_divisor(n, candidates):
    return next(c for c in candidates if n % c == 0)


def _scan_row(j, reverse, n_ctx_chunks, n_chunks):
    if not reverse:
        return j * CHUNK
    ctx_part = (n_ctx_chunks - 1 - j) * CHUNK
    lat_part = (n_ctx_chunks + (n_chunks - 1 - j)) * CHUNK
    return jnp.where(j < n_ctx_chunks, ctx_part, lat_part)


def _hgrn2_kernel(lbl_ref, lmat_ref, mk_ref,
                  qc_ref, ffc_ref, fbc_ref, ic_ref, gc_ref,
                  ql_ref, ffl_ref, fbl_ref, il_ref, gl_ref,
                  oc_ref, ol_ref,
                  qd_s, y_s, u_s, dec_s, st_s, *, layer):
    n_ctx = qc_ref.shape[1]
    n_lat = ql_ref.shape[1]
    n_ctx_chunks = n_ctx // CHUNK
    n_chunks = (n_ctx + n_lat) // CHUNK

    logits = lbl_ref[...]
    e = jnp.exp(logits - jnp.max(logits, axis=0, keepdims=True))
    sm = e / jnp.sum(e, axis=0, keepdims=True)
    lbs = jnp.sum(sm[:layer + 1], axis=0) - sm[0]

    row = lax.broadcasted_iota(jnp.int32, (CHUNK, 1), 0)
    r4 = row % 4

    def level_factor(size, d, f, cum):
        half = size // 2
        if size == 2:
            return jnp.where((row % 2 == 1) if d == 0 else (row % 2 == 0), f, 1.0)
        if size == 4:
            f_next = pltpu.roll(f, CHUNK - 1, 0)
            f_prev = pltpu.roll(f, 1, 0)
            if d == 0:
                return jnp.where(r4 == 0, f_next, jnp.where(r4 == 1, 1.0, jnp.where(r4 == 2, f, f * f_prev)))
            return jnp.where(r4 == 0, f * f_next, jnp.where(r4 == 1, f, jnp.where(r4 == 2, 1.0, f_prev)))
        mids = [b * size + (half - 1 if d == 0 else half) for b in range(CHUNK // size)]
        ref = jnp.concatenate([jnp.broadcast_to(cum[m:m + 1, :], (size, HEAD)) for m in mids], axis=0)
        return jnp.exp2(-jnp.abs(cum - ref))

    key_lane = lax.broadcasted_iota(jnp.int32, (1, CHUNK), 1)
    fine_levels = 4

    def assemble_scores(d, dots):
        groups = []
        for g in range(CHUNK // 8):
            r = slice(8 * g, 8 * g + 8)
            acc = mk_ref[d, 0, r, :] * dots[0][r]
            for lvl in range(1, fine_levels):
                acc = acc + mk_ref[d, lvl, r, :] * dots[lvl][r]
            for lvl in range(fine_levels, N_LEVELS):
                size = 2 ** lvl
                half = size // 2
                base = (8 * g // size) * size
                on_query_side = (8 * g - base >= half) if d == 0 else (8 * g - base < half)
                if on_query_side:
                    lo = base if d == 0 else base + half
                    acc = jnp.where((key_lane >= lo) & (key_lane < lo + half), dots[lvl][r], acc)
            groups.append(acc)
        return jnp.concatenate(groups, axis=0)

    def phase_a(q_ref, f_refs, i_ref, off, n):
        par = _largest_divisor(n // CHUNK, (8, 4, 2, 1))

        def body(c, carry):
            chains = []
            for u in range(par):
                ci = c * par + u
                rows = pl.ds(pl.multiple_of(ci * CHUNK, CHUNK), CHUNK)
                srows = pl.ds(pl.multiple_of(off + ci * CHUNK, CHUNK), CHUNK)
                q = _silu(q_ref[0, rows, :])
                v = i_ref[0, rows, :].astype(BF16)
                for d in range(2):
                    lb = lbs[d]
                    open_part = (1.0 - lb) * jax.nn.sigmoid(f_refs[d][0, rows, :])
                    f = lb + open_part
                    k = (1.0 - lb) - open_part
                    logf = jnp.log(f) * LOG2E
                    res = _dot(lmat_ref[d], jnp.concatenate(_split2(logf), axis=1))
                    chains.append(dict(d=d, q=q, v=v, k=k, f=f, res=res, srows=srows,
                                       chunk=off // CHUNK + ci))
            for ch in chains:
                d, q, k = ch["d"], ch["q"], ch["k"]
                cum = ch["res"][:, :HEAD] + ch["res"][:, HEAD:]
                total = cum[CHUNK - 1:CHUNK, :] if d == 0 else cum[0:1, :]
                qb, kb = q.astype(BF16), k.astype(BF16)
                dots = [_dot_nt(qb, kb)]
                for lvl in range(1, N_LEVELS):
                    xb = level_factor(2 ** lvl, d, ch["f"], cum).astype(BF16)
                    dots.append(_dot_nt(qb * xb, kb * xb))
                ch["dots"] = dots
                qd_s[d, ch["srows"], :] = (q * jnp.exp2(cum)).astype(BF16)
                ch["u"] = _dot_tn(ch["v"], (k * jnp.exp2(total - cum)).astype(BF16))
                drow = pl.multiple_of(ch["chunk"] * 8, 8)
                dec_s[d, pl.ds(drow, 8), :] = jnp.broadcast_to(jnp.exp2(total), (8, HEAD))
            for ch in chains:
                d = ch["d"]
                ch["y"] = _dot(assemble_scores(d, ch["dots"]).astype(BF16), ch["v"])
                u_s[d, pl.ds(pl.multiple_of(ch["chunk"] * HEAD, HEAD), HEAD), :] = ch["u"]
            for ch in chains:
                y_s[ch["d"], ch["srows"], :] = ch["y"]
            return carry
        lax.fori_loop(0, n // CHUNK // par, body, 0)

    phase_a(qc_ref, (ffc_ref, fbc_ref), ic_ref, 0, n_ctx)
    phase_a(ql_ref, (ffl_ref, fbl_ref), il_ref, n_ctx, n_lat)

    st_s[...] = jnp.zeros_like(st_s)

    steps = _largest_divisor(n_chunks, (17, 16, 8, 4, 2, 1))

    def phase_b(jj, carry):
        states = [st_s[0], st_s[1]]
        for u in range(steps):
            j = jj * steps + u
            for d in range(2):
                r0 = pl.multiple_of(_scan_row(j, d == 1, n_ctx_chunks, n_chunks), CHUNK)
                rows = pl.ds(r0, CHUNK)
                chunk = r0 // CHUNK
                inter = _dot_nt(qd_s[d, rows, :], states[d].astype(BF16))
                dec = dec_s[d, pl.ds(pl.multiple_of(chunk * 8, 8), 1), :]
                y_s[d, rows, :] = y_s[d, rows, :] + inter
                states[d] = states[d] * dec + u_s[d, pl.ds(pl.multiple_of(chunk * HEAD, HEAD), HEAD), :]
        st_s[0] = states[0]
        st_s[1] = states[1]
        return carry

    lax.fori_loop(0, n_chunks // steps, phase_b, 0)

    def readout(gate_ref, o_ref, off, n):
        par = _largest_divisor(n // CHUNK, (4, 2, 1))

        def body(c, carry):
            for u in range(par):
                ci = c * par + u
                rows = pl.ds(pl.multiple_of(ci * CHUNK, CHUNK), CHUNK)
                srows = pl.ds(pl.multiple_of(off + ci * CHUNK, CHUNK), CHUNK)
                y = y_s[0, srows, :] + y_s[1, srows, :]
                ms = jnp.mean(y * y, axis=-1, keepdims=True)
                o_ref[0, rows, :] = (y * lax.rsqrt(ms + EPS) * _silu(gate_ref[0, rows, :])).astype(o_ref.dtype)
            return carry
        lax.fori_loop(0, n // CHUNK // par, body, 0)

    readout(gc_ref, oc_ref, 0, n_ctx)
    readout(gl_ref, ol_ref, n_ctx, n_lat)


def _hgrn2(p_ctx, p_lat, lb_logits, consts, layer):
    _, bsz, n_ctx, _ = p_ctx.shape
    n_lat = p_lat.shape[2]
    t = n_ctx + n_lat
    lmat, mk = consts
    depth = lb_logits.shape[0]
    lbl = lb_logits.reshape(depth, 2, A_HEADS, 1, HEAD).transpose(0, 1, 3, 2, 4).reshape(depth, 2, 1, A_HEADS * HEAD)

    def col(n, c0):
        return pl.BlockSpec((None, 1, n, HEAD), lambda b, h, c0=c0: (c0 + h, b, 0, 0))

    cols = (COL_A_Q, COL_A_FF, COL_A_FB, COL_A_I, COL_A_GATE)
    return pl.pallas_call(
        functools.partial(_hgrn2_kernel, layer=layer),
        grid=(bsz, A_HEADS),
        in_specs=[
            pl.BlockSpec((depth, 2, 1, HEAD), lambda b, h: (0, 0, 0, h)),
            pl.BlockSpec(lmat.shape, lambda b, h: (0, 0, 0)),
            pl.BlockSpec(mk.shape, lambda b, h: (0, 0, 0, 0)),
        ] + [col(n_ctx, c) for c in cols] + [col(n_lat, c) for c in cols],
        out_specs=[
            pl.BlockSpec((1, n_ctx, HEAD), lambda b, h: (b, 0, h)),
            pl.BlockSpec((1, n_lat, HEAD), lambda b, h: (b, 0, h)),
        ],
        out_shape=[
            jax.ShapeDtypeStruct((bsz, n_ctx, A_HEADS * HEAD), BF16),
            jax.ShapeDtypeStruct((bsz, n_lat, A_HEADS * HEAD), BF16),
        ],
        scratch_shapes=[
            pltpu.VMEM((2, t, HEAD), BF16),
            pltpu.VMEM((2, t, HEAD), F32),
            pltpu.VMEM((2, (t // CHUNK) * HEAD, HEAD), F32),
            pltpu.VMEM((2, (t // CHUNK) * 8, HEAD), F32),
            pltpu.VMEM((2, HEAD, HEAD), F32),
        ],
        compiler_params=_params(("arbitrary", "arbitrary")),
        name="hgrn2",
    )(lbl, lmat, mk, *([p_ctx] * 5), *([p_lat] * 5))


def _mlstm_constants():
    n = CHUNK
    a = np.arange(n)[:, None]
    b = np.arange(n)[None, :]
    lt, rhs, causal = [], [], []
    for reverse in (False, True):
        upto = (b >= a) if reverse else (b <= a)
        after = (b < a) if reverse else (b > a)
        lt.append(np.stack([upto, after]))
        m = after.T
        ext = np.concatenate([m, np.zeros((n, HEAD - n)), np.ones((n, HEAD))], axis=1)
        rhs.append(np.concatenate([ext, ext], axis=0))
        causal.append(upto)
    return (jnp.asarray(np.stack(lt), F32), jnp.asarray(np.stack(rhs), BF16),
            jnp.asarray(np.stack(causal), F32))


def _mlstm_kernel(lt_ref, rhs_ref, causal_ref, eye_ref, cw_ref, bias_ref,
                  qc_ref, kc_ref, vc_ref, oc_ref, zc_ref, gc_ref,
                  ql_ref, kl_ref, vl_ref, ol_ref, zl_ref, gl_ref,
                  outc_ref, outl_ref,
                  q_s, s_s, cols_s, u_s, sc_s, st_s):
    h = pl.program_id(1)
    n_ctx = qc_ref.shape[1]
    n_lat = ql_ref.shape[1]
    n_ctx_chunks = n_ctx // CHUNK
    n_chunks = (n_ctx + n_lat) // CHUNK
    row_id = lax.broadcasted_iota(jnp.int32, (CHUNK, 1), 0)
    lane = lax.broadcasted_iota(jnp.int32, (1, HEAD), 1)
    sub8 = lax.broadcasted_iota(jnp.int32, (8, 1), 0)
    gate_row = lax.broadcasted_iota(jnp.int32, (N_GATES, 1), 0)
    eye_c = (row_id == lax.broadcasted_iota(jnp.int32, (1, CHUNK), 1)).astype(F32)
    ones_b = jnp.ones((CHUNK, HEAD), BF16)

    def conv_silu(src, ci, n, w):
        s = pl.multiple_of(ci * CHUNK, CHUNK)
        cur = src[0, pl.ds(s, CHUNK), :]
        up = src[0, pl.ds(pl.multiple_of(jnp.maximum(s - 8, 0), 8), 8), :]
        dn = src[0, pl.ds(pl.multiple_of(jnp.minimum(s + CHUNK, n - 8), 8), 8), :]
        prev_row = jnp.where(s > 0, up[7:8, :], 0.0)
        next_row = jnp.where(s + CHUNK < n, dn[0:1, :], 0.0)
        prev = jnp.where(row_id == 0, prev_row, pltpu.roll(cur, 1, 0))
        nxt = jnp.where(row_id == CHUNK - 1, next_row, pltpu.roll(cur, CHUNK - 1, 0))
        return _silu(w[0:1, :] * prev + w[1:2, :] * cur + w[2:3, :] * nxt)

    def pick_row(a, idx):
        return jnp.sum(jnp.where(gate_row == idx, a, 0.0), axis=0, keepdims=True)

    def phase_a(q_ref, k_ref, v_ref, g_ref, off, n):
        par = _largest_divisor(n // CHUNK, (8, 4, 2, 1))

        def body(c, carry):
            chunks = []
            for u in range(par):
                ci = c * par + u
                rows = pl.ds(pl.multiple_of(ci * CHUNK, CHUNK), CHUNK)
                srows = pl.ds(pl.multiple_of(off + ci * CHUNK, CHUNK), CHUNK)
                gt = _dot3_nt(eye_ref[...], g_ref[0, rows, :]) + bias_ref[:, 0:CHUNK]
                qb = (conv_silu(q_ref, ci, n, cw_ref[0]) * (HEAD ** -0.5)).astype(BF16)
                k = conv_silu(k_ref, ci, n, cw_ref[1])
                q_s[srows, :] = qb
                vaug = jnp.concatenate([v_ref[0, rows, :].astype(BF16), ones_b], axis=1)
                chunks.append(dict(k=k, qk=_dot_nt(qb, k.astype(BF16)), vaug=vaug, gt=gt, srows=srows,
                                   chunk=off // CHUNK + ci))
            chains = []
            for ck in chunks:
                for d in range(2):
                    gi = 2 * B_HEADS * d + h
                    li_row = pick_row(ck["gt"], gi) * LOG2E
                    lf_row = _log_sigmoid(pick_row(ck["gt"], gi + B_HEADS)) * LOG2E
                    lhs = jnp.concatenate([lt_ref[d, 0] * lf_row,
                                           lt_ref[d, 1] * lf_row + eye_c * li_row], axis=0)
                    out = _dot(jnp.concatenate(_split2(lhs), axis=1), rhs_ref[d])
                    chains.append(dict(ck, d=d, li_row=li_row, out=out))
            for ch in chains:
                d, out = ch["d"], ch["out"]
                cum_rep = out[0:CHUNK, HEAD:]
                logw_rep = out[CHUNK:, HEAD:]
                log_d = jnp.where(causal_ref[d] > 0.0, out[0:CHUNK, 0:CHUNK] + ch["li_row"], NEG)
                a_loc = jnp.max(log_d, axis=1, keepdims=True)
                s_s[d, ch["srows"], :] = (ch["qk"] * jnp.exp2(log_d - a_loc)).astype(BF16)
                cols_s[d, ch["srows"], :] = jnp.where(lane < CHUNK, a_loc, cum_rep)
                total = cum_rep[CHUNK - 1:CHUNK, :] if d == 0 else cum_rep[0:1, :]
                m_loc = jnp.max(logw_rep, axis=0, keepdims=True)
                kw = (ch["k"] * jnp.exp2(logw_rep - m_loc)).astype(BF16)
                ch["u"] = _dot_tn(kw, ch["vaug"])
                sc_s[d, pl.ds(pl.multiple_of(ch["chunk"] * 8, 8), 8), :] = jnp.where(
                    sub8 == 0, total, jnp.where(sub8 == 1, m_loc, 0.0))
            for ch in chains:
                u_s[ch["d"], pl.ds(pl.multiple_of(ch["chunk"] * HEAD, HEAD), HEAD), :] = ch["u"].astype(BF16)
            return carry
        lax.fori_loop(0, n // CHUNK // par, body, 0)

    phase_a(qc_ref, kc_ref, vc_ref, gc_ref, 0, n_ctx)
    phase_a(ql_ref, kl_ref, vl_ref, gl_ref, n_ctx, n_lat)

    st_s[...] = jnp.zeros_like(st_s)
    steps = _largest_divisor(n_chunks, (4, 2, 1))
    for d in range(2):
        def phase_b(jj, m, d=d):
            for u in range(steps):
                chunk = _scan_row(jj * steps + u, d == 1, n_ctx_chunks, n_chunks) // CHUNK
                srow = pl.ds(pl.multiple_of(chunk * 8, 8), 8)
                urow = pl.ds(pl.multiple_of(chunk * HEAD, HEAD), HEAD)
                sc = sc_s[d, srow, :]
                total, m_loc = sc[0:1, :], sc[1:2, :]
                m_new = jnp.maximum(total + m, m_loc)
                state = st_s[d]
                inc = u_s[d, urow, :].astype(F32)
                u_s[d, urow, :] = state.astype(BF16)
                sc_s[d, srow, :] = jnp.where(sub8 == 2, m, sc)
                keep = jnp.exp2(total + m - m_new)
                gain = jnp.exp2(m_loc - m_new)
                st_s[d] = (jnp.concatenate([keep, keep], axis=1) * state
                           + jnp.concatenate([gain, gain], axis=1) * inc)
                m = m_new
            return m
        lax.fori_loop(0, n_chunks // steps, phase_b, jnp.zeros((1, HEAD), F32))

    def phase_c(v_ref, o_ref, z_ref, out_ref, off, n):
        par = _largest_divisor(n // CHUNK, (8, 4, 2, 1))

        def body(c, carry):
            chains = []
            for u in range(par):
                ci = c * par + u
                rows = pl.ds(pl.multiple_of(ci * CHUNK, CHUNK), CHUNK)
                srows = pl.ds(pl.multiple_of(off + ci * CHUNK, CHUNK), CHUNK)
                chunk = off // CHUNK + ci
                qb = q_s[srows, :]
                vaug = jnp.concatenate([v_ref[0, rows, :].astype(BF16), ones_b], axis=1)
                for d in range(2):
                    intra = _dot(s_s[d, srows, :], vaug)
                    inter = _dot(qb, u_s[d, pl.ds(pl.multiple_of(chunk * HEAD, HEAD), HEAD), :])
                    chains.append(dict(d=d, u=u, rows=rows, srows=srows, chunk=chunk, intra=intra, inter=inter))
            ys = [None] * par
            for ch in chains:
                d = ch["d"]
                cols = cols_s[d, ch["srows"], :]
                swapped = pltpu.roll(cols, CHUNK, 1)
                a_rep = jnp.where(lane < CHUNK, cols, swapped)
                cum_rep = jnp.where(lane < CHUNK, swapped, cols)
                m = sc_s[d, pl.ds(pl.multiple_of(ch["chunk"] * 8, 8), 8), :][2:3, :]
                inter_l = cum_rep + m
                m_t = jnp.maximum(a_rep, inter_l)
                wi = jnp.exp2(a_rep - m_t)
                we = jnp.exp2(inter_l - m_t)
                num = wi * ch["intra"][:, :HEAD] + we * ch["inter"][:, :HEAD]
                den = wi * ch["intra"][:, HEAD:] + we * ch["inter"][:, HEAD:]
                hd = num / jnp.maximum(jnp.abs(den), jnp.exp2(-m_t))
                ys[ch["u"]] = hd if ys[ch["u"]] is None else ys[ch["u"]] + hd
                if d == 1:
                    rows = ch["rows"]
                    hh = jax.nn.sigmoid(o_ref[0, rows, :]) * ys[ch["u"]]
                    ms = jnp.mean(hh * hh, axis=-1, keepdims=True)
                    out_ref[0, rows, :] = (hh * lax.rsqrt(ms + EPS) * _silu(z_ref[0, rows, :])).astype(out_ref.dtype)
            return carry
        lax.fori_loop(0, n // CHUNK // par, body, 0)

    phase_c(vc_ref, oc_ref, zc_ref, outc_ref, 0, n_ctx)
    phase_c(vl_ref, ol_ref, zl_ref, outl_ref, n_ctx, n_lat)


def _mlstm(p_ctx, p_lat, g_ctx, g_lat, conv_w, gate_bias, consts):
    _, bsz, n_ctx, _ = p_ctx.shape
    n_lat = p_lat.shape[2]
    t = n_ctx + n_lat
    n_chunks = t // CHUNK
    lt, rhs, causal = consts
    eye = jnp.eye(N_GATES, HEAD, dtype=BF16)
    cw = conv_w.reshape(3, 2, B_HEADS * HEAD).transpose(1, 0, 2)
    bias = jnp.broadcast_to(gate_bias.reshape(N_GATES, 1), (N_GATES, HEAD))

    def col(n, c0):
        return pl.BlockSpec((None, 1, n, HEAD), lambda b, h, c0=c0: (c0 + h, b, 0, 0))

    def gate_spec(n):
        return pl.BlockSpec((1, n, HEAD), lambda b, h: (b, 0, 0))

    cols = (COL_B_Q, COL_B_K, COL_B_V, COL_B_O, COL_B_Z)
    return pl.pallas_call(
        _mlstm_kernel,
        grid=(bsz, B_HEADS),
        in_specs=[
            pl.BlockSpec(lt.shape, lambda b, h: (0, 0, 0, 0)),
            pl.BlockSpec(rhs.shape, lambda b, h: (0, 0, 0)),
            pl.BlockSpec(causal.shape, lambda b, h: (0, 0, 0)),
            pl.BlockSpec(eye.shape, lambda b, h: (0, 0)),
            pl.BlockSpec((2, 3, HEAD), lambda b, h: (0, 0, h)),
            pl.BlockSpec((N_GATES, HEAD), lambda b, h: (0, 0)),
        ] + [col(n_ctx, c) for c in cols] + [gate_spec(n_ctx)]
          + [col(n_lat, c) for c in cols] + [gate_spec(n_lat)],
        out_specs=[
            pl.BlockSpec((1, n_ctx, HEAD), lambda b, h: (b, 0, h)),
            pl.BlockSpec((1, n_lat, HEAD), lambda b, h: (b, 0, h)),
        ],
        out_shape=[
            jax.ShapeDtypeStruct((bsz, n_ctx, B_HEADS * HEAD), BF16),
            jax.ShapeDtypeStruct((bsz, n_lat, B_HEADS * HEAD), BF16),
        ],
        scratch_shapes=[
            pltpu.VMEM((t, HEAD), BF16),
            pltpu.VMEM((2, t, CHUNK), BF16),
            pltpu.VMEM((2, t, HEAD), F32),
            pltpu.VMEM((2, n_chunks * HEAD, 2 * HEAD), BF16),
            pltpu.VMEM((2, n_chunks * 8, HEAD), F32),
            pltpu.VMEM((2, HEAD, 2 * HEAD), F32),
        ],
        compiler_params=_params(("arbitrary", "arbitrary")),
        name="mlstm",
    )(lt, rhs, causal, eye, cw, bias, *([p_ctx] * 5), g_ctx, *([p_lat] * 5), g_lat)


def _rope(x, cos, sin_lo, sin_hi):
    return x * cos + pltpu.roll(x, HEAD - 32, 1) * sin_lo + pltpu.roll(x, 32, 1) * sin_hi


def _softmax_attend(sink_col, scores, values):
    m = jnp.maximum(sink_col, jnp.max(scores, axis=1, keepdims=True))
    p = jnp.exp2(scores - m)
    den = jnp.exp2(sink_col - m) + jnp.sum(p, axis=1, keepdims=True)
    return _dot(p.astype(BF16), values) / den


def _sink_column(sink_ref, kv, rows_per_head):
    row = lax.broadcasted_iota(jnp.int32, (C_GROUP * rows_per_head, 1), 0)
    col = jnp.zeros((C_GROUP * rows_per_head, 1), F32)
    for g in range(C_GROUP):
        col = jnp.where(row // rows_per_head == g, sink_ref[kv * C_GROUP + g] * LOG2E, col)
    return col


def _window_masks():
    r = np.arange(ATTN_BLOCK)[:, None]
    c = np.arange(3 * ATTN_BLOCK)[None, :]
    return jnp.asarray(np.stack([np.abs(c - p * ATTN_BLOCK - r) <= WINDOW for p in range(3)]), F32)


def _attn_kernel(sink_ref, mask_ref, cos_ref, slo_ref, shi_ref, q_ref, k_ref, v_ref, kc_ref, vc_ref, z_ref,
                 o_ref, kr_s):
    kv = pl.program_id(1)
    i = pl.program_id(2)
    n = k_ref.shape[1]
    span = 3 * ATTN_BLOCK

    @pl.when(i == 0)
    def _():
        def body(c, carry):
            rows = pl.ds(pl.multiple_of(c * ATTN_BLOCK, ATTN_BLOCK), ATTN_BLOCK)
            kr_s[rows, :] = _rope(k_ref[0, rows, :], cos_ref[rows, :], slo_ref[rows, :],
                                  shi_ref[rows, :]).astype(BF16)
            return carry
        lax.fori_loop(0, n // ATTN_BLOCK, body, 0)

    scale = HEAD ** -0.5 * LOG2E
    blocks_per_step = q_ref.shape[1] // ATTN_BLOCK

    def one_block(sub, carry):
        blk = i * blocks_per_step + sub
        local = pl.ds(pl.multiple_of(sub * ATTN_BLOCK, ATTN_BLOCK), ATTN_BLOCK)
        qrows = pl.ds(pl.multiple_of(blk * ATTN_BLOCK, ATTN_BLOCK), ATTN_BLOCK)
        cos = cos_ref[qrows, :]
        slo = slo_ref[qrows, :]
        shi = shi_ref[qrows, :]
        start = pl.multiple_of(jnp.clip((blk - 1) * ATTN_BLOCK, 0, n - span), ATTN_BLOCK)
        keys = jnp.concatenate([kr_s[pl.ds(start, span), :], kc_ref[0].astype(BF16)], axis=0)
        values = jnp.concatenate([v_ref[0, pl.ds(start, span), :].astype(BF16), vc_ref[0].astype(BF16)],
                                 axis=0)
        valid = mask_ref[(blk * ATTN_BLOCK - start) // ATTN_BLOCK] > 0.0
        scores = []
        for g in range(C_GROUP):
            qg = (_rope(q_ref[g, local, :], cos, slo, shi) * scale).astype(BF16)
            s = _dot_nt(qg, keys)
            scores.append(jnp.concatenate([jnp.where(valid, s[:, :span], NEG), s[:, span:]], axis=1))
        for g in range(C_GROUP):
            sink = jnp.full((ATTN_BLOCK, 1), sink_ref[kv * C_GROUP + g] * LOG2E, F32)
            out = _softmax_attend(sink, scores[g], values)
            o_ref[0, local, g * HEAD:(g + 1) * HEAD] = (out * _silu(z_ref[g, local, :])).astype(o_ref.dtype)
        return carry

    lax.fori_loop(0, blocks_per_step, one_block, 0)


def _attention(p_ctx, p_lat, sink, rope):
    _, bsz, n_ctx, _ = p_ctx.shape
    n = p_lat.shape[2]
    cos, slo, shi = rope
    masks = _window_masks()
    qrows = ATTN_BLOCK * _largest_divisor(n // ATTN_BLOCK, (8, 4, 2, 1))
    gw = C_GROUP * HEAD
    gblk = C_GROUP

    full = pl.BlockSpec((n, HEAD), lambda b, kv, i: (0, 0))
    return pl.pallas_call(
        _attn_kernel,
        grid=(bsz, C_KV_HEADS, n // qrows),
        in_specs=[
            pl.BlockSpec(memory_space=pltpu.SMEM),
            pl.BlockSpec(masks.shape, lambda b, kv, i: (0, 0, 0)),
            full, full, full,
            pl.BlockSpec((C_GROUP, None, qrows, HEAD), lambda b, kv, i: (COL_C_Q // gblk + kv, b, i, 0)),
            pl.BlockSpec((None, 1, n, HEAD), lambda b, kv, i: (COL_C_K + kv, b, 0, 0)),
            pl.BlockSpec((None, 1, n, HEAD), lambda b, kv, i: (COL_C_V + kv, b, 0, 0)),
            pl.BlockSpec((None, 1, n_ctx, HEAD), lambda b, kv, i: (COL_C_K + kv, b, 0, 0)),
            pl.BlockSpec((None, 1, n_ctx, HEAD), lambda b, kv, i: (COL_C_V + kv, b, 0, 0)),
            pl.BlockSpec((C_GROUP, None, qrows, HEAD), lambda b, kv, i: (COL_C_Z // gblk + kv, b, i, 0)),
        ],
        out_specs=pl.BlockSpec((1, qrows, gw), lambda b, kv, i: (b, i, kv)),
        out_shape=jax.ShapeDtypeStruct((bsz, n, C_Q_HEADS * HEAD), BF16),
        scratch_shapes=[pltpu.VMEM((n, HEAD), BF16)],
        compiler_params=_params(("arbitrary", "arbitrary", "arbitrary")),
        name="window_attention",
    )(sink, masks, cos, slo, shi, p_lat, p_lat, p_lat, p_ctx, p_ctx, p_lat)


def _ctx_attn_kernel(sink_ref, q_ref, k_ref, v_ref, z_ref, o_ref):
    kv = pl.program_id(1)
    n_ctx = q_ref.shape[1]
    scale = HEAD ** -0.5 * LOG2E
    q4 = jnp.concatenate([q_ref[g] * scale for g in range(C_GROUP)],
                         axis=0).astype(BF16)
    s = _dot_nt(q4, k_ref[0].astype(BF16))
    out = _softmax_attend(_sink_column(sink_ref, kv, n_ctx), s, v_ref[0].astype(BF16))
    for g in range(C_GROUP):
        zg = z_ref[g]
        o_ref[0, :, g * HEAD:(g + 1) * HEAD] = (out[g * n_ctx:(g + 1) * n_ctx] * _silu(zg)).astype(o_ref.dtype)


def _context_attention(p_ctx, sink):
    _, bsz, n_ctx, _ = p_ctx.shape
    gw = C_GROUP * HEAD
    gblk = C_GROUP
    return pl.pallas_call(
        _ctx_attn_kernel,
        grid=(bsz, C_KV_HEADS),
        in_specs=[
            pl.BlockSpec(memory_space=pltpu.SMEM),
            pl.BlockSpec((C_GROUP, None, n_ctx, HEAD), lambda b, kv: (COL_C_Q // gblk + kv, b, 0, 0)),
            pl.BlockSpec((None, 1, n_ctx, HEAD), lambda b, kv: (COL_C_K + kv, b, 0, 0)),
            pl.BlockSpec((None, 1, n_ctx, HEAD), lambda b, kv: (COL_C_V + kv, b, 0, 0)),
            pl.BlockSpec((C_GROUP, None, n_ctx, HEAD), lambda b, kv: (COL_C_Z // gblk + kv, b, 0, 0)),
        ],
        out_specs=pl.BlockSpec((1, n_ctx, gw), lambda b, kv: (b, 0, kv)),
        out_shape=jax.ShapeDtypeStruct((bsz, n_ctx, C_Q_HEADS * HEAD), BF16),
        compiler_params=_params(("arbitrary", "arbitrary")),
        name="context_attention",
    )(sink, p_ctx, p_ctx, p_ctx, p_ctx)


def _rope_tables(n):
    axis_dim = HEAD // 2
    rows = n // GRID_W
    row = jnp.repeat(jnp.arange(rows, dtype=F32), GRID_W)
    col = jnp.tile(jnp.arange(GRID_W, dtype=F32), rows)
    inv_freq = ROPE_BASE ** (-jnp.arange(0, axis_dim, 2, dtype=F32) / axis_dim)
    ang_r = row[:, None] * inv_freq[None, :]
    ang_c = col[:, None] * inv_freq[None, :]
    zero = jnp.zeros_like(ang_r)
    cos = jnp.concatenate([jnp.cos(ang_r)] * 2 + [jnp.cos(ang_c)] * 2, axis=-1)
    sin_lo = jnp.concatenate([-jnp.sin(ang_r), zero, -jnp.sin(ang_c), zero], axis=-1)
    sin_hi = jnp.concatenate([zero, jnp.sin(ang_r), zero, jnp.sin(ang_c)], axis=-1)
    return cos, sin_lo, sin_hi


def _pick_tile(n, candidates):
    for c in candidates:
        if n % c == 0:
            return c
    raise ValueError(f"no tile for {n}")


def kernel(x, c, ctx, c_ctx, w_mod, b_mod, g_pre, g_post, w_in, hgrn_lb_logits, mlstm_conv_w,
           mlstm_gate_bias, attn_sink, w_out):
    bsz, n_lat, d = x.shape
    n_ctx = ctx.shape[1]
    depth = w_mod.shape[0]
    assert n_ctx % CHUNK == 0 and n_lat % ATTN_BLOCK == 0 and n_lat >= 3 * ATTN_BLOCK

    cond = jnp.concatenate([c, c_ctx[None, :]], axis=0)
    pad = (-cond.shape[0]) % 8
    cond = jnp.pad(cond, ((0, pad), (0, 0)))
    mod = _modulation(cond, w_mod, b_mod)

    w_t = _weight_prep(jnp.swapaxes(w_in, 1, 2))
    wo = w_out.astype(BF16)
    consts = _scan_constants()
    mlstm_consts = _mlstm_constants()
    rope = _rope_tables(n_lat)

    tm_lat = _pick_tile(n_lat, (1024, 512, 256, 128))
    tm_ctx = _pick_tile(bsz * n_ctx, (1024, 512, 256, 128))
    tn = _pick_tile(MAIN_WIDTH, (1536, 512))
    to_lat = _pick_tile(n_lat, (512, 256, 128))
    to_ctx = _pick_tile(bsz * n_ctx, (512, 256, 128))

    x2d = x.reshape(bsz * n_lat, d)
    h2d = ctx.reshape(bsz * n_ctx, d)
    for layer in range(depth):
        with_ctx = layer < depth - 1
        shift_l = mod[layer, :bsz, 0:d].reshape(bsz, 1, d)
        scale_l = mod[layer, :bsz, d:2 * d].reshape(bsz, 1, d)
        gate_l = mod[layer, :bsz, 2 * d:].reshape(bsz, 1, d)
        shift_c = mod[layer, bsz:bsz + 1, 0:d].reshape(1, 1, d)
        scale_c = mod[layer, bsz:bsz + 1, d:2 * d].reshape(1, 1, d)
        gate_c = mod[layer, bsz:bsz + 1, 2 * d:].reshape(1, 1, d)

        gain_pre = g_pre[layer].reshape(1, d)
        gain_post = g_post[layer].reshape(1, d)

        p_lat, g_lat = _in_projection(x2d, shift_l, scale_l, gain_pre, w_t, layer, n_lat, tm_lat, tn)
        p_ctx, g_ctx = _in_projection(h2d, shift_c, scale_c, gain_pre, w_t, layer, bsz * n_ctx, tm_ctx, tn)
        p_lat = p_lat.reshape(MAIN_BLOCKS, bsz, n_lat, HEAD)
        p_ctx = p_ctx.reshape(MAIN_BLOCKS, bsz, n_ctx, HEAD)
        g_lat = g_lat.reshape(bsz, n_lat, HEAD)
        g_ctx = g_ctx.reshape(bsz, n_ctx, HEAD)

        a_c, a_l = _hgrn2(p_ctx, p_lat, hgrn_lb_logits, consts, layer)
        b_c, b_l = _mlstm(p_ctx, p_lat, g_ctx, g_lat, mlstm_conv_w[layer], mlstm_gate_bias[layer],
                          mlstm_consts)
        c_l = _attention(p_ctx, p_lat, attn_sink[layer], rope)

        x2d = _out_projection(a_l.reshape(bsz * n_lat, -1), b_l.reshape(bsz * n_lat, -1),
                              c_l.reshape(bsz * n_lat, -1), wo, layer, x2d, gate_l, gain_post, n_lat, to_lat)
        if with_ctx:
            c_c = _context_attention(p_ctx, attn_sink[layer])
            h2d = _out_projection(a_c.reshape(bsz * n_ctx, -1), b_c.reshape(bsz * n_ctx, -1),
                                  c_c.reshape(bsz * n_ctx, -1), wo, layer, h2d, gate_c, gain_post,
                                  bsz * n_ctx, to_ctx)
    return x2d.reshape(bsz, n_lat, d)
```

```python
import functools

import numpy as np
import jax
import jax.numpy as jnp
from jax import lax
from jax.experimental import pallas as pl
from jax.experimental.pallas import tpu as pltpu

F32 = jnp.float32
BF16 = jnp.bfloat16

EPS = 1e-6
NEG = -1e30
LOG2E = 1.4426950408889634
HEAD = 128
A_HEADS = 4
B_HEADS = 4
C_Q_HEADS = 8
C_KV_HEADS = 2
C_GROUP = C_Q_HEADS // C_KV_HEADS
CHUNK = 64
ATTN_BLOCK = 128
WINDOW = 128
GRID_W = 64
ROPE_BASE = 10000.0
N_GATES = 4 * B_HEADS

COL_A_Q, COL_A_FF, COL_A_FB, COL_A_I, COL_A_GATE = 0, 4, 8, 12, 16
COL_B_Q, COL_B_K, COL_B_V, COL_B_O, COL_B_Z = 20, 24, 28, 32, 36
COL_C_Q, COL_C_K, COL_C_V, COL_C_Z = 40, 48, 50, 52
MAIN_BLOCKS = 60
MAIN_WIDTH = MAIN_BLOCKS * HEAD
GATES_OFFSET = 9 * 4 * HEAD

VMEM_LIMIT = 56 * 1024 * 1024


def _params(sem):
    return pltpu.CompilerParams(dimension_semantics=sem, vmem_limit_bytes=VMEM_LIMIT)


def _silu(x):
    return x * jax.nn.sigmoid(x)


def _log_sigmoid(x):
    return jnp.minimum(x, 0.0) - jnp.log(1.0 + jnp.exp(-jnp.abs(x)))


def _dot(a, b):
    return jnp.dot(a, b, preferred_element_type=F32)


def _dot_nt(a, b):
    return lax.dot_general(a, b, (((1,), (1,)), ((), ())), preferred_element_type=F32)


def _dot_tn(a, b):
    return lax.dot_general(a, b, (((0,), (0,)), ((), ())), preferred_element_type=F32)


def _split3(x):
    hi = x.astype(BF16)
    r = x - hi.astype(F32)
    mid = r.astype(BF16)
    lo = (r - mid.astype(F32)).astype(BF16)
    return hi, mid, lo


def _split2(x):
    hi = x.astype(BF16)
    return hi, (x - hi.astype(F32)).astype(BF16)


def _dot3_nt(w, x):
    hi, mid, lo = _split3(x)
    return _dot_nt(w, hi) + _dot_nt(w, mid) + _dot_nt(w, lo)


def _mod_kernel(c_ref, w_ref, b_ref, o_ref):
    a = _silu(c_ref[...])
    o_ref[0] = jnp.dot(a, w_ref[0], preferred_element_type=F32,
                       precision=lax.Precision.HIGHEST) + b_ref[0]


def _modulation(cond, w_mod, b_mod):
    depth, d, n = w_mod.shape
    rows = cond.shape[0]
    tn = _pick_tile(n, (1536, 768, 384, 128))
    assert n % tn == 0
    return pl.pallas_call(
        _mod_kernel,
        grid=(depth, n // tn),
        in_specs=[
            pl.BlockSpec((rows, d), lambda l, j: (0, 0)),
            pl.BlockSpec((1, d, tn), lambda l, j: (l, 0, j)),
            pl.BlockSpec((1, 1, tn), lambda l, j: (l, 0, j)),
        ],
        out_specs=pl.BlockSpec((1, rows, tn), lambda l, j: (l, 0, j)),
        out_shape=jax.ShapeDtypeStruct((depth, rows, n), F32),
        compiler_params=_params(("arbitrary", "arbitrary")),
        name="modulation",
    )(cond, w_mod, b_mod.reshape(depth, 1, n))


NORM_ROWS = 32
EPILOGUE_ROWS = 8


def _inproj_kernel(x_ref, shift_ref, scale_ref, g_ref, w_ref, wg_ref, p_ref, gates_ref, xn_ref, gm_ref):
    tm = x_ref.shape[0]

    @pl.when(pl.program_id(1) == 0)
    def _():
        gm_ref[...] = g_ref[...] * (1.0 + scale_ref[0])

        def body(r, carry):
            for u in range(2):
                rows = pl.ds(pl.multiple_of((2 * r + u) * NORM_ROWS, NORM_ROWS), NORM_ROWS)
                xr = x_ref[rows, :]
                ms = jnp.mean(xr * xr, axis=-1, keepdims=True)
                xn_ref[rows, :] = (xr * (lax.rsqrt(ms + EPS) * gm_ref[...]) + shift_ref[0]).astype(BF16)
            return carry

        lax.fori_loop(0, tm // NORM_ROWS // 2, body, 0)
        gates_ref[...] = _dot_nt(xn_ref[...], wg_ref[...])

    res = _dot_nt(xn_ref[...], w_ref[...])
    for t in range(p_ref.shape[0]):
        p_ref[t] = res[:, t * HEAD:(t + 1) * HEAD]


PREP_ROWS = 512
MAIN_PREP_BLOCKS = MAIN_WIDTH // PREP_ROWS
assert GATES_OFFSET % PREP_ROWS == 0 and MAIN_WIDTH % PREP_ROWS == 0


def _weight_prep_kernel(w_ref, o_ref):
    c = pl.program_id(1)
    row = lax.broadcasted_iota(jnp.int32, (PREP_ROWS, 1), 0)
    keep = jnp.logical_or(c < MAIN_PREP_BLOCKS, row < N_GATES)
    o_ref[...] = jnp.where(keep, w_ref[0], 0.0).astype(BF16)


def _weight_prep(w_in_t):
    depth, n, d = w_in_t.shape

    def src_row(c):
        after = jnp.where(c * PREP_ROWS >= GATES_OFFSET, N_GATES, 0)
        return pl.multiple_of(jnp.where(c < MAIN_PREP_BLOCKS, c * PREP_ROWS + after, GATES_OFFSET), N_GATES)

    return pl.pallas_call(
        _weight_prep_kernel,
        grid=(depth, MAIN_PREP_BLOCKS + 1),
        in_specs=[pl.BlockSpec((pl.Element(1), pl.Element(PREP_ROWS), pl.Element(d)),
                               lambda l, c: (l, src_row(c), 0))],
        out_specs=pl.BlockSpec((None, PREP_ROWS, d), lambda l, c: (l, c, 0)),
        out_shape=jax.ShapeDtypeStruct((depth, MAIN_WIDTH + PREP_ROWS, d), BF16),
        compiler_params=_params(("arbitrary", "arbitrary")),
        name="weight_prep",
    )(w_in_t)


def _in_projection(x2d, shift, scale, gain, w_t, layer, rows_per_mod, tm, tn):
    r, d = x2d.shape
    n = MAIN_WIDTH
    assert r % tm == 0 and n % tn == 0 and rows_per_mod % tm == 0
    per = rows_per_mod // tm
    return pl.pallas_call(
        _inproj_kernel,
        grid=(r // tm, n // tn),
        in_specs=[
            pl.BlockSpec((tm, d), lambda i, j: (i, 0)),
            pl.BlockSpec((1, 1, d), lambda i, j: (i // per, 0, 0)),
            pl.BlockSpec((1, 1, d), lambda i, j: (i // per, 0, 0)),
            pl.BlockSpec((1, d), lambda i, j: (0, 0)),
            pl.BlockSpec((None, tn, d), lambda i, j: (layer, j, 0)),
            pl.BlockSpec((None, HEAD, d), lambda i, j: (layer, MAIN_BLOCKS, 0)),
        ],
        out_specs=[
            pl.BlockSpec((tn // HEAD, tm, HEAD), lambda i, j: (j, i, 0)),
            pl.BlockSpec((tm, HEAD), lambda i, j: (i, 0)),
        ],
        out_shape=[
            jax.ShapeDtypeStruct((n // HEAD, r, HEAD), F32),
            jax.ShapeDtypeStruct((r, HEAD), F32),
        ],
        scratch_shapes=[pltpu.VMEM((tm, d), BF16), pltpu.VMEM((1, d), F32)],
        compiler_params=_params(("arbitrary", "arbitrary")),
        name="in_projection",
    )(x2d, shift, scale, gain, w_t, w_t)


def _outproj_kernel(a_ref, b_ref, c_ref, w_ref, x_ref, gate_ref, g_ref, o_ref, y_s):
    i = pl.program_id(0)
    tm = x_ref.shape[0]

    @pl.when(i == 0)
    def _():
        y_s[1] = jnp.zeros(y_s.shape[1:], F32)

    def step(cur, prev):
        for r in range(tm // EPILOGUE_ROWS):
            rows = slice(r * EPILOGUE_ROWS, (r + 1) * EPILOGUE_ROWS)
            y = y_s[prev, rows, :]
            ms = jnp.mean(y * y, axis=-1, keepdims=True)
            o_ref[rows, :] = x_ref[rows, :] + y * (lax.rsqrt(ms + EPS) * (gate_ref[0] * g_ref[...]))
        y_s[cur] = _dot(jnp.concatenate([a_ref[...], b_ref[...], c_ref[...]], axis=1), w_ref[...])

    pl.when(i % 2 == 0)(lambda: step(0, 1))
    pl.when(i % 2 == 1)(lambda: step(1, 0))


def _out_projection(a, b, c, w_out, layer, x2d, gate, gain, rows_per_mod, tm):
    r, d = x2d.shape
    assert r % tm == 0 and rows_per_mod % tm == 0
    per = rows_per_mod // tm
    n = r // tm
    cur = lambda i: (jnp.minimum(i, n - 1), 0)
    prev = lambda i: (jnp.maximum(i - 1, 0), 0)
    return pl.pallas_call(
        _outproj_kernel,
        grid=(n + 1,),
        in_specs=[
            pl.BlockSpec((tm, a.shape[1]), cur),
            pl.BlockSpec((tm, b.shape[1]), cur),
            pl.BlockSpec((tm, c.shape[1]), cur),
            pl.BlockSpec((None,) + w_out.shape[1:], lambda i: (layer, 0, 0)),
            pl.BlockSpec((tm, d), prev),
            pl.BlockSpec((1, 1, d), lambda i: (jnp.maximum(i - 1, 0) // per, 0, 0)),
            pl.BlockSpec((1, d), lambda i: (0, 0)),
        ],
        out_specs=pl.BlockSpec((tm, d), prev),
        out_shape=jax.ShapeDtypeStruct((r, d), F32),
        scratch_shapes=[pltpu.VMEM((2, tm, d), F32)],
        compiler_params=_params(("arbitrary",)),
        name="out_projection",
    )(a, b, c, w_out, x2d, gate, gain)


N_LEVELS = 7


def _scan_matrices(reverse):
    n = CHUNK
    t = np.arange(n)[:, None]
    u = np.arange(n)[None, :]
    cum = (u >= t) if reverse else (u <= t)
    masks = [t == u]
    size = 2
    while size <= n:
        half = size // 2
        same = (t // size) == (u // size)
        if reverse:
            masks.append(same & ((t % size) < half) & ((u % size) >= half))
        else:
            masks.append(same & ((t % size) >= half) & ((u % size) < half))
        size *= 2
    return cum.astype(np.float32), np.stack(masks).astype(np.float32)


def _scan_constants():
    cf, mf = _scan_matrices(False)
    cb, mb = _scan_matrices(True)
    return (jnp.asarray(np.stack([cf, cb]), BF16), jnp.asarray(np.stack([mf, mb]), F32))


def _largest_divisor(n, candidates):
    return next(c for c in candidates if n % c == 0)


def _scan_row(j, reverse, n_ctx_chunks, n_chunks):
    if not reverse:
        return j * CHUNK
    ctx_part = (n_ctx_chunks - 1 - j) * CHUNK
    lat_part = (n_ctx_chunks + (n_chunks - 1 - j)) * CHUNK
    return jnp.where(j < n_ctx_chunks, ctx_part, lat_part)


def _hgrn2_kernel(lbl_ref, lmat_ref, mk_ref,
                  qc_ref, ffc_ref, fbc_ref, ic_ref, gc_ref,
                  ql_ref, ffl_ref, fbl_ref, il_ref, gl_ref,
                  oc_ref, ol_ref,
                  qd_s, y_s, u_s, dec_s, st_s, *, layer):
    n_ctx = qc_ref.shape[1]
    n_lat = ql_ref.shape[1]
    n_ctx_chunks = n_ctx // CHUNK
    n_chunks = (n_ctx + n_lat) // CHUNK

    logits = lbl_ref[...]
    e = jnp.exp(logits - jnp.max(logits, axis=0, keepdims=True))
    sm = e / jnp.sum(e, axis=0, keepdims=True)
    lbs = jnp.sum(sm[:layer + 1], axis=0) - sm[0]

    row = lax.broadcasted_iota(jnp.int32, (CHUNK, 1), 0)
    r4 = row % 4

    def level_factor(size, d, f, cum):
        half = size // 2
        if size == 2:
            return jnp.where((row % 2 == 1) if d == 0 else (row % 2 == 0), f, 1.0)
        if size == 4:
            f_next = pltpu.roll(f, CHUNK - 1, 0)
            f_prev = pltpu.roll(f, 1, 0)
            if d == 0:
                return jnp.where(r4 == 0, f_next, jnp.where(r4 == 1, 1.0, jnp.where(r4 == 2, f, f * f_prev)))
            return jnp.where(r4 == 0, f * f_next, jnp.where(r4 == 1, f, jnp.where(r4 == 2, 1.0, f_prev)))
        mids = [b * size + (half - 1 if d == 0 else half) for b in range(CHUNK // size)]
        ref = jnp.concatenate([jnp.broadcast_to(cum[m:m + 1, :], (size, HEAD)) for m in mids], axis=0)
        return jnp.exp2(-jnp.abs(cum - ref))

    key_lane = lax.broadcasted_iota(jnp.int32, (1, CHUNK), 1)
    fine_levels = 4

    def assemble_scores(d, dots):
        groups = []
        for g in range(CHUNK // 8):
            r = slice(8 * g, 8 * g + 8)
            acc = mk_ref[d, 0, r, :] * dots[0][r]
            for lvl in range(1, fine_levels):
                acc = acc + mk_ref[d, lvl, r, :] * dots[lvl][r]
            for lvl in range(fine_levels, N_LEVELS):
                size = 2 ** lvl
                half = size // 2
                base = (8 * g // size) * size
                on_query_side = (8 * g - base >= half) if d == 0 else (8 * g - base < half)
                if on_query_side:
                    lo = base if d == 0 else base + half
                    acc = jnp.where((key_lane >= lo) & (key_lane < lo + half), dots[lvl][r], acc)
            groups.append(acc)
        return jnp.concatenate(groups, axis=0)

    def phase_a(q_ref, f_refs, i_ref, off, n):
        par = _largest_divisor(n // CHUNK, (16, 8, 4, 2, 1))

        def body(c, carry):
            chains = []
            for u in range(par):
                ci = c * par + u
                rows = pl.ds(pl.multiple_of(ci * CHUNK, CHUNK), CHUNK)
                srows = pl.ds(pl.multiple_of(off + ci * CHUNK, CHUNK), CHUNK)
                q = _silu(q_ref[0, rows, :])
                v = i_ref[0, rows, :].astype(BF16)
                for d in range(2):
                    lb = lbs[d]
                    open_part = (1.0 - lb) * jax.nn.sigmoid(f_refs[d][0, rows, :])
                    f = lb + open_part
                    k = (1.0 - lb) - open_part
                    logf = jnp.log(f) * LOG2E
                    res = _dot(lmat_ref[d], jnp.concatenate(_split2(logf), axis=1))
                    chains.append(dict(d=d, q=q, v=v, k=k, f=f, res=res, srows=srows,
                                       chunk=off // CHUNK + ci))
            for ch in chains:
                d, q, k = ch["d"], ch["q"], ch["k"]
                cum = ch["res"][:, :HEAD] + ch["res"][:, HEAD:]
                total = cum[CHUNK - 1:CHUNK, :] if d == 0 else cum[0:1, :]
                qb, kb = q.astype(BF16), k.astype(BF16)
                dots = [_dot_nt(qb, kb)]
                for lvl in range(1, N_LEVELS):
                    xb = level_factor(2 ** lvl, d, ch["f"], cum).astype(BF16)
                    dots.append(_dot_nt(qb * xb, kb * xb))
                ch["dots"] = dots
                qd_s[d, ch["srows"], :] = (q * jnp.exp2(cum)).astype(BF16)
                ch["u"] = _dot_tn(ch["v"], (k * jnp.exp2(total - cum)).astype(BF16))
                drow = pl.multiple_of(ch["chunk"] * 8, 8)
                dec_s[d, pl.ds(drow, 8), :] = jnp.broadcast_to(jnp.exp2(total), (8, HEAD))
            for ch in chains:
                d = ch["d"]
                ch["y"] = _dot(assemble_scores(d, ch["dots"]).astype(BF16), ch["v"])
                u_s[d, pl.ds(pl.multiple_of(ch["chunk"] * HEAD, HEAD), HEAD), :] = ch["u"]
            for ch in chains:
                y_s[ch["d"], ch["srows"], :] = ch["y"]
            return carry
        lax.fori_loop(0, n // CHUNK // par, body, 0)

    phase_a(qc_ref, (ffc_ref, fbc_ref), ic_ref, 0, n_ctx)
    phase_a(ql_ref, (ffl_ref, fbl_ref), il_ref, n_ctx, n_lat)

    st_s[...] = jnp.zeros_like(st_s)

    steps = _largest_divisor(n_chunks, (17, 16, 8, 4, 2, 1))

    def phase_b(jj, carry):
        states = [st_s[0], st_s[1]]
        for u in range(steps):
            j = jj * steps + u
            for d in range(2):
                r0 = pl.multiple_of(_scan_row(j, d == 1, n_ctx_chunks, n_chunks), CHUNK)
                rows = pl.ds(r0, CHUNK)
                chunk = r0 // CHUNK
                inter = _dot_nt(qd_s[d, rows, :], states[d].astype(BF16))
                dec = dec_s[d, pl.ds(pl.multiple_of(chunk * 8, 8), 1), :]
                y_s[d, rows, :] = y_s[d, rows, :] + inter
                states[d] = states[d] * dec + u_s[d, pl.ds(pl.multiple_of(chunk * HEAD, HEAD), HEAD), :]
        st_s[0] = states[0]
        st_s[1] = states[1]
        return carry

    lax.fori_loop(0, n_chunks // steps, phase_b, 0)

    def readout(gate_ref, o_ref, off, n):
        par = _largest_divisor(n // CHUNK, (4, 2, 1))

        def body(c, carry):
            for u in range(par):
                ci = c * par + u
                rows = pl.ds(pl.multiple_of(ci * CHUNK, CHUNK), CHUNK)
                srows = pl.ds(pl.multiple_of(off + ci * CHUNK, CHUNK), CHUNK)
                y = y_s[0, srows, :] + y_s[1, srows, :]
                ms = jnp.mean(y * y, axis=-1, keepdims=True)
                o_ref[0, rows, :] = (y * lax.rsqrt(ms + EPS) * _silu(gate_ref[0, rows, :])).astype(o_ref.dtype)
            return carry
        lax.fori_loop(0, n // CHUNK // par, body, 0)

    readout(gc_ref, oc_ref, 0, n_ctx)
    readout(gl_ref, ol_ref, n_ctx, n_lat)


def _hgrn2(p_ctx, p_lat, lb_logits, consts, layer):
    _, bsz, n_ctx, _ = p_ctx.shape
    n_lat = p_lat.shape[2]
    t = n_ctx + n_lat
    lmat, mk = consts
    depth = lb_logits.shape[0]
    lbl = lb_logits.reshape(depth, 2, A_HEADS, 1, HEAD).transpose(0, 1, 3, 2, 4).reshape(depth, 2, 1, A_HEADS * HEAD)

    def col(n, c0):
        return pl.BlockSpec((None, 1, n, HEAD), lambda b, h, c0=c0: (c0 + h, b, 0, 0))

    cols = (COL_A_Q, COL_A_FF, COL_A_FB, COL_A_I, COL_A_GATE)
    return pl.pallas_call(
        functools.partial(_hgrn2_kernel, layer=layer),
        grid=(bsz, A_HEADS),
        in_specs=[
            pl.BlockSpec((depth, 2, 1, HEAD), lambda b, h: (0, 0, 0, h)),
            pl.BlockSpec(lmat.shape, lambda b, h: (0, 0, 0)),
            pl.BlockSpec(mk.shape, lambda b, h: (0, 0, 0, 0)),
        ] + [col(n_ctx, c) for c in cols] + [col(n_lat, c) for c in cols],
        out_specs=[
            pl.BlockSpec((1, n_ctx, HEAD), lambda b, h: (b, 0, h)),
            pl.BlockSpec((1, n_lat, HEAD), lambda b, h: (b, 0, h)),
        ],
        out_shape=[
            jax.ShapeDtypeStruct((bsz, n_ctx, A_HEADS * HEAD), BF16),
            jax.ShapeDtypeStruct((bsz, n_lat, A_HEADS * HEAD), BF16),
        ],
        scratch_shapes=[
            pltpu.VMEM((2, t, HEAD), BF16),
            pltpu.VMEM((2, t, HEAD), F32),
            pltpu.VMEM((2, (t // CHUNK) * HEAD, HEAD), F32),
            pltpu.VMEM((2, (t // CHUNK) * 8, HEAD), F32),
            pltpu.VMEM((2, HEAD, HEAD), F32),
        ],
        compiler_params=_params(("arbitrary", "arbitrary")),
        name="hgrn2",
    )(lbl, lmat, mk, *([p_ctx] * 5), *([p_lat] * 5))


def _mlstm_constants():
    n = CHUNK
    a = np.arange(n)[:, None]
    b = np.arange(n)[None, :]
    lt, rhs, causal = [], [], []
    for reverse in (False, True):
        upto = (b >= a) if reverse else (b <= a)
        after = (b < a) if reverse else (b > a)
        lt.append(np.stack([upto, after]))
        m = after.T
        ext = np.concatenate([m, np.zeros((n, HEAD - n)), np.ones((n, HEAD))], axis=1)
        rhs.append(np.concatenate([ext, ext], axis=0))
        causal.append(upto)
    return (jnp.asarray(np.stack(lt), F32), jnp.asarray(np.stack(rhs), BF16),
            jnp.asarray(np.stack(causal), F32))


def _mlstm_kernel(lt_ref, rhs_ref, causal_ref, eye_ref, cw_ref, bias_ref,
                  qc_ref, kc_ref, vc_ref, oc_ref, zc_ref, gc_ref,
                  ql_ref, kl_ref, vl_ref, ol_ref, zl_ref, gl_ref,
                  outc_ref, outl_ref,
                  q_s, s_s, cols_s, u_s, sc_s, st_s):
    h = pl.program_id(1)
    n_ctx = qc_ref.shape[1]
    n_lat = ql_ref.shape[1]
    n_ctx_chunks = n_ctx // CHUNK
    n_chunks = (n_ctx + n_lat) // CHUNK
    row_id = lax.broadcasted_iota(jnp.int32, (CHUNK, 1), 0)
    lane = lax.broadcasted_iota(jnp.int32, (1, HEAD), 1)
    sub8 = lax.broadcasted_iota(jnp.int32, (8, 1), 0)
    gate_row = lax.broadcasted_iota(jnp.int32, (N_GATES, 1), 0)
    eye_c = (row_id == lax.broadcasted_iota(jnp.int32, (1, CHUNK), 1)).astype(F32)
    ones_b = jnp.ones((CHUNK, HEAD), BF16)

    def conv_silu(src, ci, n, w):
        s = pl.multiple_of(ci * CHUNK, CHUNK)
        cur = src[0, pl.ds(s, CHUNK), :]
        up = src[0, pl.ds(pl.multiple_of(jnp.maximum(s - 8, 0), 8), 8), :]
        dn = src[0, pl.ds(pl.multiple_of(jnp.minimum(s + CHUNK, n - 8), 8), 8), :]
        prev_row = jnp.where(s > 0, up[7:8, :], 0.0)
        next_row = jnp.where(s + CHUNK < n, dn[0:1, :], 0.0)
        prev = jnp.where(row_id == 0, prev_row, pltpu.roll(cur, 1, 0))
        nxt = jnp.where(row_id == CHUNK - 1, next_row, pltpu.roll(cur, CHUNK - 1, 0))
        return _silu(w[0:1, :] * prev + w[1:2, :] * cur + w[2:3, :] * nxt)

    def pick_row(a, idx):
        return jnp.sum(jnp.where(gate_row == idx, a, 0.0), axis=0, keepdims=True)

    def phase_a(q_ref, k_ref, v_ref, g_ref, off, n):
        par = _largest_divisor(n // CHUNK, (8, 4, 2, 1))

        def body(c, carry):
            chunks = []
            for u in range(par):
                ci = c * par + u
                rows = pl.ds(pl.multiple_of(ci * CHUNK, CHUNK), CHUNK)
                srows = pl.ds(pl.multiple_of(off + ci * CHUNK, CHUNK), CHUNK)
                gt = _dot3_nt(eye_ref[...], g_ref[0, rows, :]) + bias_ref[:, 0:CHUNK]
                qb = (conv_silu(q_ref, ci, n, cw_ref[0]) * (HEAD ** -0.5)).astype(BF16)
                k = conv_silu(k_ref, ci, n, cw_ref[1])
                q_s[srows, :] = qb
                vaug = jnp.concatenate([v_ref[0, rows, :].astype(BF16), ones_b], axis=1)
                chunks.append(dict(k=k, qk=_dot_nt(qb, k.astype(BF16)), vaug=vaug, gt=gt, srows=srows,
                                   chunk=off // CHUNK + ci))
            chains = []
            for ck in chunks:
                for d in range(2):
                    gi = 2 * B_HEADS * d + h
                    li_row = pick_row(ck["gt"], gi) * LOG2E
                    lf_row = _log_sigmoid(pick_row(ck["gt"], gi + B_HEADS)) * LOG2E
                    lhs = jnp.concatenate([lt_ref[d, 0] * lf_row,
                                           lt_ref[d, 1] * lf_row + eye_c * li_row], axis=0)
                    out = _dot(jnp.concatenate(_split2(lhs), axis=1), rhs_ref[d])
                    chains.append(dict(ck, d=d, li_row=li_row, out=out))
            for ch in chains:
                d, out = ch["d"], ch["out"]
                cum_rep = out[0:CHUNK, HEAD:]
                logw_rep = out[CHUNK:, HEAD:]
                log_d = jnp.where(causal_ref[d] > 0.0, out[0:CHUNK, 0:CHUNK] + ch["li_row"], NEG)
                a_loc = jnp.max(log_d, axis=1, keepdims=True)
                s_s[d, ch["srows"], :] = (ch["qk"] * jnp.exp2(log_d - a_loc)).astype(BF16)
                cols_s[d, ch["srows"], :] = jnp.where(lane < CHUNK, a_loc, cum_rep)
                total = cum_rep[CHUNK - 1:CHUNK, :] if d == 0 else cum_rep[0:1, :]
                m_loc = jnp.max(logw_rep, axis=0, keepdims=True)
                kw = (ch["k"] * jnp.exp2(logw_rep - m_loc)).astype(BF16)
                ch["u"] = _dot_tn(kw, ch["vaug"])
                sc_s[d, pl.ds(pl.multiple_of(ch["chunk"] * 8, 8), 8), :] = jnp.where(
                    sub8 == 0, total, jnp.where(sub8 == 1, m_loc, 0.0))
            for ch in chains:
                u_s[ch["d"], pl.ds(pl.multiple_of(ch["chunk"] * HEAD, HEAD), HEAD), :] = ch["u"].astype(BF16)
            return carry
        lax.fori_loop(0, n // CHUNK // par, body, 0)

    phase_a(qc_ref, kc_ref, vc_ref, gc_ref, 0, n_ctx)
    phase_a(ql_ref, kl_ref, vl_ref, gl_ref, n_ctx, n_lat)

    st_s[...] = jnp.zeros_like(st_s)
    steps = _largest_divisor(n_chunks, (4, 2, 1))
    for d in range(2):
        def phase_b(jj, m, d=d):
            for u in range(steps):
                chunk = _scan_row(jj * steps + u, d == 1, n_ctx_chunks, n_chunks) // CHUNK
                srow = pl.ds(pl.multiple_of(chunk * 8, 8), 8)
                urow = pl.ds(pl.multiple_of(chunk * HEAD, HEAD), HEAD)
                sc = sc_s[d, srow, :]
                total, m_loc = sc[0:1, :], sc[1:2, :]
                m_new = jnp.maximum(total + m, m_loc)
                state = st_s[d]
                inc = u_s[d, urow, :].astype(F32)
                u_s[d, urow, :] = state.astype(BF16)
                sc_s[d, srow, :] = jnp.where(sub8 == 2, m, sc)
                keep = jnp.exp2(total + m - m_new)
                gain = jnp.exp2(m_loc - m_new)
                st_s[d] = (jnp.concatenate([keep, keep], axis=1) * state
                           + jnp.concatenate([gain, gain], axis=1) * inc)
                m = m_new
            return m
        lax.fori_loop(0, n_chunks // steps, phase_b, jnp.zeros((1, HEAD), F32))

    def phase_c(v_ref, o_ref, z_ref, out_ref, off, n):
        par = _largest_divisor(n // CHUNK, (8, 4, 2, 1))

        def body(c, carry):
            chains = []
            for u in range(par):
                ci = c * par + u
                rows = pl.ds(pl.multiple_of(ci * CHUNK, CHUNK), CHUNK)
                srows = pl.ds(pl.multiple_of(off + ci * CHUNK, CHUNK), CHUNK)
                chunk = off // CHUNK + ci
                qb = q_s[srows, :]
                vaug = jnp.concatenate([v_ref[0, rows, :].astype(BF16), ones_b], axis=1)
                for d in range(2):
                    intra = _dot(s_s[d, srows, :], vaug)
                    inter = _dot(qb, u_s[d, pl.ds(pl.multiple_of(chunk * HEAD, HEAD), HEAD), :])
                    chains.append(dict(d=d, u=u, rows=rows, srows=srows, chunk=chunk, intra=intra, inter=inter))
            ys = [None] * par
            for ch in chains:
                d = ch["d"]
                cols = cols_s[d, ch["srows"], :]
                swapped = pltpu.roll(cols, CHUNK, 1)
                a_rep = jnp.where(lane < CHUNK, cols, swapped)
                cum_rep = jnp.where(lane < CHUNK, swapped, cols)
                m = sc_s[d, pl.ds(pl.multiple_of(ch["chunk"] * 8, 8), 8), :][2:3, :]
                inter_l = cum_rep + m
                m_t = jnp.maximum(a_rep, inter_l)
                wi = jnp.exp2(a_rep - m_t)
                we = jnp.exp2(inter_l - m_t)
                num = wi * ch["intra"][:, :HEAD] + we * ch["inter"][:, :HEAD]
                den = wi * ch["intra"][:, HEAD:] + we * ch["inter"][:, HEAD:]
                hd = num / jnp.maximum(jnp.abs(den), jnp.exp2(-m_t))
                ys[ch["u"]] = hd if ys[ch["u"]] is None else ys[ch["u"]] + hd
                if d == 1:
                    rows = ch["rows"]
                    hh = jax.nn.sigmoid(o_ref[0, rows, :]) * ys[ch["u"]]
                    ms = jnp.mean(hh * hh, axis=-1, keepdims=True)
                    out_ref[0, rows, :] = (hh * lax.rsqrt(ms + EPS) * _silu(z_ref[0, rows, :])).astype(out_ref.dtype)
            return carry
        lax.fori_loop(0, n // CHUNK // par, body, 0)

    phase_c(vc_ref, oc_ref, zc_ref, outc_ref, 0, n_ctx)
    phase_c(vl_ref, ol_ref, zl_ref, outl_ref, n_ctx, n_lat)


def _mlstm(p_ctx, p_lat, g_ctx, g_lat, conv_w, gate_bias, consts):
    _, bsz, n_ctx, _ = p_ctx.shape
    n_lat = p_lat.shape[2]
    t = n_ctx + n_lat
    n_chunks = t // CHUNK
    lt, rhs, causal = consts
    eye = jnp.eye(N_GATES, HEAD, dtype=BF16)
    cw = conv_w.reshape(3, 2, B_HEADS * HEAD).transpose(1, 0, 2)
    bias = jnp.broadcast_to(gate_bias.reshape(N_GATES, 1), (N_GATES, HEAD))

    def col(n, c0):
        return pl.BlockSpec((None, 1, n, HEAD), lambda b, h, c0=c0: (c0 + h, b, 0, 0))

    def gate_spec(n):
        return pl.BlockSpec((1, n, HEAD), lambda b, h: (b, 0, 0))

    cols = (COL_B_Q, COL_B_K, COL_B_V, COL_B_O, COL_B_Z)
    return pl.pallas_call(
        _mlstm_kernel,
        grid=(bsz, B_HEADS),
        in_specs=[
            pl.BlockSpec(lt.shape, lambda b, h: (0, 0, 0, 0)),
            pl.BlockSpec(rhs.shape, lambda b, h: (0, 0, 0)),
            pl.BlockSpec(causal.shape, lambda b, h: (0, 0, 0)),
            pl.BlockSpec(eye.shape, lambda b, h: (0, 0)),
            pl.BlockSpec((2, 3, HEAD), lambda b, h: (0, 0, h)),
            pl.BlockSpec((N_GATES, HEAD), lambda b, h: (0, 0)),
        ] + [col(n_ctx, c) for c in cols] + [gate_spec(n_ctx)]
          + [col(n_lat, c) for c in cols] + [gate_spec(n_lat)],
        out_specs=[
            pl.BlockSpec((1, n_ctx, HEAD), lambda b, h: (b, 0, h)),
            pl.BlockSpec((1, n_lat, HEAD), lambda b, h: (b, 0, h)),
        ],
        out_shape=[
            jax.ShapeDtypeStruct((bsz, n_ctx, B_HEADS * HEAD), BF16),
            jax.ShapeDtypeStruct((bsz, n_lat, B_HEADS * HEAD), BF16),
        ],
        scratch_shapes=[
            pltpu.VMEM((t, HEAD), BF16),
            pltpu.VMEM((2, t, CHUNK), BF16),
            pltpu.VMEM((2, t, HEAD), F32),
            pltpu.VMEM((2, n_chunks * HEAD, 2 * HEAD), BF16),
            pltpu.VMEM((2, n_chunks * 8, HEAD), F32),
            pltpu.VMEM((2, HEAD, 2 * HEAD), F32),
        ],
        compiler_params=_params(("arbitrary", "arbitrary")),
        name="mlstm",
    )(lt, rhs, causal, eye, cw, bias, *([p_ctx] * 5), g_ctx, *([p_lat] * 5), g_lat)


def _rope(x, cos, sin_lo, sin_hi):
    return x * cos + pltpu.roll(x, HEAD - 32, 1) * sin_lo + pltpu.roll(x, 32, 1) * sin_hi


def _softmax_attend(sink_col, scores, values):
    m = jnp.maximum(sink_col, jnp.max(scores, axis=1, keepdims=True))
    p = jnp.exp2(scores - m)
    den = jnp.exp2(sink_col - m) + jnp.sum(p, axis=1, keepdims=True)
    return _dot(p.astype(BF16), values) / den


def _sink_column(sink_ref, kv, rows_per_head):
    row = lax.broadcasted_iota(jnp.int32, (C_GROUP * rows_per_head, 1), 0)
    col = jnp.zeros((C_GROUP * rows_per_head, 1), F32)
    for g in range(C_GROUP):
        col = jnp.where(row // rows_per_head == g, sink_ref[kv * C_GROUP + g] * LOG2E, col)
    return col


def _window_masks():
    r = np.arange(ATTN_BLOCK)[:, None]
    c = np.arange(3 * ATTN_BLOCK)[None, :]
    return jnp.asarray(np.stack([np.abs(c - p * ATTN_BLOCK - r) <= WINDOW for p in range(3)]), F32)


def _attn_kernel(sink_ref, mask_ref, cos_ref, slo_ref, shi_ref, q_ref, k_ref, v_ref, kc_ref, vc_ref, z_ref,
                 o_ref, kr_s):
    kv = pl.program_id(1)
    i = pl.program_id(2)
    n = k_ref.shape[1]
    span = 3 * ATTN_BLOCK

    @pl.when(i == 0)
    def _():
        def body(c, carry):
            rows = pl.ds(pl.multiple_of(c * ATTN_BLOCK, ATTN_BLOCK), ATTN_BLOCK)
            kr_s[rows, :] = _rope(k_ref[0, rows, :], cos_ref[rows, :], slo_ref[rows, :],
                                  shi_ref[rows, :]).astype(BF16)
            return carry
        lax.fori_loop(0, n // ATTN_BLOCK, body, 0)

    scale = HEAD ** -0.5 * LOG2E
    blocks_per_step = q_ref.shape[1] // ATTN_BLOCK

    par = _largest_divisor(blocks_per_step, (4, 2, 1))

    def block_group(trip, carry):
        chains = []
        for u in range(par):
            sub = trip * par + u
            blk = i * blocks_per_step + sub
            local = pl.ds(pl.multiple_of(sub * ATTN_BLOCK, ATTN_BLOCK), ATTN_BLOCK)
            qrows = pl.ds(pl.multiple_of(blk * ATTN_BLOCK, ATTN_BLOCK), ATTN_BLOCK)
            cos = cos_ref[qrows, :]
            slo = slo_ref[qrows, :]
            shi = shi_ref[qrows, :]
            start = pl.multiple_of(jnp.clip((blk - 1) * ATTN_BLOCK, 0, n - span), ATTN_BLOCK)
            keys = jnp.concatenate([kr_s[pl.ds(start, span), :], kc_ref[0].astype(BF16)], axis=0)
            values = jnp.concatenate([v_ref[0, pl.ds(start, span), :].astype(BF16), vc_ref[0].astype(BF16)],
                                     axis=0)
            valid = mask_ref[(blk * ATTN_BLOCK - start) // ATTN_BLOCK] > 0.0
            for g in range(C_GROUP):
                qg = (_rope(q_ref[g, local, :], cos, slo, shi) * scale).astype(BF16)
                s = _dot_nt(qg, keys)
                s = jnp.concatenate([jnp.where(valid, s[:, :span], NEG), s[:, span:]], axis=1)
                chains.append((g, local, s, values))
        for g, local, s, values in chains:
            sink = jnp.full((ATTN_BLOCK, 1), sink_ref[kv * C_GROUP + g] * LOG2E, F32)
            out = _softmax_attend(sink, s, values)
            o_ref[0, local, g * HEAD:(g + 1) * HEAD] = (out * _silu(z_ref[g, local, :])).astype(o_ref.dtype)
        return carry

    lax.fori_loop(0, blocks_per_step // par, block_group, 0)


def _attention(p_ctx, p_lat, sink, rope):
    _, bsz, n_ctx, _ = p_ctx.shape
    n = p_lat.shape[2]
    cos, slo, shi = rope
    masks = _window_masks()
    qrows = ATTN_BLOCK * _largest_divisor(n // ATTN_BLOCK, (8, 4, 2, 1))
    gw = C_GROUP * HEAD
    gblk = C_GROUP

    full = pl.BlockSpec((n, HEAD), lambda b, kv, i: (0, 0))
    return pl.pallas_call(
        _attn_kernel,
        grid=(bsz, C_KV_HEADS, n // qrows),
        in_specs=[
            pl.BlockSpec(memory_space=pltpu.SMEM),
            pl.BlockSpec(masks.shape, lambda b, kv, i: (0, 0, 0)),
            full, full, full,
            pl.BlockSpec((C_GROUP, None, qrows, HEAD), lambda b, kv, i: (COL_C_Q // gblk + kv, b, i, 0)),
            pl.BlockSpec((None, 1, n, HEAD), lambda b, kv, i: (COL_C_K + kv, b, 0, 0)),
            pl.BlockSpec((None, 1, n, HEAD), lambda b, kv, i: (COL_C_V + kv, b, 0, 0)),
            pl.BlockSpec((None, 1, n_ctx, HEAD), lambda b, kv, i: (COL_C_K + kv, b, 0, 0)),
            pl.BlockSpec((None, 1, n_ctx, HEAD), lambda b, kv, i: (COL_C_V + kv, b, 0, 0)),
            pl.BlockSpec((C_GROUP, None, qrows, HEAD), lambda b, kv, i: (COL_C_Z // gblk + kv, b, i, 0)),
        ],
        out_specs=pl.BlockSpec((1, qrows, gw), lambda b, kv, i: (b, i, kv)),
        out_shape=jax.ShapeDtypeStruct((bsz, n, C_Q_HEADS * HEAD), BF16),
        scratch_shapes=[pltpu.VMEM((n, HEAD), BF16)],
        compiler_params=_params(("arbitrary", "arbitrary", "arbitrary")),
        name="window_attention",
    )(sink, masks, cos, slo, shi, p_lat, p_lat, p_lat, p_ctx, p_ctx, p_lat)


def _ctx_attn_kernel(sink_ref, q_ref, k_ref, v_ref, z_ref, o_ref):
    kv = pl.program_id(1)
    n_ctx = q_ref.shape[1]
    scale = HEAD ** -0.5 * LOG2E
    q4 = jnp.concatenate([q_ref[g] * scale for g in range(C_GROUP)],
                         axis=0).astype(BF16)
    s = _dot_nt(q4, k_ref[0].astype(BF16))
    out = _softmax_attend(_sink_column(sink_ref, kv, n_ctx), s, v_ref[0].astype(BF16))
    for g in range(C_GROUP):
        zg = z_ref[g]
        o_ref[0, :, g * HEAD:(g + 1) * HEAD] = (out[g * n_ctx:(g + 1) * n_ctx] * _silu(zg)).astype(o_ref.dtype)


def _context_attention(p_ctx, sink):
    _, bsz, n_ctx, _ = p_ctx.shape
    gw = C_GROUP * HEAD
    gblk = C_GROUP
    return pl.pallas_call(
        _ctx_attn_kernel,
        grid=(bsz, C_KV_HEADS),
        in_specs=[
            pl.BlockSpec(memory_space=pltpu.SMEM),
            pl.BlockSpec((C_GROUP, None, n_ctx, HEAD), lambda b, kv: (COL_C_Q // gblk + kv, b, 0, 0)),
            pl.BlockSpec((None, 1, n_ctx, HEAD), lambda b, kv: (COL_C_K + kv, b, 0, 0)),
            pl.BlockSpec((None, 1, n_ctx, HEAD), lambda b, kv: (COL_C_V + kv, b, 0, 0)),
            pl.BlockSpec((C_GROUP, None, n_ctx, HEAD), lambda b, kv: (COL_C_Z // gblk + kv, b, 0, 0)),
        ],
        out_specs=pl.BlockSpec((1, n_ctx, gw), lambda b, kv: (b, 0, kv)),
        out_shape=jax.ShapeDtypeStruct((bsz, n_ctx, C_Q_HEADS * HEAD), BF16),
        compiler_params=_params(("arbitrary", "arbitrary")),
        name="context_attention",
    )(sink, p_ctx, p_ctx, p_ctx, p_ctx)


def _rope_tables(n):
    axis_dim = HEAD // 2
    rows = n // GRID_W
    row = jnp.repeat(jnp.arange(rows, dtype=F32), GRID_W)
    col = jnp.tile(jnp.arange(GRID_W, dtype=F32), rows)
    inv_freq = ROPE_BASE ** (-jnp.arange(0, axis_dim, 2, dtype=F32) / axis_dim)
    ang_r = row[:, None] * inv_freq[None, :]
    ang_c = col[:, None] * inv_freq[None, :]
    zero = jnp.zeros_like(ang_r)
    cos = jnp.concatenate([jnp.cos(ang_r)] * 2 + [jnp.cos(ang_c)] * 2, axis=-1)
    sin_lo = jnp.concatenate([-jnp.sin(ang_r), zero, -jnp.sin(ang_c), zero], axis=-1)
    sin_hi = jnp.concatenate([zero, jnp.sin(ang_r), zero, jnp.sin(ang_c)], axis=-1)
    return cos, sin_lo, sin_hi


def _pick_tile(n, candidates):
    for c in candidates:
        if n % c == 0:
            return c
    raise ValueError(f"no tile for {n}")


def kernel(x, c, ctx, c_ctx, w_mod, b_mod, g_pre, g_post, w_in, hgrn_lb_logits, mlstm_conv_w,
           mlstm_gate_bias, attn_sink, w_out):
    bsz, n_lat, d = x.shape
    n_ctx = ctx.shape[1]
    depth = w_mod.shape[0]
    assert n_ctx % CHUNK == 0 and n_lat % ATTN_BLOCK == 0 and n_lat >= 3 * ATTN_BLOCK

    cond = jnp.concatenate([c, c_ctx[None, :]], axis=0)
    pad = (-cond.shape[0]) % 8
    cond = jnp.pad(cond, ((0, pad), (0, 0)))
    mod = _modulation(cond, w_mod, b_mod)

    w_t = _weight_prep(jnp.swapaxes(w_in, 1, 2))
    wo = w_out.astype(BF16)
    consts = _scan_constants()
    mlstm_consts = _mlstm_constants()
    rope = _rope_tables(n_lat)

    tm_lat = _pick_tile(n_lat, (1024, 512, 256, 128))
    tm_ctx = _pick_tile(bsz * n_ctx, (1024, 512, 256, 128))
    tn = _pick_tile(MAIN_WIDTH, (1536, 512))
    to_lat = _pick_tile(n_lat, (512, 256, 128))
    to_ctx = _pick_tile(bsz * n_ctx, (512, 256, 128))

    x2d = x.reshape(bsz * n_lat, d)
    h2d = ctx.reshape(bsz * n_ctx, d)
    for layer in range(depth):
        with_ctx = layer < depth - 1
        shift_l = mod[layer, :bsz, 0:d].reshape(bsz, 1, d)
        scale_l = mod[layer, :bsz, d:2 * d].reshape(bsz, 1, d)
        gate_l = mod[layer, :bsz, 2 * d:].reshape(bsz, 1, d)
        shift_c = mod[layer, bsz:bsz + 1, 0:d].reshape(1, 1, d)
        scale_c = mod[layer, bsz:bsz + 1, d:2 * d].reshape(1, 1, d)
        gate_c = mod[layer, bsz:bsz + 1, 2 * d:].reshape(1, 1, d)

        gain_pre = g_pre[layer].reshape(1, d)
        gain_post = g_post[layer].reshape(1, d)

        p_lat, g_lat = _in_projection(x2d, shift_l, scale_l, gain_pre, w_t, layer, n_lat, tm_lat, tn)
        p_ctx, g_ctx = _in_projection(h2d, shift_c, scale_c, gain_pre, w_t, layer, bsz * n_ctx, tm_ctx, tn)
        p_lat = p_lat.reshape(MAIN_BLOCKS, bsz, n_lat, HEAD)
        p_ctx = p_ctx.reshape(MAIN_BLOCKS, bsz, n_ctx, HEAD)
        g_lat = g_lat.reshape(bsz, n_lat, HEAD)
        g_ctx = g_ctx.reshape(bsz, n_ctx, HEAD)

        a_c, a_l = _hgrn2(p_ctx, p_lat, hgrn_lb_logits, consts, layer)
        b_c, b_l = _mlstm(p_ctx, p_lat, g_ctx, g_lat, mlstm_conv_w[layer], mlstm_gate_bias[layer],
                          mlstm_consts)
        c_l = _attention(p_ctx, p_lat, attn_sink[layer], rope)

        x2d = _out_projection(a_l.reshape(bsz * n_lat, -1), b_l.reshape(bsz * n_lat, -1),
                              c_l.reshape(bsz * n_lat, -1), wo, layer, x2d, gate_l, gain_post, n_lat, to_lat)
        if with_ctx:
            c_c = _context_attention(p_ctx, attn_sink[layer])
            h2d = _out_projection(a_c.reshape(bsz * n_ctx, -1), b_c.reshape(bsz * n_ctx, -1),
                                  c_c.reshape(bsz * n_ctx, -1), wo, layer, h2d, gate_c, gain_post,
                                  bsz * n_ctx, to_ctx)
    return x2d.reshape(bsz, n_lat, d)
```

```python
import functools

import numpy as np
import jax
import jax.numpy as jnp
from jax import lax
from jax.experimental import pallas as pl
from jax.experimental.pallas import tpu as pltpu

F32 = jnp.float32
BF16 = jnp.bfloat16

EPS = 1e-6
NEG = -1e30
LOG2E = 1.4426950408889634
HEAD = 128
A_HEADS = 4
B_HEADS = 4
C_Q_HEADS = 8
C_KV_HEADS = 2
C_GROUP = C_Q_HEADS // C_KV_HEADS
CHUNK = 64
ATTN_BLOCK = 128
WINDOW = 128
GRID_W = 64
ROPE_BASE = 10000.0
N_GATES = 4 * B_HEADS

COL_A_Q, COL_A_FF, COL_A_FB, COL_A_I, COL_A_GATE = 0, 4, 8, 12, 16
COL_B_Q, COL_B_K, COL_B_V, COL_B_O, COL_B_Z = 20, 24, 28, 32, 36
COL_C_Q, COL_C_K, COL_C_V, COL_C_Z = 40, 48, 50, 52
MAIN_BLOCKS = 60
MAIN_WIDTH = MAIN_BLOCKS * HEAD
GATES_OFFSET = 9 * 4 * HEAD

VMEM_LIMIT = 56 * 1024 * 1024


def _params(sem):
    return pltpu.CompilerParams(dimension_semantics=sem, vmem_limit_bytes=VMEM_LIMIT)


def _silu(x):
    return x * jax.nn.sigmoid(x)


def _log_sigmoid(x):
    return jnp.minimum(x, 0.0) - jnp.log(1.0 + jnp.exp(-jnp.abs(x)))


def _dot(a, b):
    return jnp.dot(a, b, preferred_element_type=F32)


def _dot_nt(a, b):
    return lax.dot_general(a, b, (((1,), (1,)), ((), ())), preferred_element_type=F32)


def _dot_tn(a, b):
    return lax.dot_general(a, b, (((0,), (0,)), ((), ())), preferred_element_type=F32)


def _split3(x):
    hi = x.astype(BF16)
    r = x - hi.astype(F32)
    mid = r.astype(BF16)
    lo = (r - mid.astype(F32)).astype(BF16)
    return hi, mid, lo


def _split2(x):
    hi = x.astype(BF16)
    return hi, (x - hi.astype(F32)).astype(BF16)


def _dot3_nt(w, x):
    hi, mid, lo = _split3(x)
    return _dot_nt(w, hi) + _dot_nt(w, mid) + _dot_nt(w, lo)


def _mod_kernel(c_ref, w_ref, b_ref, o_ref):
    a = _silu(c_ref[...])
    o_ref[0] = jnp.dot(a, w_ref[0], preferred_element_type=F32,
                       precision=lax.Precision.HIGHEST) + b_ref[0]


def _modulation(cond, w_mod, b_mod):
    depth, d, n = w_mod.shape
    rows = cond.shape[0]
    tn = _pick_tile(n, (1536, 768, 384, 128))
    assert n % tn == 0
    return pl.pallas_call(
        _mod_kernel,
        grid=(depth, n // tn),
        in_specs=[
            pl.BlockSpec((rows, d), lambda l, j: (0, 0)),
            pl.BlockSpec((1, d, tn), lambda l, j: (l, 0, j)),
            pl.BlockSpec((1, 1, tn), lambda l, j: (l, 0, j)),
        ],
        out_specs=pl.BlockSpec((1, rows, tn), lambda l, j: (l, 0, j)),
        out_shape=jax.ShapeDtypeStruct((depth, rows, n), F32),
        compiler_params=_params(("arbitrary", "arbitrary")),
        name="modulation",
    )(cond, w_mod, b_mod.reshape(depth, 1, n))


NORM_ROWS = 32
EPILOGUE_ROWS = 8


def _inproj_kernel(x_ref, shift_ref, scale_ref, g_ref, w_ref, wg_ref, p_ref, gates_ref, xn_ref, gm_ref):
    tm = x_ref.shape[0]

    @pl.when(pl.program_id(1) == 0)
    def _():
        gm_ref[...] = g_ref[...] * (1.0 + scale_ref[0])

        def body(r, carry):
            for u in range(2):
                rows = pl.ds(pl.multiple_of((2 * r + u) * NORM_ROWS, NORM_ROWS), NORM_ROWS)
                xr = x_ref[rows, :]
                ms = jnp.mean(xr * xr, axis=-1, keepdims=True)
                xn_ref[rows, :] = (xr * (lax.rsqrt(ms + EPS) * gm_ref[...]) + shift_ref[0]).astype(BF16)
            return carry

        lax.fori_loop(0, tm // NORM_ROWS // 2, body, 0)
        gates_ref[...] = _dot_nt(xn_ref[...], wg_ref[...])

    res = _dot_nt(xn_ref[...], w_ref[...])
    for t in range(p_ref.shape[0]):
        p_ref[t] = res[:, t * HEAD:(t + 1) * HEAD]


PREP_ROWS = 512
MAIN_PREP_BLOCKS = MAIN_WIDTH // PREP_ROWS
assert GATES_OFFSET % PREP_ROWS == 0 and MAIN_WIDTH % PREP_ROWS == 0


def _weight_prep_kernel(w_ref, o_ref):
    c = pl.program_id(1)
    row = lax.broadcasted_iota(jnp.int32, (PREP_ROWS, 1), 0)
    keep = jnp.logical_or(c < MAIN_PREP_BLOCKS, row < N_GATES)
    o_ref[...] = jnp.where(keep, w_ref[0], 0.0).astype(BF16)


def _weight_prep(w_in_t):
    depth, n, d = w_in_t.shape

    def src_row(c):
        after = jnp.where(c * PREP_ROWS >= GATES_OFFSET, N_GATES, 0)
        return pl.multiple_of(jnp.where(c < MAIN_PREP_BLOCKS, c * PREP_ROWS + after, GATES_OFFSET), N_GATES)

    return pl.pallas_call(
        _weight_prep_kernel,
        grid=(depth, MAIN_PREP_BLOCKS + 1),
        in_specs=[pl.BlockSpec((pl.Element(1), pl.Element(PREP_ROWS), pl.Element(d)),
                               lambda l, c: (l, src_row(c), 0))],
        out_specs=pl.BlockSpec((None, PREP_ROWS, d), lambda l, c: (l, c, 0)),
        out_shape=jax.ShapeDtypeStruct((depth, MAIN_WIDTH + PREP_ROWS, d), BF16),
        compiler_params=_params(("arbitrary", "arbitrary")),
        name="weight_prep",
    )(w_in_t)


def _in_projection(x2d, shift, scale, gain, w_t, layer, rows_per_mod, tm, tn):
    r, d = x2d.shape
    n = MAIN_WIDTH
    assert r % tm == 0 and n % tn == 0 and rows_per_mod % tm == 0
    per = rows_per_mod // tm
    return pl.pallas_call(
        _inproj_kernel,
        grid=(r // tm, n // tn),
        in_specs=[
            pl.BlockSpec((tm, d), lambda i, j: (i, 0)),
            pl.BlockSpec((1, 1, d), lambda i, j: (i // per, 0, 0)),
            pl.BlockSpec((1, 1, d), lambda i, j: (i // per, 0, 0)),
            pl.BlockSpec((1, d), lambda i, j: (0, 0)),
            pl.BlockSpec((None, tn, d), lambda i, j: (layer, j, 0)),
            pl.BlockSpec((None, HEAD, d), lambda i, j: (layer, MAIN_BLOCKS, 0)),
        ],
        out_specs=[
            pl.BlockSpec((tn // HEAD, tm, HEAD), lambda i, j: (j, i, 0)),
            pl.BlockSpec((tm, HEAD), lambda i, j: (i, 0)),
        ],
        out_shape=[
            jax.ShapeDtypeStruct((n // HEAD, r, HEAD), F32),
            jax.ShapeDtypeStruct((r, HEAD), F32),
        ],
        scratch_shapes=[pltpu.VMEM((tm, d), BF16), pltpu.VMEM((1, d), F32)],
        compiler_params=_params(("arbitrary", "arbitrary")),
        name="in_projection",
    )(x2d, shift, scale, gain, w_t, w_t)


def _outproj_kernel(a_ref, b_ref, c_ref, w_ref, x_ref, gate_ref, g_ref, o_ref, y_s):
    i = pl.program_id(0)
    tm = x_ref.shape[0]

    @pl.when(i == 0)
    def _():
        y_s[1] = jnp.zeros(y_s.shape[1:], F32)

    def step(cur, prev):
        for r in range(tm // EPILOGUE_ROWS):
            rows = slice(r * EPILOGUE_ROWS, (r + 1) * EPILOGUE_ROWS)
            y = y_s[prev, rows, :]
            ms = jnp.mean(y * y, axis=-1, keepdims=True)
            o_ref[rows, :] = x_ref[rows, :] + y * (lax.rsqrt(ms + EPS) * (gate_ref[0] * g_ref[...]))
        y_s[cur] = _dot(jnp.concatenate([a_ref[...], b_ref[...], c_ref[...]], axis=1), w_ref[...])

    pl.when(i % 2 == 0)(lambda: step(0, 1))
    pl.when(i % 2 == 1)(lambda: step(1, 0))


def _out_projection(a, b, c, w_out, layer, x2d, gate, gain, rows_per_mod, tm):
    r, d = x2d.shape
    assert r % tm == 0 and rows_per_mod % tm == 0
    per = rows_per_mod // tm
    n = r // tm
    cur = lambda i: (jnp.minimum(i, n - 1), 0)
    prev = lambda i: (jnp.maximum(i - 1, 0), 0)
    return pl.pallas_call(
        _outproj_kernel,
        grid=(n + 1,),
        in_specs=[
            pl.BlockSpec((tm, a.shape[1]), cur),
            pl.BlockSpec((tm, b.shape[1]), cur),
            pl.BlockSpec((tm, c.shape[1]), cur),
            pl.BlockSpec((None,) + w_out.shape[1:], lambda i: (layer, 0, 0)),
            pl.BlockSpec((tm, d), prev),
            pl.BlockSpec((1, 1, d), lambda i: (jnp.maximum(i - 1, 0) // per, 0, 0)),
            pl.BlockSpec((1, d), lambda i: (0, 0)),
        ],
        out_specs=pl.BlockSpec((tm, d), prev),
        out_shape=jax.ShapeDtypeStruct((r, d), F32),
        scratch_shapes=[pltpu.VMEM((2, tm, d), F32)],
        compiler_params=_params(("arbitrary",)),
        name="out_projection",
    )(a, b, c, w_out, x2d, gate, gain)


N_LEVELS = 7


def _scan_matrices(reverse):
    n = CHUNK
    t = np.arange(n)[:, None]
    u = np.arange(n)[None, :]
    cum = (u >= t) if reverse else (u <= t)
    masks = [t == u]
    size = 2
    while size <= n:
        half = size // 2
        same = (t // size) == (u // size)
        if reverse:
            masks.append(same & ((t % size) < half) & ((u % size) >= half))
        else:
            masks.append(same & ((t % size) >= half) & ((u % size) < half))
        size *= 2
    return cum.astype(np.float32), np.stack(masks).astype(np.float32)


def _scan_constants():
    cf, mf = _scan_matrices(False)
    cb, mb = _scan_matrices(True)
    return (jnp.asarray(np.stack([cf, cb]), BF16), jnp.asarray(np.stack([mf, mb]), F32))


def _largest_divisor(n, candidates):
    return next(c for c in candidates if n % c == 0)


def _scan_row(j, reverse, n_ctx_chunks, n_chunks):
    if not reverse:
        return j * CHUNK
    ctx_part = (n_ctx_chunks - 1 - j) * CHUNK
    lat_part = (n_ctx_chunks + (n_chunks - 1 - j)) * CHUNK
    return jnp.where(j < n_ctx_chunks, ctx_part, lat_part)


def _hgrn2_kernel(lbl_ref, lmat_ref, mk_ref,
                  qc_ref, ffc_ref, fbc_ref, ic_ref, gc_ref,
                  ql_ref, ffl_ref, fbl_ref, il_ref, gl_ref,
                  oc_ref, ol_ref,
                  qd_s, y_s, u_s, dec_s, st_s, *, layer):
    n_ctx = qc_ref.shape[1]
    n_lat = ql_ref.shape[1]
    n_ctx_chunks = n_ctx // CHUNK
    n_chunks = (n_ctx + n_lat) // CHUNK

    logits = lbl_ref[...]
    e = jnp.exp(logits - jnp.max(logits, axis=0, keepdims=True))
    sm = e / jnp.sum(e, axis=0, keepdims=True)
    lbs = jnp.sum(sm[:layer + 1], axis=0) - sm[0]

    row = lax.broadcasted_iota(jnp.int32, (CHUNK, 1), 0)
    r4 = row % 4

    def level_factor(size, d, f, cum):
        half = size // 2
        if size == 2:
            return jnp.where((row % 2 == 1) if d == 0 else (row % 2 == 0), f, 1.0)
        if size == 4:
            f_next = pltpu.roll(f, CHUNK - 1, 0)
            f_prev = pltpu.roll(f, 1, 0)
            if d == 0:
                return jnp.where(r4 == 0, f_next, jnp.where(r4 == 1, 1.0, jnp.where(r4 == 2, f, f * f_prev)))
            return jnp.where(r4 == 0, f * f_next, jnp.where(r4 == 1, f, jnp.where(r4 == 2, 1.0, f_prev)))
        mids = [b * size + (half - 1 if d == 0 else half) for b in range(CHUNK // size)]
        ref = jnp.concatenate([jnp.broadcast_to(cum[m:m + 1, :], (size, HEAD)) for m in mids], axis=0)
        return jnp.exp2(-jnp.abs(cum - ref))

    key_lane = lax.broadcasted_iota(jnp.int32, (1, CHUNK), 1)
    fine_levels = 4

    def assemble_scores(d, dots):
        groups = []
        for g in range(CHUNK // 8):
            r = slice(8 * g, 8 * g + 8)
            acc = mk_ref[d, 0, r, :] * dots[0][r]
            for lvl in range(1, fine_levels):
                acc = acc + mk_ref[d, lvl, r, :] * dots[lvl][r]
            for lvl in range(fine_levels, N_LEVELS):
                size = 2 ** lvl
                half = size // 2
                base = (8 * g // size) * size
                on_query_side = (8 * g - base >= half) if d == 0 else (8 * g - base < half)
                if on_query_side:
                    lo = base if d == 0 else base + half
                    acc = jnp.where((key_lane >= lo) & (key_lane < lo + half), dots[lvl][r], acc)
            groups.append(acc)
        return jnp.concatenate(groups, axis=0)

    def phase_a(q_ref, f_refs, i_ref, off, n):
        par = _largest_divisor(n // CHUNK, (16, 8, 4, 2, 1))

        def body(c, carry):
            chains = []
            for u in range(par):
                ci = c * par + u
                rows = pl.ds(pl.multiple_of(ci * CHUNK, CHUNK), CHUNK)
                srows = pl.ds(pl.multiple_of(off + ci * CHUNK, CHUNK), CHUNK)
                q = _silu(q_ref[0, rows, :])
                v = i_ref[0, rows, :].astype(BF16)
                for d in range(2):
                    lb = lbs[d]
                    open_part = (1.0 - lb) * jax.nn.sigmoid(f_refs[d][0, rows, :])
                    f = lb + open_part
                    k = (1.0 - lb) - open_part
                    logf = jnp.log(f) * LOG2E
                    res = _dot(lmat_ref[d], jnp.concatenate(_split2(logf), axis=1))
                    chains.append(dict(d=d, q=q, v=v, k=k, f=f, res=res, srows=srows,
                                       chunk=off // CHUNK + ci))
            for ch in chains:
                d, q, k = ch["d"], ch["q"], ch["k"]
                cum = ch["res"][:, :HEAD] + ch["res"][:, HEAD:]
                total = cum[CHUNK - 1:CHUNK, :] if d == 0 else cum[0:1, :]
                qb, kb = q.astype(BF16), k.astype(BF16)
                dots = [_dot_nt(qb, kb)]
                for lvl in range(1, N_LEVELS):
                    xb = level_factor(2 ** lvl, d, ch["f"], cum).astype(BF16)
                    dots.append(_dot_nt(qb * xb, kb * xb))
                ch["dots"] = dots
                qd_s[d, ch["srows"], :] = (q * jnp.exp2(cum)).astype(BF16)
                ch["u"] = _dot_tn(ch["v"], (k * jnp.exp2(total - cum)).astype(BF16))
                drow = pl.multiple_of(ch["chunk"] * 8, 8)
                dec_s[d, pl.ds(drow, 8), :] = jnp.broadcast_to(jnp.exp2(total), (8, HEAD))
            for ch in chains:
                d = ch["d"]
                ch["y"] = _dot(assemble_scores(d, ch["dots"]).astype(BF16), ch["v"])
                u_s[d, pl.ds(pl.multiple_of(ch["chunk"] * HEAD, HEAD), HEAD), :] = ch["u"]
            for ch in chains:
                y_s[ch["d"], ch["srows"], :] = ch["y"]
            return carry
        lax.fori_loop(0, n // CHUNK // par, body, 0)

    phase_a(qc_ref, (ffc_ref, fbc_ref), ic_ref, 0, n_ctx)
    phase_a(ql_ref, (ffl_ref, fbl_ref), il_ref, n_ctx, n_lat)

    st_s[...] = jnp.zeros_like(st_s)

    steps = _largest_divisor(n_chunks, (17, 16, 8, 4, 2, 1))

    def phase_b(jj, carry):
        states = [st_s[0], st_s[1]]
        for u in range(steps):
            j = jj * steps + u
            for d in range(2):
                chunk = _scan_row(j, d == 1, n_ctx_chunks, n_chunks) // CHUNK
                slot = pl.ds(pl.multiple_of(chunk * HEAD, HEAD), HEAD)
                dec = dec_s[d, pl.ds(pl.multiple_of(chunk * 8, 8), 1), :]
                increment = u_s[d, slot, :]
                u_s[d, slot, :] = states[d]
                states[d] = states[d] * dec + increment
        st_s[0] = states[0]
        st_s[1] = states[1]
        return carry

    lax.fori_loop(0, n_chunks // steps, phase_b, 0)

    def readout(gate_ref, o_ref, off, n):
        par = _largest_divisor(n // CHUNK, (16, 8, 4, 2, 1))

        def body(c, carry):
            chunks = []
            for u in range(par):
                ci = c * par + u
                rows = pl.ds(pl.multiple_of(ci * CHUNK, CHUNK), CHUNK)
                srows = pl.ds(pl.multiple_of(off + ci * CHUNK, CHUNK), CHUNK)
                slot = pl.ds(pl.multiple_of((off // CHUNK + ci) * HEAD, HEAD), HEAD)
                inter = [_dot_nt(qd_s[d, srows, :], u_s[d, slot, :].astype(BF16)) for d in range(2)]
                chunks.append((rows, srows, inter))
            for rows, srows, inter in chunks:
                y = (y_s[0, srows, :] + inter[0]) + (y_s[1, srows, :] + inter[1])
                ms = jnp.mean(y * y, axis=-1, keepdims=True)
                o_ref[0, rows, :] = (y * lax.rsqrt(ms + EPS) * _silu(gate_ref[0, rows, :])).astype(o_ref.dtype)
            return carry
        lax.fori_loop(0, n // CHUNK // par, body, 0)

    readout(gc_ref, oc_ref, 0, n_ctx)
    readout(gl_ref, ol_ref, n_ctx, n_lat)


def _hgrn2(p_ctx, p_lat, lb_logits, consts, layer):
    _, bsz, n_ctx, _ = p_ctx.shape
    n_lat = p_lat.shape[2]
    t = n_ctx + n_lat
    lmat, mk = consts
    depth = lb_logits.shape[0]
    lbl = lb_logits.reshape(depth, 2, A_HEADS, 1, HEAD).transpose(0, 1, 3, 2, 4).reshape(depth, 2, 1, A_HEADS * HEAD)

    def col(n, c0):
        return pl.BlockSpec((None, 1, n, HEAD), lambda b, h, c0=c0: (c0 + h, b, 0, 0))

    cols = (COL_A_Q, COL_A_FF, COL_A_FB, COL_A_I, COL_A_GATE)
    return pl.pallas_call(
        functools.partial(_hgrn2_kernel, layer=layer),
        grid=(bsz, A_HEADS),
        in_specs=[
            pl.BlockSpec((depth, 2, 1, HEAD), lambda b, h: (0, 0, 0, h)),
            pl.BlockSpec(lmat.shape, lambda b, h: (0, 0, 0)),
            pl.BlockSpec(mk.shape, lambda b, h: (0, 0, 0, 0)),
        ] + [col(n_ctx, c) for c in cols] + [col(n_lat, c) for c in cols],
        out_specs=[
            pl.BlockSpec((1, n_ctx, HEAD), lambda b, h: (b, 0, h)),
            pl.BlockSpec((1, n_lat, HEAD), lambda b, h: (b, 0, h)),
        ],
        out_shape=[
            jax.ShapeDtypeStruct((bsz, n_ctx, A_HEADS * HEAD), BF16),
            jax.ShapeDtypeStruct((bsz, n_lat, A_HEADS * HEAD), BF16),
        ],
        scratch_shapes=[
            pltpu.VMEM((2, t, HEAD), BF16),
            pltpu.VMEM((2, t, HEAD), F32),
            pltpu.VMEM((2, (t // CHUNK) * HEAD, HEAD), F32),
            pltpu.VMEM((2, (t // CHUNK) * 8, HEAD), F32),
            pltpu.VMEM((2, HEAD, HEAD), F32),
        ],
        compiler_params=_params(("arbitrary", "arbitrary")),
        name="hgrn2",
    )(lbl, lmat, mk, *([p_ctx] * 5), *([p_lat] * 5))


def _mlstm_constants():
    n = CHUNK
    a = np.arange(n)[:, None]
    b = np.arange(n)[None, :]
    lt, rhs, causal = [], [], []
    for reverse in (False, True):
        upto = (b >= a) if reverse else (b <= a)
        after = (b < a) if reverse else (b > a)
        lt.append(np.stack([upto, after]))
        m = after.T
        ext = np.concatenate([m, np.zeros((n, HEAD - n)), np.ones((n, HEAD))], axis=1)
        rhs.append(np.concatenate([ext, ext], axis=0))
        causal.append(upto)
    return (jnp.asarray(np.stack(lt), F32), jnp.asarray(np.stack(rhs), BF16),
            jnp.asarray(np.stack(causal), F32))


def _mlstm_kernel(lt_ref, rhs_ref, causal_ref, eye_ref, cw_ref, bias_ref,
                  qc_ref, kc_ref, vc_ref, oc_ref, zc_ref, gc_ref,
                  ql_ref, kl_ref, vl_ref, ol_ref, zl_ref, gl_ref,
                  outc_ref, outl_ref,
                  q_s, s_s, cols_s, u_s, sc_s, st_s):
    h = pl.program_id(1)
    n_ctx = qc_ref.shape[1]
    n_lat = ql_ref.shape[1]
    n_ctx_chunks = n_ctx // CHUNK
    n_chunks = (n_ctx + n_lat) // CHUNK
    row_id = lax.broadcasted_iota(jnp.int32, (CHUNK, 1), 0)
    lane = lax.broadcasted_iota(jnp.int32, (1, HEAD), 1)
    sub8 = lax.broadcasted_iota(jnp.int32, (8, 1), 0)
    gate_row = lax.broadcasted_iota(jnp.int32, (N_GATES, 1), 0)
    eye_c = (row_id == lax.broadcasted_iota(jnp.int32, (1, CHUNK), 1)).astype(F32)
    ones_b = jnp.ones((CHUNK, HEAD), BF16)

    def conv_silu(src, ci, n, w):
        s = pl.multiple_of(ci * CHUNK, CHUNK)
        cur = src[0, pl.ds(s, CHUNK), :]
        up = src[0, pl.ds(pl.multiple_of(jnp.maximum(s - 8, 0), 8), 8), :]
        dn = src[0, pl.ds(pl.multiple_of(jnp.minimum(s + CHUNK, n - 8), 8), 8), :]
        prev_row = jnp.where(s > 0, up[7:8, :], 0.0)
        next_row = jnp.where(s + CHUNK < n, dn[0:1, :], 0.0)
        prev = jnp.where(row_id == 0, prev_row, pltpu.roll(cur, 1, 0))
        nxt = jnp.where(row_id == CHUNK - 1, next_row, pltpu.roll(cur, CHUNK - 1, 0))
        return _silu(w[0:1, :] * prev + w[1:2, :] * cur + w[2:3, :] * nxt)

    def pick_row(a, idx):
        return jnp.sum(jnp.where(gate_row == idx, a, 0.0), axis=0, keepdims=True)

    def phase_a(q_ref, k_ref, v_ref, g_ref, off, n):
        par = _largest_divisor(n // CHUNK, (8, 4, 2, 1))

        def body(c, carry):
            chunks = []
            for u in range(par):
                ci = c * par + u
                rows = pl.ds(pl.multiple_of(ci * CHUNK, CHUNK), CHUNK)
                srows = pl.ds(pl.multiple_of(off + ci * CHUNK, CHUNK), CHUNK)
                gt = _dot3_nt(eye_ref[...], g_ref[0, rows, :]) + bias_ref[:, 0:CHUNK]
                qb = (conv_silu(q_ref, ci, n, cw_ref[0]) * (HEAD ** -0.5)).astype(BF16)
                k = conv_silu(k_ref, ci, n, cw_ref[1])
                q_s[srows, :] = qb
                vaug = jnp.concatenate([v_ref[0, rows, :].astype(BF16), ones_b], axis=1)
                chunks.append(dict(k=k, qk=_dot_nt(qb, k.astype(BF16)), vaug=vaug, gt=gt, srows=srows,
                                   chunk=off // CHUNK + ci))
            chains = []
            for ck in chunks:
                for d in range(2):
                    gi = 2 * B_HEADS * d + h
                    li_row = pick_row(ck["gt"], gi) * LOG2E
                    lf_row = _log_sigmoid(pick_row(ck["gt"], gi + B_HEADS)) * LOG2E
                    lhs = jnp.concatenate([lt_ref[d, 0] * lf_row,
                                           lt_ref[d, 1] * lf_row + eye_c * li_row], axis=0)
                    out = _dot(jnp.concatenate(_split2(lhs), axis=1), rhs_ref[d])
                    chains.append(dict(ck, d=d, li_row=li_row, out=out))
            for ch in chains:
                d, out = ch["d"], ch["out"]
                cum_rep = out[0:CHUNK, HEAD:]
                logw_rep = out[CHUNK:, HEAD:]
                log_d = jnp.where(causal_ref[d] > 0.0, out[0:CHUNK, 0:CHUNK] + ch["li_row"], NEG)
                a_loc = jnp.max(log_d, axis=1, keepdims=True)
                s_s[d, ch["srows"], :] = (ch["qk"] * jnp.exp2(log_d - a_loc)).astype(BF16)
                cols_s[d, ch["srows"], :] = jnp.where(lane < CHUNK, a_loc, cum_rep)
                total = cum_rep[CHUNK - 1:CHUNK, :] if d == 0 else cum_rep[0:1, :]
                m_loc = jnp.max(logw_rep, axis=0, keepdims=True)
                kw = (ch["k"] * jnp.exp2(logw_rep - m_loc)).astype(BF16)
                ch["u"] = _dot_tn(kw, ch["vaug"])
                sc_s[d, pl.ds(pl.multiple_of(ch["chunk"] * 8, 8), 8), :] = jnp.where(
                    sub8 == 0, total, jnp.where(sub8 == 1, m_loc, 0.0))
            for ch in chains:
                u_s[ch["d"], pl.ds(pl.multiple_of(ch["chunk"] * HEAD, HEAD), HEAD), :] = ch["u"].astype(BF16)
            return carry
        lax.fori_loop(0, n // CHUNK // par, body, 0)

    phase_a(qc_ref, kc_ref, vc_ref, gc_ref, 0, n_ctx)
    phase_a(ql_ref, kl_ref, vl_ref, gl_ref, n_ctx, n_lat)

    st_s[...] = jnp.zeros_like(st_s)
    steps = _largest_divisor(n_chunks, (4, 2, 1))
    for d in range(2):
        def phase_b(jj, m, d=d):
            for u in range(steps):
                chunk = _scan_row(jj * steps + u, d == 1, n_ctx_chunks, n_chunks) // CHUNK
                srow = pl.ds(pl.multiple_of(chunk * 8, 8), 8)
                urow = pl.ds(pl.multiple_of(chunk * HEAD, HEAD), HEAD)
                sc = sc_s[d, srow, :]
                total, m_loc = sc[0:1, :], sc[1:2, :]
                m_new = jnp.maximum(total + m, m_loc)
                state = st_s[d]
                inc = u_s[d, urow, :].astype(F32)
                u_s[d, urow, :] = state.astype(BF16)
                sc_s[d, srow, :] = jnp.where(sub8 == 2, m, sc)
                keep = jnp.exp2(total + m - m_new)
                gain = jnp.exp2(m_loc - m_new)
                st_s[d] = (jnp.concatenate([keep, keep], axis=1) * state
                           + jnp.concatenate([gain, gain], axis=1) * inc)
                m = m_new
            return m
        lax.fori_loop(0, n_chunks // steps, phase_b, jnp.zeros((1, HEAD), F32))

    def phase_c(v_ref, o_ref, z_ref, out_ref, off, n):
        par = _largest_divisor(n // CHUNK, (8, 4, 2, 1))

        def body(c, carry):
            chains = []
            for u in range(par):
                ci = c * par + u
                rows = pl.ds(pl.multiple_of(ci * CHUNK, CHUNK), CHUNK)
                srows = pl.ds(pl.multiple_of(off + ci * CHUNK, CHUNK), CHUNK)
                chunk = off // CHUNK + ci
                qb = q_s[srows, :]
                vaug = jnp.concatenate([v_ref[0, rows, :].astype(BF16), ones_b], axis=1)
                for d in range(2):
                    intra = _dot(s_s[d, srows, :], vaug)
                    inter = _dot(qb, u_s[d, pl.ds(pl.multiple_of(chunk * HEAD, HEAD), HEAD), :])
                    chains.append(dict(d=d, u=u, rows=rows, srows=srows, chunk=chunk, intra=intra, inter=inter))
            ys = [None] * par
            for ch in chains:
                d = ch["d"]
                cols = cols_s[d, ch["srows"], :]
                swapped = pltpu.roll(cols, CHUNK, 1)
                a_rep = jnp.where(lane < CHUNK, cols, swapped)
                cum_rep = jnp.where(lane < CHUNK, swapped, cols)
                m = sc_s[d, pl.ds(pl.multiple_of(ch["chunk"] * 8, 8), 8), :][2:3, :]
                inter_l = cum_rep + m
                m_t = jnp.maximum(a_rep, inter_l)
                wi = jnp.exp2(a_rep - m_t)
                we = jnp.exp2(inter_l - m_t)
                num = wi * ch["intra"][:, :HEAD] + we * ch["inter"][:, :HEAD]
                den = wi * ch["intra"][:, HEAD:] + we * ch["inter"][:, HEAD:]
                hd = num / jnp.maximum(jnp.abs(den), jnp.exp2(-m_t))
                ys[ch["u"]] = hd if ys[ch["u"]] is None else ys[ch["u"]] + hd
                if d == 1:
                    rows = ch["rows"]
                    hh = jax.nn.sigmoid(o_ref[0, rows, :]) * ys[ch["u"]]
                    ms = jnp.mean(hh * hh, axis=-1, keepdims=True)
                    out_ref[0, rows, :] = (hh * lax.rsqrt(ms + EPS) * _silu(z_ref[0, rows, :])).astype(out_ref.dtype)
            return carry
        lax.fori_loop(0, n // CHUNK // par, body, 0)

    phase_c(vc_ref, oc_ref, zc_ref, outc_ref, 0, n_ctx)
    phase_c(vl_ref, ol_ref, zl_ref, outl_ref, n_ctx, n_lat)


def _mlstm(p_ctx, p_lat, g_ctx, g_lat, conv_w, gate_bias, consts):
    _, bsz, n_ctx, _ = p_ctx.shape
    n_lat = p_lat.shape[2]
    t = n_ctx + n_lat
    n_chunks = t // CHUNK
    lt, rhs, causal = consts
    eye = jnp.eye(N_GATES, HEAD, dtype=BF16)
    cw = conv_w.reshape(3, 2, B_HEADS * HEAD).transpose(1, 0, 2)
    bias = jnp.broadcast_to(gate_bias.reshape(N_GATES, 1), (N_GATES, HEAD))

    def col(n, c0):
        return pl.BlockSpec((None, 1, n, HEAD), lambda b, h, c0=c0: (c0 + h, b, 0, 0))

    def gate_spec(n):
        return pl.BlockSpec((1, n, HEAD), lambda b, h: (b, 0, 0))

    cols = (COL_B_Q, COL_B_K, COL_B_V, COL_B_O, COL_B_Z)
    return pl.pallas_call(
        _mlstm_kernel,
        grid=(bsz, B_HEADS),
        in_specs=[
            pl.BlockSpec(lt.shape, lambda b, h: (0, 0, 0, 0)),
            pl.BlockSpec(rhs.shape, lambda b, h: (0, 0, 0)),
            pl.BlockSpec(causal.shape, lambda b, h: (0, 0, 0)),
            pl.BlockSpec(eye.shape, lambda b, h: (0, 0)),
            pl.BlockSpec((2, 3, HEAD), lambda b, h: (0, 0, h)),
            pl.BlockSpec((N_GATES, HEAD), lambda b, h: (0, 0)),
        ] + [col(n_ctx, c) for c in cols] + [gate_spec(n_ctx)]
          + [col(n_lat, c) for c in cols] + [gate_spec(n_lat)],
        out_specs=[
            pl.BlockSpec((1, n_ctx, HEAD), lambda b, h: (b, 0, h)),
            pl.BlockSpec((1, n_lat, HEAD), lambda b, h: (b, 0, h)),
        ],
        out_shape=[
            jax.ShapeDtypeStruct((bsz, n_ctx, B_HEADS * HEAD), BF16),
            jax.ShapeDtypeStruct((bsz, n_lat, B_HEADS * HEAD), BF16),
        ],
        scratch_shapes=[
            pltpu.VMEM((t, HEAD), BF16),
            pltpu.VMEM((2, t, CHUNK), BF16),
            pltpu.VMEM((2, t, HEAD), F32),
            pltpu.VMEM((2, n_chunks * HEAD, 2 * HEAD), BF16),
            pltpu.VMEM((2, n_chunks * 8, HEAD), F32),
            pltpu.VMEM((2, HEAD, 2 * HEAD), F32),
        ],
        compiler_params=_params(("arbitrary", "arbitrary")),
        name="mlstm",
    )(lt, rhs, causal, eye, cw, bias, *([p_ctx] * 5), g_ctx, *([p_lat] * 5), g_lat)


def _rope(x, cos, sin_lo, sin_hi):
    return x * cos + pltpu.roll(x, HEAD - 32, 1) * sin_lo + pltpu.roll(x, 32, 1) * sin_hi


def _softmax_attend(sink_col, scores, values):
    m = jnp.maximum(sink_col, jnp.max(scores, axis=1, keepdims=True))
    p = jnp.exp2(scores - m)
    den = jnp.exp2(sink_col - m) + jnp.sum(p, axis=1, keepdims=True)
    return _dot(p.astype(BF16), values) / den


def _sink_column(sink_ref, kv, rows_per_head):
    row = lax.broadcasted_iota(jnp.int32, (C_GROUP * rows_per_head, 1), 0)
    col = jnp.zeros((C_GROUP * rows_per_head, 1), F32)
    for g in range(C_GROUP):
        col = jnp.where(row // rows_per_head == g, sink_ref[kv * C_GROUP + g] * LOG2E, col)
    return col


def _window_masks():
    r = np.arange(ATTN_BLOCK)[:, None]
    c = np.arange(3 * ATTN_BLOCK)[None, :]
    return jnp.asarray(np.stack([np.abs(c - p * ATTN_BLOCK - r) <= WINDOW for p in range(3)]), F32)


def _attn_kernel(sink_ref, mask_ref, cos_ref, slo_ref, shi_ref, q_ref, k_ref, v_ref, kc_ref, vc_ref, z_ref,
                 o_ref, kr_s):
    kv = pl.program_id(1)
    i = pl.program_id(2)
    n = k_ref.shape[1]
    span = 3 * ATTN_BLOCK

    @pl.when(i == 0)
    def _():
        group = _largest_divisor(n // ATTN_BLOCK, (4, 2, 1))

        def body(c, carry):
            for u in range(group):
                rows = pl.ds(pl.multiple_of((c * group + u) * ATTN_BLOCK, ATTN_BLOCK), ATTN_BLOCK)
                kr_s[rows, :] = _rope(k_ref[0, rows, :], cos_ref[rows, :], slo_ref[rows, :],
                                      shi_ref[rows, :]).astype(BF16)
            return carry
        lax.fori_loop(0, n // ATTN_BLOCK // group, body, 0)

    scale = HEAD ** -0.5 * LOG2E
    blocks_per_step = q_ref.shape[1] // ATTN_BLOCK

    par = _largest_divisor(blocks_per_step, (4, 2, 1))

    def block_group(trip, carry):
        chains = []
        for u in range(par):
            sub = trip * par + u
            blk = i * blocks_per_step + sub
            local = pl.ds(pl.multiple_of(sub * ATTN_BLOCK, ATTN_BLOCK), ATTN_BLOCK)
            qrows = pl.ds(pl.multiple_of(blk * ATTN_BLOCK, ATTN_BLOCK), ATTN_BLOCK)
            cos = cos_ref[qrows, :]
            slo = slo_ref[qrows, :]
            shi = shi_ref[qrows, :]
            start = pl.multiple_of(jnp.clip((blk - 1) * ATTN_BLOCK, 0, n - span), ATTN_BLOCK)
            keys = jnp.concatenate([kr_s[pl.ds(start, span), :], kc_ref[0].astype(BF16)], axis=0)
            values = jnp.concatenate([v_ref[0, pl.ds(start, span), :].astype(BF16), vc_ref[0].astype(BF16)],
                                     axis=0)
            valid = mask_ref[(blk * ATTN_BLOCK - start) // ATTN_BLOCK] > 0.0
            for g in range(C_GROUP):
                qg = (_rope(q_ref[g, local, :], cos, slo, shi) * scale).astype(BF16)
                s = _dot_nt(qg, keys)
                s = jnp.concatenate([jnp.where(valid, s[:, :span], NEG), s[:, span:]], axis=1)
                chains.append((g, local, s, values))
        for g, local, s, values in chains:
            sink = jnp.full((ATTN_BLOCK, 1), sink_ref[kv * C_GROUP + g] * LOG2E, F32)
            out = _softmax_attend(sink, s, values)
            o_ref[0, local, g * HEAD:(g + 1) * HEAD] = (out * _silu(z_ref[g, local, :])).astype(o_ref.dtype)
        return carry

    lax.fori_loop(0, blocks_per_step // par, block_group, 0)


def _attention(p_ctx, p_lat, sink, rope):
    _, bsz, n_ctx, _ = p_ctx.shape
    n = p_lat.shape[2]
    cos, slo, shi = rope
    masks = _window_masks()
    qrows = ATTN_BLOCK * _largest_divisor(n // ATTN_BLOCK, (8, 4, 2, 1))
    gw = C_GROUP * HEAD
    gblk = C_GROUP

    full = pl.BlockSpec((n, HEAD), lambda b, kv, i: (0, 0))
    return pl.pallas_call(
        _attn_kernel,
        grid=(bsz, C_KV_HEADS, n // qrows),
        in_specs=[
            pl.BlockSpec(memory_space=pltpu.SMEM),
            pl.BlockSpec(masks.shape, lambda b, kv, i: (0, 0, 0)),
            full, full, full,
            pl.BlockSpec((C_GROUP, None, qrows, HEAD), lambda b, kv, i: (COL_C_Q // gblk + kv, b, i, 0)),
            pl.BlockSpec((None, 1, n, HEAD), lambda b, kv, i: (COL_C_K + kv, b, 0, 0)),
            pl.BlockSpec((None, 1, n, HEAD), lambda b, kv, i: (COL_C_V + kv, b, 0, 0)),
            pl.BlockSpec((None, 1, n_ctx, HEAD), lambda b, kv, i: (COL_C_K + kv, b, 0, 0)),
            pl.BlockSpec((None, 1, n_ctx, HEAD), lambda b, kv, i: (COL_C_V + kv, b, 0, 0)),
            pl.BlockSpec((C_GROUP, None, qrows, HEAD), lambda b, kv, i: (COL_C_Z // gblk + kv, b, i, 0)),
        ],
        out_specs=pl.BlockSpec((1, qrows, gw), lambda b, kv, i: (b, i, kv)),
        out_shape=jax.ShapeDtypeStruct((bsz, n, C_Q_HEADS * HEAD), BF16),
        scratch_shapes=[pltpu.VMEM((n, HEAD), BF16)],
        compiler_params=_params(("arbitrary", "arbitrary", "arbitrary")),
        name="window_attention",
    )(sink, masks, cos, slo, shi, p_lat, p_lat, p_lat, p_ctx, p_ctx, p_lat)


def _ctx_attn_kernel(sink_ref, q_ref, k_ref, v_ref, z_ref, o_ref):
    kv = pl.program_id(1)
    n_ctx = q_ref.shape[1]
    scale = HEAD ** -0.5 * LOG2E
    q4 = jnp.concatenate([q_ref[g] * scale for g in range(C_GROUP)],
                         axis=0).astype(BF16)
    s = _dot_nt(q4, k_ref[0].astype(BF16))
    out = _softmax_attend(_sink_column(sink_ref, kv, n_ctx), s, v_ref[0].astype(BF16))
    for g in range(C_GROUP):
        zg = z_ref[g]
        o_ref[0, :, g * HEAD:(g + 1) * HEAD] = (out[g * n_ctx:(g + 1) * n_ctx] * _silu(zg)).astype(o_ref.dtype)


def _context_attention(p_ctx, sink):
    _, bsz, n_ctx, _ = p_ctx.shape
    gw = C_GROUP * HEAD
    gblk = C_GROUP
    return pl.pallas_call(
        _ctx_attn_kernel,
        grid=(bsz, C_KV_HEADS),
        in_specs=[
            pl.BlockSpec(memory_space=pltpu.SMEM),
            pl.BlockSpec((C_GROUP, None, n_ctx, HEAD), lambda b, kv: (COL_C_Q // gblk + kv, b, 0, 0)),
            pl.BlockSpec((None, 1, n_ctx, HEAD), lambda b, kv: (COL_C_K + kv, b, 0, 0)),
            pl.BlockSpec((None, 1, n_ctx, HEAD), lambda b, kv: (COL_C_V + kv, b, 0, 0)),
            pl.BlockSpec((C_GROUP, None, n_ctx, HEAD), lambda b, kv: (COL_C_Z // gblk + kv, b, 0, 0)),
        ],
        out_specs=pl.BlockSpec((1, n_ctx, gw), lambda b, kv: (b, 0, kv)),
        out_shape=jax.ShapeDtypeStruct((bsz, n_ctx, C_Q_HEADS * HEAD), BF16),
        compiler_params=_params(("arbitrary", "arbitrary")),
        name="context_attention",
    )(sink, p_ctx, p_ctx, p_ctx, p_ctx)


def _rope_tables(n):
    axis_dim = HEAD // 2
    rows = n // GRID_W
    row = jnp.repeat(jnp.arange(rows, dtype=F32), GRID_W)
    col = jnp.tile(jnp.arange(GRID_W, dtype=F32), rows)
    inv_freq = ROPE_BASE ** (-jnp.arange(0, axis_dim, 2, dtype=F32) / axis_dim)
    ang_r = row[:, None] * inv_freq[None, :]
    ang_c = col[:, None] * inv_freq[None, :]
    zero = jnp.zeros_like(ang_r)
    cos = jnp.concatenate([jnp.cos(ang_r)] * 2 + [jnp.cos(ang_c)] * 2, axis=-1)
    sin_lo = jnp.concatenate([-jnp.sin(ang_r), zero, -jnp.sin(ang_c), zero], axis=-1)
    sin_hi = jnp.concatenate([zero, jnp.sin(ang_r), zero, jnp.sin(ang_c)], axis=-1)
    return cos, sin_lo, sin_hi


def _pick_tile(n, candidates):
    for c in candidates:
        if n % c == 0:
            return c
    raise ValueError(f"no tile for {n}")


def kernel(x, c, ctx, c_ctx, w_mod, b_mod, g_pre, g_post, w_in, hgrn_lb_logits, mlstm_conv_w,
           mlstm_gate_bias, attn_sink, w_out):
    bsz, n_lat, d = x.shape
    n_ctx = ctx.shape[1]
    depth = w_mod.shape[0]
    assert n_ctx % CHUNK == 0 and n_lat % ATTN_BLOCK == 0 and n_lat >= 3 * ATTN_BLOCK

    cond = jnp.concatenate([c, c_ctx[None, :]], axis=0)
    pad = (-cond.shape[0]) % 8
    cond = jnp.pad(cond, ((0, pad), (0, 0)))
    mod = _modulation(cond, w_mod, b_mod)

    w_t = _weight_prep(jnp.swapaxes(w_in, 1, 2))
    wo = w_out.astype(BF16)
    consts = _scan_constants()
    mlstm_consts = _mlstm_constants()
    rope = _rope_tables(n_lat)

    tm_lat = _pick_tile(n_lat, (1024, 512, 256, 128))
    tm_ctx = _pick_tile(bsz * n_ctx, (1024, 512, 256, 128))
    tn = _pick_tile(MAIN_WIDTH, (1536, 512))
    to_lat = _pick_tile(n_lat, (512, 256, 128))
    to_ctx = _pick_tile(bsz * n_ctx, (512, 256, 128))

    x2d = x.reshape(bsz * n_lat, d)
    h2d = ctx.reshape(bsz * n_ctx, d)
    for layer in range(depth):
        with_ctx = layer < depth - 1
        shift_l = mod[layer, :bsz, 0:d].reshape(bsz, 1, d)
        scale_l = mod[layer, :bsz, d:2 * d].reshape(bsz, 1, d)
        gate_l = mod[layer, :bsz, 2 * d:].reshape(bsz, 1, d)
        shift_c = mod[layer, bsz:bsz + 1, 0:d].reshape(1, 1, d)
        scale_c = mod[layer, bsz:bsz + 1, d:2 * d].reshape(1, 1, d)
        gate_c = mod[layer, bsz:bsz + 1, 2 * d:].reshape(1, 1, d)

        gain_pre = g_pre[layer].reshape(1, d)
        gain_post = g_post[layer].reshape(1, d)

        p_lat, g_lat = _in_projection(x2d, shift_l, scale_l, gain_pre, w_t, layer, n_lat, tm_lat, tn)
        p_ctx, g_ctx = _in_projection(h2d, shift_c, scale_c, gain_pre, w_t, layer, bsz * n_ctx, tm_ctx, tn)
        p_lat = p_lat.reshape(MAIN_BLOCKS, bsz, n_lat, HEAD)
        p_ctx = p_ctx.reshape(MAIN_BLOCKS, bsz, n_ctx, HEAD)
        g_lat = g_lat.reshape(bsz, n_lat, HEAD)
        g_ctx = g_ctx.reshape(bsz, n_ctx, HEAD)

        a_c, a_l = _hgrn2(p_ctx, p_lat, hgrn_lb_logits, consts, layer)
        b_c, b_l = _mlstm(p_ctx, p_lat, g_ctx, g_lat, mlstm_conv_w[layer], mlstm_gate_bias[layer],
                          mlstm_consts)
        c_l = _attention(p_ctx, p_lat, attn_sink[layer], rope)

        x2d = _out_projection(a_l.reshape(bsz * n_lat, -1), b_l.reshape(bsz * n_lat, -1),
                              c_l.reshape(bsz * n_lat, -1), wo, layer, x2d, gate_l, gain_post, n_lat, to_lat)
        if with_ctx:
            c_c = _context_attention(p_ctx, attn_sink[layer])
            h2d = _out_projection(a_c.reshape(bsz * n_ctx, -1), b_c.reshape(bsz * n_ctx, -1),
                                  c_c.reshape(bsz * n_ctx, -1), wo, layer, h2d, gate_c, gain_post,
                                  bsz * n_ctx, to_ctx)
    return x2d.reshape(bsz, n_lat, d)
```

```python
import functools

import numpy as np
import jax
import jax.numpy as jnp
from jax import lax
from jax.experimental import pallas as pl
from jax.experimental.pallas import tpu as pltpu

F32 = jnp.float32
BF16 = jnp.bfloat16

EPS = 1e-6
NEG = -1e30
LOG2E = 1.4426950408889634
HEAD = 128
A_HEADS = 4
B_HEADS = 4
C_Q_HEADS = 8
C_KV_HEADS = 2
C_GROUP = C_Q_HEADS // C_KV_HEADS
CHUNK = 64
ATTN_BLOCK = 128
WINDOW = 128
GRID_W = 64
ROPE_BASE = 10000.0
N_GATES = 4 * B_HEADS

COL_A_Q, COL_A_FF, COL_A_FB, COL_A_I, COL_A_GATE = 0, 4, 8, 12, 16
COL_B_Q, COL_B_K, COL_B_V, COL_B_O, COL_B_Z = 20, 24, 28, 32, 36
COL_C_Q, COL_C_K, COL_C_V, COL_C_Z = 40, 48, 50, 52
MAIN_BLOCKS = 60
MAIN_WIDTH = MAIN_BLOCKS * HEAD
GATES_OFFSET = 9 * 4 * HEAD

VMEM_LIMIT = 56 * 1024 * 1024


def _params(sem):
    return pltpu.CompilerParams(dimension_semantics=sem, vmem_limit_bytes=VMEM_LIMIT)


def _silu(x):
    return x * jax.nn.sigmoid(x)


def _log_sigmoid(x):
    return jnp.minimum(x, 0.0) - jnp.log(1.0 + jnp.exp(-jnp.abs(x)))


def _dot(a, b):
    return jnp.dot(a, b, preferred_element_type=F32)


def _dot_nt(a, b):
    return lax.dot_general(a, b, (((1,), (1,)), ((), ())), preferred_element_type=F32)


def _dot_tn(a, b):
    return lax.dot_general(a, b, (((0,), (0,)), ((), ())), preferred_element_type=F32)


def _split3(x):
    hi = x.astype(BF16)
    r = x - hi.astype(F32)
    mid = r.astype(BF16)
    lo = (r - mid.astype(F32)).astype(BF16)
    return hi, mid, lo


def _split2(x):
    hi = x.astype(BF16)
    return hi, (x - hi.astype(F32)).astype(BF16)


def _dot3_nt(w, x):
    hi, mid, lo = _split3(x)
    return _dot_nt(w, hi) + _dot_nt(w, mid) + _dot_nt(w, lo)


def _mod_kernel(c_ref, w_ref, b_ref, o_ref):
    a = _silu(c_ref[...])
    o_ref[0] = jnp.dot(a, w_ref[0], preferred_element_type=F32,
                       precision=lax.Precision.HIGHEST) + b_ref[0]


def _modulation(cond, w_mod, b_mod):
    depth, d, n = w_mod.shape
    rows = cond.shape[0]
    tn = _pick_tile(n, (1536, 768, 384, 128))
    assert n % tn == 0
    return pl.pallas_call(
        _mod_kernel,
        grid=(depth, n // tn),
        in_specs=[
            pl.BlockSpec((rows, d), lambda l, j: (0, 0)),
            pl.BlockSpec((1, d, tn), lambda l, j: (l, 0, j)),
            pl.BlockSpec((1, 1, tn), lambda l, j: (l, 0, j)),
        ],
        out_specs=pl.BlockSpec((1, rows, tn), lambda l, j: (l, 0, j)),
        out_shape=jax.ShapeDtypeStruct((depth, rows, n), F32),
        compiler_params=_params(("arbitrary", "arbitrary")),
        name="modulation",
    )(cond, w_mod, b_mod.reshape(depth, 1, n))


NORM_ROWS = 32
EPILOGUE_ROWS = 8


def _inproj_kernel(x_ref, shift_ref, scale_ref, g_ref, w_ref, wg_ref, p_ref, gates_ref, xn_ref, gm_ref):
    tm = x_ref.shape[0]

    @pl.when(pl.program_id(1) == 0)
    def _():
        gm_ref[...] = g_ref[...] * (1.0 + scale_ref[0])

        groups = _largest_divisor(tm // NORM_ROWS, (4, 2, 1))

        def body(r, carry):
            for u in range(groups):
                rows = pl.ds(pl.multiple_of((groups * r + u) * NORM_ROWS, NORM_ROWS), NORM_ROWS)
                xr = x_ref[rows, :]
                ms = jnp.mean(xr * xr, axis=-1, keepdims=True)
                xn_ref[rows, :] = (xr * (lax.rsqrt(ms + EPS) * gm_ref[...]) + shift_ref[0]).astype(BF16)
            return carry

        lax.fori_loop(0, tm // NORM_ROWS // groups, body, 0)
        gates_ref[...] = _dot_nt(xn_ref[...], wg_ref[...])

    res = _dot_nt(xn_ref[...], w_ref[...])
    for t in range(p_ref.shape[0]):
        p_ref[t] = res[:, t * HEAD:(t + 1) * HEAD]


PREP_ROWS = 512
MAIN_PREP_BLOCKS = MAIN_WIDTH // PREP_ROWS
assert GATES_OFFSET % PREP_ROWS == 0 and MAIN_WIDTH % PREP_ROWS == 0


def _weight_prep_kernel(w_ref, o_ref):
    c = pl.program_id(1)
    row = lax.broadcasted_iota(jnp.int32, (PREP_ROWS, 1), 0)
    keep = jnp.logical_or(c < MAIN_PREP_BLOCKS, row < N_GATES)
    o_ref[...] = jnp.where(keep, w_ref[0], 0.0).astype(BF16)


def _weight_prep(w_in_t):
    depth, n, d = w_in_t.shape

    def src_row(c):
        after = jnp.where(c * PREP_ROWS >= GATES_OFFSET, N_GATES, 0)
        return pl.multiple_of(jnp.where(c < MAIN_PREP_BLOCKS, c * PREP_ROWS + after, GATES_OFFSET), N_GATES)

    return pl.pallas_call(
        _weight_prep_kernel,
        grid=(depth, MAIN_PREP_BLOCKS + 1),
        in_specs=[pl.BlockSpec((pl.Element(1), pl.Element(PREP_ROWS), pl.Element(d)),
                               lambda l, c: (l, src_row(c), 0))],
        out_specs=pl.BlockSpec((None, PREP_ROWS, d), lambda l, c: (l, c, 0)),
        out_shape=jax.ShapeDtypeStruct((depth, MAIN_WIDTH + PREP_ROWS, d), BF16),
        compiler_params=_params(("arbitrary", "arbitrary")),
        name="weight_prep",
    )(w_in_t)


def _in_projection(x2d, shift, scale, gain, w_t, layer, rows_per_mod, tm, tn):
    r, d = x2d.shape
    n = MAIN_WIDTH
    assert r % tm == 0 and n % tn == 0 and rows_per_mod % tm == 0
    per = rows_per_mod // tm
    return pl.pallas_call(
        _inproj_kernel,
        grid=(r // tm, n // tn),
        in_specs=[
            pl.BlockSpec((tm, d), lambda i, j: (i, 0)),
            pl.BlockSpec((1, 1, d), lambda i, j: (i // per, 0, 0)),
            pl.BlockSpec((1, 1, d), lambda i, j: (i // per, 0, 0)),
            pl.BlockSpec((1, d), lambda i, j: (0, 0)),
            pl.BlockSpec((None, tn, d), lambda i, j: (layer, j, 0)),
            pl.BlockSpec((None, HEAD, d), lambda i, j: (layer, MAIN_BLOCKS, 0)),
        ],
        out_specs=[
            pl.BlockSpec((tn // HEAD, tm, HEAD), lambda i, j: (j, i, 0)),
            pl.BlockSpec((tm, HEAD), lambda i, j: (i, 0)),
        ],
        out_shape=[
            jax.ShapeDtypeStruct((n // HEAD, r, HEAD), F32),
            jax.ShapeDtypeStruct((r, HEAD), F32),
        ],
        scratch_shapes=[pltpu.VMEM((tm, d), BF16), pltpu.VMEM((1, d), F32)],
        compiler_params=_params(("arbitrary", "arbitrary")),
        name="in_projection",
    )(x2d, shift, scale, gain, w_t, w_t)


def _outproj_kernel(a_ref, b_ref, c_ref, w_ref, x_ref, gate_ref, g_ref, o_ref, y_s):
    i = pl.program_id(0)
    tm = x_ref.shape[0]

    @pl.when(i == 0)
    def _():
        y_s[1] = jnp.zeros(y_s.shape[1:], F32)

    def step(cur, prev):
        for r in range(tm // EPILOGUE_ROWS):
            rows = slice(r * EPILOGUE_ROWS, (r + 1) * EPILOGUE_ROWS)
            y = y_s[prev, rows, :]
            ms = jnp.mean(y * y, axis=-1, keepdims=True)
            o_ref[rows, :] = x_ref[rows, :] + y * (lax.rsqrt(ms + EPS) * (gate_ref[0] * g_ref[...]))
        y_s[cur] = _dot(jnp.concatenate([a_ref[...], b_ref[...], c_ref[...]], axis=1), w_ref[...])

    pl.when(i % 2 == 0)(lambda: step(0, 1))
    pl.when(i % 2 == 1)(lambda: step(1, 0))


def _out_projection(a, b, c, w_out, layer, x2d, gate, gain, rows_per_mod, tm):
    r, d = x2d.shape
    assert r % tm == 0 and rows_per_mod % tm == 0
    per = rows_per_mod // tm
    n = r // tm
    cur = lambda i: (jnp.minimum(i, n - 1), 0)
    prev = lambda i: (jnp.maximum(i - 1, 0), 0)
    return pl.pallas_call(
        _outproj_kernel,
        grid=(n + 1,),
        in_specs=[
            pl.BlockSpec((tm, a.shape[1]), cur),
            pl.BlockSpec((tm, b.shape[1]), cur),
            pl.BlockSpec((tm, c.shape[1]), cur),
            pl.BlockSpec((None,) + w_out.shape[1:], lambda i: (layer, 0, 0)),
            pl.BlockSpec((tm, d), prev),
            pl.BlockSpec((1, 1, d), lambda i: (jnp.maximum(i - 1, 0) // per, 0, 0)),
            pl.BlockSpec((1, d), lambda i: (0, 0)),
        ],
        out_specs=pl.BlockSpec((tm, d), prev),
        out_shape=jax.ShapeDtypeStruct((r, d), F32),
        scratch_shapes=[pltpu.VMEM((2, tm, d), F32)],
        compiler_params=_params(("arbitrary",)),
        name="out_projection",
    )(a, b, c, w_out, x2d, gate, gain)


N_LEVELS = 7


def _scan_matrices(reverse):
    n = CHUNK
    t = np.arange(n)[:, None]
    u = np.arange(n)[None, :]
    cum = (u >= t) if reverse else (u <= t)
    masks = [t == u]
    size = 2
    while size <= n:
        half = size // 2
        same = (t // size) == (u // size)
        if reverse:
            masks.append(same & ((t % size) < half) & ((u % size) >= half))
        else:
            masks.append(same & ((t % size) >= half) & ((u % size) < half))
        size *= 2
    return cum.astype(np.float32), np.stack(masks).astype(np.float32)


def _scan_constants():
    cf, mf = _scan_matrices(False)
    cb, mb = _scan_matrices(True)
    return (jnp.asarray(np.stack([cf, cb]), BF16), jnp.asarray(np.stack([mf, mb]), F32))


def _largest_divisor(n, candidates):
    return next(c for c in candidates if n % c == 0)


def _scan_row(j, reverse, n_ctx_chunks, n_chunks):
    if not reverse:
        return j * CHUNK
    ctx_part = (n_ctx_chunks - 1 - j) * CHUNK
    lat_part = (n_ctx_chunks + (n_chunks - 1 - j)) * CHUNK
    return jnp.where(j < n_ctx_chunks, ctx_part, lat_part)


def _hgrn2_kernel(lbl_ref, lmat_ref, mk_ref,
                  qc_ref, ffc_ref, fbc_ref, ic_ref, gc_ref,
                  ql_ref, ffl_ref, fbl_ref, il_ref, gl_ref,
                  oc_ref, ol_ref,
                  qd_s, y_s, u_s, dec_s, st_s, *, layer):
    n_ctx = qc_ref.shape[1]
    n_lat = ql_ref.shape[1]
    n_ctx_chunks = n_ctx // CHUNK
    n_chunks = (n_ctx + n_lat) // CHUNK

    logits = lbl_ref[...]
    e = jnp.exp(logits - jnp.max(logits, axis=0, keepdims=True))
    sm = e / jnp.sum(e, axis=0, keepdims=True)
    lbs = jnp.sum(sm[:layer + 1], axis=0) - sm[0]

    row = lax.broadcasted_iota(jnp.int32, (CHUNK, 1), 0)
    r4 = row % 4

    def level_factor(size, d, f, cum):
        half = size // 2
        if size == 2:
            return jnp.where((row % 2 == 1) if d == 0 else (row % 2 == 0), f, 1.0)
        if size == 4:
            f_next = pltpu.roll(f, CHUNK - 1, 0)
            f_prev = pltpu.roll(f, 1, 0)
            if d == 0:
                return jnp.where(r4 == 0, f_next, jnp.where(r4 == 1, 1.0, jnp.where(r4 == 2, f, f * f_prev)))
            return jnp.where(r4 == 0, f * f_next, jnp.where(r4 == 1, f, jnp.where(r4 == 2, 1.0, f_prev)))
        mids = [b * size + (half - 1 if d == 0 else half) for b in range(CHUNK // size)]
        ref = jnp.concatenate([jnp.broadcast_to(cum[m:m + 1, :], (size, HEAD)) for m in mids], axis=0)
        return jnp.exp2(-jnp.abs(cum - ref))

    key_lane = lax.broadcasted_iota(jnp.int32, (1, CHUNK), 1)
    fine_levels = 4

    def assemble_scores(d, dots):
        groups = []
        for g in range(CHUNK // 8):
            r = slice(8 * g, 8 * g + 8)
            acc = mk_ref[d, 0, r, :] * dots[0][r]
            for lvl in range(1, fine_levels):
                acc = acc + mk_ref[d, lvl, r, :] * dots[lvl][r]
            for lvl in range(fine_levels, N_LEVELS):
                size = 2 ** lvl
                half = size // 2
                base = (8 * g // size) * size
                on_query_side = (8 * g - base >= half) if d == 0 else (8 * g - base < half)
                if on_query_side:
                    lo = base if d == 0 else base + half
                    acc = jnp.where((key_lane >= lo) & (key_lane < lo + half), dots[lvl][r], acc)
            groups.append(acc)
        return jnp.concatenate(groups, axis=0)

    def phase_a(q_ref, f_refs, i_ref, off, n):
        par = _largest_divisor(n // CHUNK, (16, 8, 4, 2, 1))

        def body(c, carry):
            chains = []
            for u in range(par):
                ci = c * par + u
                rows = pl.ds(pl.multiple_of(ci * CHUNK, CHUNK), CHUNK)
                srows = pl.ds(pl.multiple_of(off + ci * CHUNK, CHUNK), CHUNK)
                q = _silu(q_ref[0, rows, :])
                v = i_ref[0, rows, :].astype(BF16)
                for d in range(2):
                    lb = lbs[d]
                    open_part = (1.0 - lb) * jax.nn.sigmoid(f_refs[d][0, rows, :])
                    f = lb + open_part
                    k = (1.0 - lb) - open_part
                    logf = jnp.log(f) * LOG2E
                    res = _dot(lmat_ref[d], jnp.concatenate(_split2(logf), axis=1))
                    chains.append(dict(d=d, q=q, v=v, k=k, f=f, res=res, srows=srows,
                                       chunk=off // CHUNK + ci))
            for ch in chains:
                d, q, k = ch["d"], ch["q"], ch["k"]
                cum = ch["res"][:, :HEAD] + ch["res"][:, HEAD:]
                total = cum[CHUNK - 1:CHUNK, :] if d == 0 else cum[0:1, :]
                qb, kb = q.astype(BF16), k.astype(BF16)
                dots = [_dot_nt(qb, kb)]
                for lvl in range(1, N_LEVELS):
                    xb = level_factor(2 ** lvl, d, ch["f"], cum).astype(BF16)
                    dots.append(_dot_nt(qb * xb, kb * xb))
                ch["dots"] = dots
                qd_s[d, ch["srows"], :] = (q * jnp.exp2(cum)).astype(BF16)
                ch["u"] = _dot_tn(ch["v"], (k * jnp.exp2(total - cum)).astype(BF16))
                drow = pl.multiple_of(ch["chunk"] * 8, 8)
                dec_s[d, pl.ds(drow, 8), :] = jnp.broadcast_to(jnp.exp2(total), (8, HEAD))
            for ch in chains:
                d = ch["d"]
                ch["y"] = _dot(assemble_scores(d, ch["dots"]).astype(BF16), ch["v"])
                u_s[d, pl.ds(pl.multiple_of(ch["chunk"] * HEAD, HEAD), HEAD), :] = ch["u"]
            for ch in chains:
                y_s[ch["d"], ch["srows"], :] = ch["y"]
            return carry
        lax.fori_loop(0, n // CHUNK // par, body, 0)

    phase_a(qc_ref, (ffc_ref, fbc_ref), ic_ref, 0, n_ctx)
    phase_a(ql_ref, (ffl_ref, fbl_ref), il_ref, n_ctx, n_lat)

    st_s[...] = jnp.zeros_like(st_s)

    steps = _largest_divisor(n_chunks, (17, 16, 8, 4, 2, 1))

    def phase_b(jj, carry):
        states = [st_s[0], st_s[1]]
        for u in range(steps):
            j = jj * steps + u
            for d in range(2):
                chunk = _scan_row(j, d == 1, n_ctx_chunks, n_chunks) // CHUNK
                slot = pl.ds(pl.multiple_of(chunk * HEAD, HEAD), HEAD)
                dec = dec_s[d, pl.ds(pl.multiple_of(chunk * 8, 8), 1), :]
                increment = u_s[d, slot, :]
                u_s[d, slot, :] = states[d]
                states[d] = states[d] * dec + increment
        st_s[0] = states[0]
        st_s[1] = states[1]
        return carry

    lax.fori_loop(0, n_chunks // steps, phase_b, 0)

    def readout(gate_ref, o_ref, off, n):
        par = _largest_divisor(n // CHUNK, (16, 8, 4, 2, 1))

        def body(c, carry):
            chunks = []
            for u in range(par):
                ci = c * par + u
                rows = pl.ds(pl.multiple_of(ci * CHUNK, CHUNK), CHUNK)
                srows = pl.ds(pl.multiple_of(off + ci * CHUNK, CHUNK), CHUNK)
                slot = pl.ds(pl.multiple_of((off // CHUNK + ci) * HEAD, HEAD), HEAD)
                inter = [_dot_nt(qd_s[d, srows, :], u_s[d, slot, :].astype(BF16)) for d in range(2)]
                chunks.append((rows, srows, inter))
            for rows, srows, inter in chunks:
                y = (y_s[0, srows, :] + inter[0]) + (y_s[1, srows, :] + inter[1])
                ms = jnp.mean(y * y, axis=-1, keepdims=True)
                o_ref[0, rows, :] = (y * lax.rsqrt(ms + EPS) * _silu(gate_ref[0, rows, :])).astype(o_ref.dtype)
            return carry
        lax.fori_loop(0, n // CHUNK // par, body, 0)

    readout(gc_ref, oc_ref, 0, n_ctx)
    readout(gl_ref, ol_ref, n_ctx, n_lat)


def _hgrn2(p_ctx, p_lat, lb_logits, consts, layer):
    _, bsz, n_ctx, _ = p_ctx.shape
    n_lat = p_lat.shape[2]
    t = n_ctx + n_lat
    lmat, mk = consts
    depth = lb_logits.shape[0]
    lbl = lb_logits.reshape(depth, 2, A_HEADS, 1, HEAD).transpose(0, 1, 3, 2, 4).reshape(depth, 2, 1, A_HEADS * HEAD)

    def col(n, c0):
        return pl.BlockSpec((None, 1, n, HEAD), lambda b, h, c0=c0: (c0 + h, b, 0, 0))

    cols = (COL_A_Q, COL_A_FF, COL_A_FB, COL_A_I, COL_A_GATE)
    return pl.pallas_call(
        functools.partial(_hgrn2_kernel, layer=layer),
        grid=(bsz, A_HEADS),
        in_specs=[
            pl.BlockSpec((depth, 2, 1, HEAD), lambda b, h: (0, 0, 0, h)),
            pl.BlockSpec(lmat.shape, lambda b, h: (0, 0, 0)),
            pl.BlockSpec(mk.shape, lambda b, h: (0, 0, 0, 0)),
        ] + [col(n_ctx, c) for c in cols] + [col(n_lat, c) for c in cols],
        out_specs=[
            pl.BlockSpec((1, n_ctx, HEAD), lambda b, h: (b, 0, h)),
            pl.BlockSpec((1, n_lat, HEAD), lambda b, h: (b, 0, h)),
        ],
        out_shape=[
            jax.ShapeDtypeStruct((bsz, n_ctx, A_HEADS * HEAD), BF16),
            jax.ShapeDtypeStruct((bsz, n_lat, A_HEADS * HEAD), BF16),
        ],
        scratch_shapes=[
            pltpu.VMEM((2, t, HEAD), BF16),
            pltpu.VMEM((2, t, HEAD), F32),
            pltpu.VMEM((2, (t // CHUNK) * HEAD, HEAD), F32),
            pltpu.VMEM((2, (t // CHUNK) * 8, HEAD), F32),
            pltpu.VMEM((2, HEAD, HEAD), F32),
        ],
        compiler_params=_params(("arbitrary", "arbitrary")),
        name="hgrn2",
    )(lbl, lmat, mk, *([p_ctx] * 5), *([p_lat] * 5))


def _mlstm_constants():
    n = CHUNK
    a = np.arange(n)[:, None]
    b = np.arange(n)[None, :]
    lt, rhs, causal = [], [], []
    for reverse in (False, True):
        upto = (b >= a) if reverse else (b <= a)
        after = (b < a) if reverse else (b > a)
        lt.append(np.stack([upto, after]))
        m = after.T
        ext = np.concatenate([m, np.zeros((n, HEAD - n)), np.ones((n, HEAD))], axis=1)
        rhs.append(np.concatenate([ext, ext], axis=0))
        causal.append(upto)
    return (jnp.asarray(np.stack(lt), F32), jnp.asarray(np.stack(rhs), BF16),
            jnp.asarray(np.stack(causal), F32))


def _mlstm_kernel(lt_ref, rhs_ref, causal_ref, eye_ref, cw_ref, bias_ref,
                  qc_ref, kc_ref, vc_ref, oc_ref, zc_ref, gc_ref,
                  ql_ref, kl_ref, vl_ref, ol_ref, zl_ref, gl_ref,
                  outc_ref, outl_ref,
                  q_s, s_s, cols_s, u_s, sc_s, st_s):
    h = pl.program_id(1)
    n_ctx = qc_ref.shape[1]
    n_lat = ql_ref.shape[1]
    n_ctx_chunks = n_ctx // CHUNK
    n_chunks = (n_ctx + n_lat) // CHUNK
    row_id = lax.broadcasted_iota(jnp.int32, (CHUNK, 1), 0)
    lane = lax.broadcasted_iota(jnp.int32, (1, HEAD), 1)
    sub8 = lax.broadcasted_iota(jnp.int32, (8, 1), 0)
    gate_row = lax.broadcasted_iota(jnp.int32, (N_GATES, 1), 0)
    eye_c = (row_id == lax.broadcasted_iota(jnp.int32, (1, CHUNK), 1)).astype(F32)
    ones_b = jnp.ones((CHUNK, HEAD), BF16)

    def conv_silu(src, ci, n, w):
        s = pl.multiple_of(ci * CHUNK, CHUNK)
        cur = src[0, pl.ds(s, CHUNK), :]
        up = src[0, pl.ds(pl.multiple_of(jnp.maximum(s - 8, 0), 8), 8), :]
        dn = src[0, pl.ds(pl.multiple_of(jnp.minimum(s + CHUNK, n - 8), 8), 8), :]
        prev_row = jnp.where(s > 0, up[7:8, :], 0.0)
        next_row = jnp.where(s + CHUNK < n, dn[0:1, :], 0.0)
        prev = jnp.where(row_id == 0, prev_row, pltpu.roll(cur, 1, 0))
        nxt = jnp.where(row_id == CHUNK - 1, next_row, pltpu.roll(cur, CHUNK - 1, 0))
        return _silu(w[0:1, :] * prev + w[1:2, :] * cur + w[2:3, :] * nxt)

    def pick_row(a, idx):
        return jnp.sum(jnp.where(gate_row == idx, a, 0.0), axis=0, keepdims=True)

    def phase_a(q_ref, k_ref, v_ref, g_ref, off, n):
        par = _largest_divisor(n // CHUNK, (8, 4, 2, 1))

        def body(c, carry):
            chunks = []
            for u in range(par):
                ci = c * par + u
                rows = pl.ds(pl.multiple_of(ci * CHUNK, CHUNK), CHUNK)
                srows = pl.ds(pl.multiple_of(off + ci * CHUNK, CHUNK), CHUNK)
                gt = _dot3_nt(eye_ref[...], g_ref[0, rows, :]) + bias_ref[:, 0:CHUNK]
                qb = (conv_silu(q_ref, ci, n, cw_ref[0]) * (HEAD ** -0.5)).astype(BF16)
                k = conv_silu(k_ref, ci, n, cw_ref[1])
                q_s[srows, :] = qb
                vaug = jnp.concatenate([v_ref[0, rows, :].astype(BF16), ones_b], axis=1)
                chunks.append(dict(k=k, qk=_dot_nt(qb, k.astype(BF16)), vaug=vaug, gt=gt, srows=srows,
                                   chunk=off // CHUNK + ci))
            chains = []
            for ck in chunks:
                for d in range(2):
                    gi = 2 * B_HEADS * d + h
                    li_row = pick_row(ck["gt"], gi) * LOG2E
                    lf_row = _log_sigmoid(pick_row(ck["gt"], gi + B_HEADS)) * LOG2E
                    lhs = jnp.concatenate([lt_ref[d, 0] * lf_row,
                                           lt_ref[d, 1] * lf_row + eye_c * li_row], axis=0)
                    out = _dot(jnp.concatenate(_split2(lhs), axis=1), rhs_ref[d])
                    chains.append(dict(ck, d=d, li_row=li_row, out=out))
            for ch in chains:
                d, out = ch["d"], ch["out"]
                cum_rep = out[0:CHUNK, HEAD:]
                logw_rep = out[CHUNK:, HEAD:]
                log_d = jnp.where(causal_ref[d] > 0.0, out[0:CHUNK, 0:CHUNK] + ch["li_row"], NEG)
                a_loc = jnp.max(log_d, axis=1, keepdims=True)
                s_s[d, ch["srows"], :] = (ch["qk"] * jnp.exp2(log_d - a_loc)).astype(BF16)
                cols_s[d, ch["srows"], :] = jnp.where(lane < CHUNK, a_loc, cum_rep)
                total = cum_rep[CHUNK - 1:CHUNK, :] if d == 0 else cum_rep[0:1, :]
                m_loc = jnp.max(logw_rep, axis=0, keepdims=True)
                kw = (ch["k"] * jnp.exp2(logw_rep - m_loc)).astype(BF16)
                ch["u"] = _dot_tn(kw, ch["vaug"])
                sc_s[d, pl.ds(pl.multiple_of(ch["chunk"] * 8, 8), 8), :] = jnp.where(
                    sub8 == 0, total, jnp.where(sub8 == 1, m_loc, 0.0))
            for ch in chains:
                u_s[ch["d"], pl.ds(pl.multiple_of(ch["chunk"] * HEAD, HEAD), HEAD), :] = ch["u"].astype(BF16)
            return carry
        lax.fori_loop(0, n // CHUNK // par, body, 0)

    phase_a(qc_ref, kc_ref, vc_ref, gc_ref, 0, n_ctx)
    phase_a(ql_ref, kl_ref, vl_ref, gl_ref, n_ctx, n_lat)

    st_s[...] = jnp.zeros_like(st_s)
    steps = _largest_divisor(n_chunks, (4, 2, 1))
    for d in range(2):
        def phase_b(jj, m, d=d):
            for u in range(steps):
                chunk = _scan_row(jj * steps + u, d == 1, n_ctx_chunks, n_chunks) // CHUNK
                srow = pl.ds(pl.multiple_of(chunk * 8, 8), 8)
                urow = pl.ds(pl.multiple_of(chunk * HEAD, HEAD), HEAD)
                sc = sc_s[d, srow, :]
                total, m_loc = sc[0:1, :], sc[1:2, :]
                m_new = jnp.maximum(total + m, m_loc)
                state = st_s[d]
                inc = u_s[d, urow, :].astype(F32)
                u_s[d, urow, :] = state.astype(BF16)
                sc_s[d, srow, :] = jnp.where(sub8 == 2, m, sc)
                keep = jnp.exp2(total + m - m_new)
                gain = jnp.exp2(m_loc - m_new)
                st_s[d] = (jnp.concatenate([keep, keep], axis=1) * state
                           + jnp.concatenate([gain, gain], axis=1) * inc)
                m = m_new
            return m
        lax.fori_loop(0, n_chunks // steps, phase_b, jnp.zeros((1, HEAD), F32))

    def phase_c(v_ref, o_ref, z_ref, out_ref, off, n):
        par = _largest_divisor(n // CHUNK, (8, 4, 2, 1))

        def body(c, carry):
            chains = []
            for u in range(par):
                ci = c * par + u
                rows = pl.ds(pl.multiple_of(ci * CHUNK, CHUNK), CHUNK)
                srows = pl.ds(pl.multiple_of(off + ci * CHUNK, CHUNK), CHUNK)
                chunk = off // CHUNK + ci
                qb = q_s[srows, :]
                vaug = jnp.concatenate([v_ref[0, rows, :].astype(BF16), ones_b], axis=1)
                for d in range(2):
                    intra = _dot(s_s[d, srows, :], vaug)
                    inter = _dot(qb, u_s[d, pl.ds(pl.multiple_of(chunk * HEAD, HEAD), HEAD), :])
                    chains.append(dict(d=d, u=u, rows=rows, srows=srows, chunk=chunk, intra=intra, inter=inter))
            ys = [None] * par
            for ch in chains:
                d = ch["d"]
                cols = cols_s[d, ch["srows"], :]
                swapped = pltpu.roll(cols, CHUNK, 1)
                a_rep = jnp.where(lane < CHUNK, cols, swapped)
                cum_rep = jnp.where(lane < CHUNK, swapped, cols)
                m = sc_s[d, pl.ds(pl.multiple_of(ch["chunk"] * 8, 8), 8), :][2:3, :]
                inter_l = cum_rep + m
                m_t = jnp.maximum(a_rep, inter_l)
                wi = jnp.exp2(a_rep - m_t)
                we = jnp.exp2(inter_l - m_t)
                num = wi * ch["intra"][:, :HEAD] + we * ch["inter"][:, :HEAD]
                den = wi * ch["intra"][:, HEAD:] + we * ch["inter"][:, HEAD:]
                hd = num / jnp.maximum(jnp.abs(den), jnp.exp2(-m_t))
                ys[ch["u"]] = hd if ys[ch["u"]] is None else ys[ch["u"]] + hd
                if d == 1:
                    rows = ch["rows"]
                    hh = jax.nn.sigmoid(o_ref[0, rows, :]) * ys[ch["u"]]
                    ms = jnp.mean(hh * hh, axis=-1, keepdims=True)
                    out_ref[0, rows, :] = (hh * lax.rsqrt(ms + EPS) * _silu(z_ref[0, rows, :])).astype(out_ref.dtype)
            return carry
        lax.fori_loop(0, n // CHUNK // par, body, 0)

    phase_c(vc_ref, oc_ref, zc_ref, outc_ref, 0, n_ctx)
    phase_c(vl_ref, ol_ref, zl_ref, outl_ref, n_ctx, n_lat)


def _mlstm(p_ctx, p_lat, g_ctx, g_lat, conv_w, gate_bias, consts):
    _, bsz, n_ctx, _ = p_ctx.shape
    n_lat = p_lat.shape[2]
    t = n_ctx + n_lat
    n_chunks = t // CHUNK
    lt, rhs, causal = consts
    eye = jnp.eye(N_GATES, HEAD, dtype=BF16)
    cw = conv_w.reshape(3, 2, B_HEADS * HEAD).transpose(1, 0, 2)
    bias = jnp.broadcast_to(gate_bias.reshape(N_GATES, 1), (N_GATES, HEAD))

    def col(n, c0):
        return pl.BlockSpec((None, 1, n, HEAD), lambda b, h, c0=c0: (c0 + h, b, 0, 0))

    def gate_spec(n):
        return pl.BlockSpec((1, n, HEAD), lambda b, h: (b, 0, 0))

    cols = (COL_B_Q, COL_B_K, COL_B_V, COL_B_O, COL_B_Z)
    return pl.pallas_call(
        _mlstm_kernel,
        grid=(bsz, B_HEADS),
        in_specs=[
            pl.BlockSpec(lt.shape, lambda b, h: (0, 0, 0, 0)),
            pl.BlockSpec(rhs.shape, lambda b, h: (0, 0, 0)),
            pl.BlockSpec(causal.shape, lambda b, h: (0, 0, 0)),
            pl.BlockSpec(eye.shape, lambda b, h: (0, 0)),
            pl.BlockSpec((2, 3, HEAD), lambda b, h: (0, 0, h)),
            pl.BlockSpec((N_GATES, HEAD), lambda b, h: (0, 0)),
        ] + [col(n_ctx, c) for c in cols] + [gate_spec(n_ctx)]
          + [col(n_lat, c) for c in cols] + [gate_spec(n_lat)],
        out_specs=[
            pl.BlockSpec((1, n_ctx, HEAD), lambda b, h: (b, 0, h)),
            pl.BlockSpec((1, n_lat, HEAD), lambda b, h: (b, 0, h)),
        ],
        out_shape=[
            jax.ShapeDtypeStruct((bsz, n_ctx, B_HEADS * HEAD), BF16),
            jax.ShapeDtypeStruct((bsz, n_lat, B_HEADS * HEAD), BF16),
        ],
        scratch_shapes=[
            pltpu.VMEM((t, HEAD), BF16),
            pltpu.VMEM((2, t, CHUNK), BF16),
            pltpu.VMEM((2, t, HEAD), F32),
            pltpu.VMEM((2, n_chunks * HEAD, 2 * HEAD), BF16),
            pltpu.VMEM((2, n_chunks * 8, HEAD), F32),
            pltpu.VMEM((2, HEAD, 2 * HEAD), F32),
        ],
        compiler_params=_params(("arbitrary", "arbitrary")),
        name="mlstm",
    )(lt, rhs, causal, eye, cw, bias, *([p_ctx] * 5), g_ctx, *([p_lat] * 5), g_lat)


def _rope(x, cos, sin_lo, sin_hi):
    return x * cos + pltpu.roll(x, HEAD - 32, 1) * sin_lo + pltpu.roll(x, 32, 1) * sin_hi


def _softmax_attend(sink_col, scores, values):
    m = jnp.maximum(sink_col, jnp.max(scores, axis=1, keepdims=True))
    p = jnp.exp2(scores - m)
    den = jnp.exp2(sink_col - m) + jnp.sum(p, axis=1, keepdims=True)
    return _dot(p.astype(BF16), values) / den


def _sink_column(sink_ref, kv, rows_per_head):
    row = lax.broadcasted_iota(jnp.int32, (C_GROUP * rows_per_head, 1), 0)
    col = jnp.zeros((C_GROUP * rows_per_head, 1), F32)
    for g in range(C_GROUP):
        col = jnp.where(row // rows_per_head == g, sink_ref[kv * C_GROUP + g] * LOG2E, col)
    return col


def _window_masks():
    r = np.arange(ATTN_BLOCK)[:, None]
    c = np.arange(3 * ATTN_BLOCK)[None, :]
    return jnp.asarray(np.stack([np.abs(c - p * ATTN_BLOCK - r) <= WINDOW for p in range(3)]), F32)


def _attn_kernel(sink_ref, mask_ref, cos_ref, slo_ref, shi_ref, q_ref, k_ref, v_ref, kc_ref, vc_ref, z_ref,
                 o_ref, kr_s):
    kv = pl.program_id(1)
    i = pl.program_id(2)
    n = k_ref.shape[1]
    span = 3 * ATTN_BLOCK

    @pl.when(i == 0)
    def _():
        group = _largest_divisor(n // ATTN_BLOCK, (4, 2, 1))

        def body(c, carry):
            for u in range(group):
                rows = pl.ds(pl.multiple_of((c * group + u) * ATTN_BLOCK, ATTN_BLOCK), ATTN_BLOCK)
                kr_s[rows, :] = _rope(k_ref[0, rows, :], cos_ref[rows, :], slo_ref[rows, :],
                                      shi_ref[rows, :]).astype(BF16)
            return carry
        lax.fori_loop(0, n // ATTN_BLOCK // group, body, 0)

    scale = HEAD ** -0.5 * LOG2E
    blocks_per_step = q_ref.shape[1] // ATTN_BLOCK

    par = _largest_divisor(blocks_per_step, (4, 2, 1))

    def block_group(trip, carry):
        chains = []
        for u in range(par):
            sub = trip * par + u
            blk = i * blocks_per_step + sub
            local = pl.ds(pl.multiple_of(sub * ATTN_BLOCK, ATTN_BLOCK), ATTN_BLOCK)
            qrows = pl.ds(pl.multiple_of(blk * ATTN_BLOCK, ATTN_BLOCK), ATTN_BLOCK)
            cos = cos_ref[qrows, :]
            slo = slo_ref[qrows, :]
            shi = shi_ref[qrows, :]
            start = pl.multiple_of(jnp.clip((blk - 1) * ATTN_BLOCK, 0, n - span), ATTN_BLOCK)
            keys = jnp.concatenate([kr_s[pl.ds(start, span), :], kc_ref[0].astype(BF16)], axis=0)
            values = jnp.concatenate([v_ref[0, pl.ds(start, span), :].astype(BF16), vc_ref[0].astype(BF16)],
                                     axis=0)
            valid = mask_ref[(blk * ATTN_BLOCK - start) // ATTN_BLOCK] > 0.0
            for g in range(C_GROUP):
                qg = (_rope(q_ref[g, local, :], cos, slo, shi) * scale).astype(BF16)
                s = _dot_nt(qg, keys)
                s = jnp.concatenate([jnp.where(valid, s[:, :span], NEG), s[:, span:]], axis=1)
                chains.append((g, local, s, values))
        for g, local, s, values in chains:
            sink = jnp.full((ATTN_BLOCK, 1), sink_ref[kv * C_GROUP + g] * LOG2E, F32)
            out = _softmax_attend(sink, s, values)
            o_ref[0, local, g * HEAD:(g + 1) * HEAD] = (out * _silu(z_ref[g, local, :])).astype(o_ref.dtype)
        return carry

    lax.fori_loop(0, blocks_per_step // par, block_group, 0)


def _attention(p_ctx, p_lat, sink, rope):
    _, bsz, n_ctx, _ = p_ctx.shape
    n = p_lat.shape[2]
    cos, slo, shi = rope
    masks = _window_masks()
    qrows = ATTN_BLOCK * _largest_divisor(n // ATTN_BLOCK, (8, 4, 2, 1))
    gw = C_GROUP * HEAD
    gblk = C_GROUP

    full = pl.BlockSpec((n, HEAD), lambda b, kv, i: (0, 0))
    return pl.pallas_call(
        _attn_kernel,
        grid=(bsz, C_KV_HEADS, n // qrows),
        in_specs=[
            pl.BlockSpec(memory_space=pltpu.SMEM),
            pl.BlockSpec(masks.shape, lambda b, kv, i: (0, 0, 0)),
            full, full, full,
            pl.BlockSpec((C_GROUP, None, qrows, HEAD), lambda b, kv, i: (COL_C_Q // gblk + kv, b, i, 0)),
            pl.BlockSpec((None, 1, n, HEAD), lambda b, kv, i: (COL_C_K + kv, b, 0, 0)),
            pl.BlockSpec((None, 1, n, HEAD), lambda b, kv, i: (COL_C_V + kv, b, 0, 0)),
            pl.BlockSpec((None, 1, n_ctx, HEAD), lambda b, kv, i: (COL_C_K + kv, b, 0, 0)),
            pl.BlockSpec((None, 1, n_ctx, HEAD), lambda b, kv, i: (COL_C_V + kv, b, 0, 0)),
            pl.BlockSpec((C_GROUP, None, qrows, HEAD), lambda b, kv, i: (COL_C_Z // gblk + kv, b, i, 0)),
        ],
        out_specs=pl.BlockSpec((1, qrows, gw), lambda b, kv, i: (b, i, kv)),
        out_shape=jax.ShapeDtypeStruct((bsz, n, C_Q_HEADS * HEAD), BF16),
        scratch_shapes=[pltpu.VMEM((n, HEAD), BF16)],
        compiler_params=_params(("arbitrary", "arbitrary", "arbitrary")),
        name="window_attention",
    )(sink, masks, cos, slo, shi, p_lat, p_lat, p_lat, p_ctx, p_ctx, p_lat)


def _ctx_attn_kernel(sink_ref, q_ref, k_ref, v_ref, z_ref, o_ref):
    kv = pl.program_id(1)
    n_ctx = q_ref.shape[1]
    scale = HEAD ** -0.5 * LOG2E
    q4 = jnp.concatenate([q_ref[g] * scale for g in range(C_GROUP)],
                         axis=0).astype(BF16)
    s = _dot_nt(q4, k_ref[0].astype(BF16))
    out = _softmax_attend(_sink_column(sink_ref, kv, n_ctx), s, v_ref[0].astype(BF16))
    for g in range(C_GROUP):
        zg = z_ref[g]
        o_ref[0, :, g * HEAD:(g + 1) * HEAD] = (out[g * n_ctx:(g + 1) * n_ctx] * _silu(zg)).astype(o_ref.dtype)


def _context_attention(p_ctx, sink):
    _, bsz, n_ctx, _ = p_ctx.shape
    gw = C_GROUP * HEAD
    gblk = C_GROUP
    return pl.pallas_call(
        _ctx_attn_kernel,
        grid=(bsz, C_KV_HEADS),
        in_specs=[
            pl.BlockSpec(memory_space=pltpu.SMEM),
            pl.BlockSpec((C_GROUP, None, n_ctx, HEAD), lambda b, kv: (COL_C_Q // gblk + kv, b, 0, 0)),
            pl.BlockSpec((None, 1, n_ctx, HEAD), lambda b, kv: (COL_C_K + kv, b, 0, 0)),
            pl.BlockSpec((None, 1, n_ctx, HEAD), lambda b, kv: (COL_C_V + kv, b, 0, 0)),
            pl.BlockSpec((C_GROUP, None, n_ctx, HEAD), lambda b, kv: (COL_C_Z // gblk + kv, b, 0, 0)),
        ],
        out_specs=pl.BlockSpec((1, n_ctx, gw), lambda b, kv: (b, 0, kv)),
        out_shape=jax.ShapeDtypeStruct((bsz, n_ctx, C_Q_HEADS * HEAD), BF16),
        compiler_params=_params(("arbitrary", "arbitrary")),
        name="context_attention",
    )(sink, p_ctx, p_ctx, p_ctx, p_ctx)


def _rope_tables(n):
    axis_dim = HEAD // 2
    rows = n // GRID_W
    inv_freq = ROPE_BASE ** (-jnp.arange(0, axis_dim, 2, dtype=F32) / axis_dim)
    ang_r = jnp.arange(rows, dtype=F32)[:, None] * inv_freq[None, :]
    ang_c = jnp.arange(GRID_W, dtype=F32)[:, None] * inv_freq[None, :]
    by_row = lambda a: jnp.repeat(a, GRID_W, axis=0)
    by_col = lambda a: jnp.tile(a, (rows, 1))
    cos_r, sin_r = by_row(jnp.cos(ang_r)), by_row(jnp.sin(ang_r))
    cos_c, sin_c = by_col(jnp.cos(ang_c)), by_col(jnp.sin(ang_c))
    zero = jnp.zeros_like(cos_r)
    cos = jnp.concatenate([cos_r, cos_r, cos_c, cos_c], axis=-1)
    sin_lo = jnp.concatenate([-sin_r, zero, -sin_c, zero], axis=-1)
    sin_hi = jnp.concatenate([zero, sin_r, zero, sin_c], axis=-1)
    return cos, sin_lo, sin_hi


def _pick_tile(n, candidates):
    for c in candidates:
        if n % c == 0:
            return c
    raise ValueError(f"no tile for {n}")


def kernel(x, c, ctx, c_ctx, w_mod, b_mod, g_pre, g_post, w_in, hgrn_lb_logits, mlstm_conv_w,
           mlstm_gate_bias, attn_sink, w_out):
    bsz, n_lat, d = x.shape
    n_ctx = ctx.shape[1]
    depth = w_mod.shape[0]
    assert n_ctx % CHUNK == 0 and n_lat % ATTN_BLOCK == 0 and n_lat >= 3 * ATTN_BLOCK

    cond = jnp.concatenate([c, c_ctx[None, :]], axis=0)
    pad = (-cond.shape[0]) % 8
    cond = jnp.pad(cond, ((0, pad), (0, 0)))
    mod = _modulation(cond, w_mod, b_mod)

    w_t = _weight_prep(jnp.swapaxes(w_in, 1, 2))
    wo = w_out.astype(BF16)
    consts = _scan_constants()
    mlstm_consts = _mlstm_constants()
    rope = _rope_tables(n_lat)

    tm_lat = _pick_tile(n_lat, (1024, 512, 256, 128))
    tm_ctx = _pick_tile(bsz * n_ctx, (1024, 512, 256, 128))
    tn = _pick_tile(MAIN_WIDTH, (1536, 512))
    to_lat = _pick_tile(n_lat, (512, 256, 128))
    to_ctx = _pick_tile(bsz * n_ctx, (512, 256, 128))

    x2d = x.reshape(bsz * n_lat, d)
    h2d = ctx.reshape(bsz * n_ctx, d)
    for layer in range(depth):
        with_ctx = layer < depth - 1
        shift_l = mod[layer, :bsz, 0:d].reshape(bsz, 1, d)
        scale_l = mod[layer, :bsz, d:2 * d].reshape(bsz, 1, d)
        gate_l = mod[layer, :bsz, 2 * d:].reshape(bsz, 1, d)
        shift_c = mod[layer, bsz:bsz + 1, 0:d].reshape(1, 1, d)
        scale_c = mod[layer, bsz:bsz + 1, d:2 * d].reshape(1, 1, d)
        gate_c = mod[layer, bsz:bsz + 1, 2 * d:].reshape(1, 1, d)

        gain_pre = g_pre[layer].reshape(1, d)
        gain_post = g_post[layer].reshape(1, d)

        p_lat, g_lat = _in_projection(x2d, shift_l, scale_l, gain_pre, w_t, layer, n_lat, tm_lat, tn)
        p_ctx, g_ctx = _in_projection(h2d, shift_c, scale_c, gain_pre, w_t, layer, bsz * n_ctx, tm_ctx, tn)
        p_lat = p_lat.reshape(MAIN_BLOCKS, bsz, n_lat, HEAD)
        p_ctx = p_ctx.reshape(MAIN_BLOCKS, bsz, n_ctx, HEAD)
        g_lat = g_lat.reshape(bsz, n_lat, HEAD)
        g_ctx = g_ctx.reshape(bsz, n_ctx, HEAD)

        a_c, a_l = _hgrn2(p_ctx, p_lat, hgrn_lb_logits, consts, layer)
        b_c, b_l = _mlstm(p_ctx, p_lat, g_ctx, g_lat, mlstm_conv_w[layer], mlstm_gate_bias[layer],
                          mlstm_consts)
        c_l = _attention(p_ctx, p_lat, attn_sink[layer], rope)

        x2d = _out_projection(a_l.reshape(bsz * n_lat, -1), b_l.reshape(bsz * n_lat, -1),
                              c_l.reshape(bsz * n_lat, -1), wo, layer, x2d, gate_l, gain_post, n_lat, to_lat)
        if with_ctx:
            c_c = _context_attention(p_ctx, attn_sink[layer])
            h2d = _out_projection(a_c.reshape(bsz * n_ctx, -1), b_c.reshape(bsz * n_ctx, -1),
                                  c_c.reshape(bsz * n_ctx, -1), wo, layer, h2d, gate_c, gain_post,
                                  bsz * n_ctx, to_ctx)
    return x2d.reshape(bsz, n_lat, d)
```

```python
import functools

import numpy as np
import jax
import jax.numpy as jnp
from jax import lax
from jax.experimental import pallas as pl
from jax.experimental.pallas import tpu as pltpu

F32 = jnp.float32
BF16 = jnp.bfloat16

EPS = 1e-6
NEG = -1e30
LOG2E = 1.4426950408889634
HEAD = 128
A_HEADS = 4
B_HEADS = 4
C_Q_HEADS = 8
C_KV_HEADS = 2
C_GROUP = C_Q_HEADS // C_KV_HEADS
CHUNK = 64
ATTN_BLOCK = 128
WINDOW = 128
GRID_W = 64
ROPE_BASE = 10000.0
N_GATES = 4 * B_HEADS

COL_A_Q, COL_A_FF, COL_A_FB, COL_A_I, COL_A_GATE = 0, 4, 8, 12, 16
COL_B_Q, COL_B_K, COL_B_V, COL_B_O, COL_B_Z = 20, 24, 28, 32, 36
COL_C_Q, COL_C_K, COL_C_V, COL_C_Z = 40, 48, 50, 52
MAIN_BLOCKS = 60
MAIN_WIDTH = MAIN_BLOCKS * HEAD
GATES_OFFSET = 9 * 4 * HEAD

VMEM_LIMIT = 56 * 1024 * 1024


def _params(sem):
    return pltpu.CompilerParams(dimension_semantics=sem, vmem_limit_bytes=VMEM_LIMIT)


def _silu(x):
    return x * jax.nn.sigmoid(x)


def _log_sigmoid(x):
    return jnp.minimum(x, 0.0) - jnp.log(1.0 + jnp.exp(-jnp.abs(x)))


def _dot(a, b):
    return jnp.dot(a, b, preferred_element_type=F32)


def _dot_nt(a, b):
    return lax.dot_general(a, b, (((1,), (1,)), ((), ())), preferred_element_type=F32)


def _dot_tn(a, b):
    return lax.dot_general(a, b, (((0,), (0,)), ((), ())), preferred_element_type=F32)


def _split3(x):
    hi = x.astype(BF16)
    r = x - hi.astype(F32)
    mid = r.astype(BF16)
    lo = (r - mid.astype(F32)).astype(BF16)
    return hi, mid, lo


def _split2(x):
    hi = x.astype(BF16)
    return hi, (x - hi.astype(F32)).astype(BF16)


def _dot3_nt(w, x):
    hi, mid, lo = _split3(x)
    return _dot_nt(w, hi) + _dot_nt(w, mid) + _dot_nt(w, lo)


def _mod_kernel(c_ref, w_ref, b_ref, o_ref):
    a = _silu(c_ref[...])
    o_ref[0] = jnp.dot(a, w_ref[0], preferred_element_type=F32,
                       precision=lax.Precision.HIGHEST) + b_ref[0]


def _modulation(cond, w_mod, b_mod):
    depth, d, n = w_mod.shape
    rows = cond.shape[0]
    tn = _pick_tile(n, (1536, 768, 384, 128))
    assert n % tn == 0
    return pl.pallas_call(
        _mod_kernel,
        grid=(depth, n // tn),
        in_specs=[
            pl.BlockSpec((rows, d), lambda l, j: (0, 0)),
            pl.BlockSpec((1, d, tn), lambda l, j: (l, 0, j)),
            pl.BlockSpec((1, 1, tn), lambda l, j: (l, 0, j)),
        ],
        out_specs=pl.BlockSpec((1, rows, tn), lambda l, j: (l, 0, j)),
        out_shape=jax.ShapeDtypeStruct((depth, rows, n), F32),
        compiler_params=_params(("arbitrary", "arbitrary")),
        name="modulation",
    )(cond, w_mod, b_mod.reshape(depth, 1, n))


NORM_ROWS = 32
EPILOGUE_ROWS = 8


def _inproj_kernel(x_ref, shift_ref, scale_ref, g_ref, w_ref, wg_ref, p_ref, gates_ref, xn_ref, gm_ref):
    tm = x_ref.shape[0]

    @pl.when(pl.program_id(1) == 0)
    def _():
        gm_ref[...] = g_ref[...] * (1.0 + scale_ref[0])

        groups = _largest_divisor(tm // NORM_ROWS, (4, 2, 1))

        def body(r, carry):
            for u in range(groups):
                rows = pl.ds(pl.multiple_of((groups * r + u) * NORM_ROWS, NORM_ROWS), NORM_ROWS)
                xr = x_ref[rows, :]
                ms = jnp.mean(xr * xr, axis=-1, keepdims=True)
                xn_ref[rows, :] = (xr * (lax.rsqrt(ms + EPS) * gm_ref[...]) + shift_ref[0]).astype(BF16)
            return carry

        lax.fori_loop(0, tm // NORM_ROWS // groups, body, 0)
        gates_ref[...] = _dot_nt(xn_ref[...], wg_ref[...])

    res = _dot_nt(xn_ref[...], w_ref[...])
    for t in range(p_ref.shape[0]):
        p_ref[t] = res[:, t * HEAD:(t + 1) * HEAD]


PREP_ROWS = 512
MAIN_PREP_BLOCKS = MAIN_WIDTH // PREP_ROWS
assert GATES_OFFSET % PREP_ROWS == 0 and MAIN_WIDTH % PREP_ROWS == 0


def _weight_prep_kernel(w_ref, o_ref):
    c = pl.program_id(1)
    row = lax.broadcasted_iota(jnp.int32, (PREP_ROWS, 1), 0)
    keep = jnp.logical_or(c < MAIN_PREP_BLOCKS, row < N_GATES)
    o_ref[...] = jnp.where(keep, w_ref[0], 0.0).astype(BF16)


def _weight_prep(w_in_t):
    depth, n, d = w_in_t.shape

    def src_row(c):
        after = jnp.where(c * PREP_ROWS >= GATES_OFFSET, N_GATES, 0)
        return pl.multiple_of(jnp.where(c < MAIN_PREP_BLOCKS, c * PREP_ROWS + after, GATES_OFFSET), N_GATES)

    return pl.pallas_call(
        _weight_prep_kernel,
        grid=(depth, MAIN_PREP_BLOCKS + 1),
        in_specs=[pl.BlockSpec((pl.Element(1), pl.Element(PREP_ROWS), pl.Element(d)),
                               lambda l, c: (l, src_row(c), 0))],
        out_specs=pl.BlockSpec((None, PREP_ROWS, d), lambda l, c: (l, c, 0)),
        out_shape=jax.ShapeDtypeStruct((depth, MAIN_WIDTH + PREP_ROWS, d), BF16),
        compiler_params=_params(("arbitrary", "arbitrary")),
        name="weight_prep",
    )(w_in_t)


def _in_projection(x2d, shift, scale, gain, w_t, layer, rows_per_mod, tm, tn):
    r, d = x2d.shape
    n = MAIN_WIDTH
    assert r % tm == 0 and n % tn == 0 and rows_per_mod % tm == 0
    per = rows_per_mod // tm
    return pl.pallas_call(
        _inproj_kernel,
        grid=(r // tm, n // tn),
        in_specs=[
            pl.BlockSpec((tm, d), lambda i, j: (i, 0)),
            pl.BlockSpec((1, 1, d), lambda i, j: (i // per, 0, 0)),
            pl.BlockSpec((1, 1, d), lambda i, j: (i // per, 0, 0)),
            pl.BlockSpec((1, d), lambda i, j: (0, 0)),
            pl.BlockSpec((None, tn, d), lambda i, j: (layer, j, 0)),
            pl.BlockSpec((None, HEAD, d), lambda i, j: (layer, MAIN_BLOCKS, 0)),
        ],
        out_specs=[
            pl.BlockSpec((tn // HEAD, tm, HEAD), lambda i, j: (j, i, 0)),
            pl.BlockSpec((tm, HEAD), lambda i, j: (i, 0)),
        ],
        out_shape=[
            jax.ShapeDtypeStruct((n // HEAD, r, HEAD), F32),
            jax.ShapeDtypeStruct((r, HEAD), F32),
        ],
        scratch_shapes=[pltpu.VMEM((tm, d), BF16), pltpu.VMEM((1, d), F32)],
        compiler_params=_params(("arbitrary", "arbitrary")),
        name="in_projection",
    )(x2d, shift, scale, gain, w_t, w_t)


def _outproj_kernel(a_ref, b_ref, c_ref, w_ref, x_ref, gate_ref, g_ref, o_ref, y_s):
    i = pl.program_id(0)
    tm = x_ref.shape[0]

    @pl.when(i == 0)
    def _():
        y_s[1] = jnp.zeros(y_s.shape[1:], F32)

    def step(cur, prev):
        for r in range(tm // EPILOGUE_ROWS):
            rows = slice(r * EPILOGUE_ROWS, (r + 1) * EPILOGUE_ROWS)
            y = y_s[prev, rows, :]
            ms = jnp.mean(y * y, axis=-1, keepdims=True)
            o_ref[rows, :] = x_ref[rows, :] + y * (lax.rsqrt(ms + EPS) * (gate_ref[0] * g_ref[...]))
        y_s[cur] = _dot(jnp.concatenate([a_ref[...], b_ref[...], c_ref[...]], axis=1), w_ref[...])

    pl.when(i % 2 == 0)(lambda: step(0, 1))
    pl.when(i % 2 == 1)(lambda: step(1, 0))


def _out_projection(a, b, c, w_out, layer, x2d, gate, gain, rows_per_mod, tm):
    r, d = x2d.shape
    assert r % tm == 0 and rows_per_mod % tm == 0
    per = rows_per_mod // tm
    n = r // tm
    cur = lambda i: (jnp.minimum(i, n - 1), 0)
    prev = lambda i: (jnp.maximum(i - 1, 0), 0)
    return pl.pallas_call(
        _outproj_kernel,
        grid=(n + 1,),
        in_specs=[
            pl.BlockSpec((tm, a.shape[1]), cur),
            pl.BlockSpec((tm, b.shape[1]), cur),
            pl.BlockSpec((tm, c.shape[1]), cur),
            pl.BlockSpec((None,) + w_out.shape[1:], lambda i: (layer, 0, 0)),
            pl.BlockSpec((tm, d), prev),
            pl.BlockSpec((1, 1, d), lambda i: (jnp.maximum(i - 1, 0) // per, 0, 0)),
            pl.BlockSpec((1, d), lambda i: (0, 0)),
        ],
        out_specs=pl.BlockSpec((tm, d), prev),
        out_shape=jax.ShapeDtypeStruct((r, d), F32),
        scratch_shapes=[pltpu.VMEM((2, tm, d), F32)],
        compiler_params=_params(("arbitrary",)),
        name="out_projection",
    )(a, b, c, w_out, x2d, gate, gain)


N_LEVELS = 7


def _scan_matrices(reverse):
    n = CHUNK
    t = np.arange(n)[:, None]
    u = np.arange(n)[None, :]
    cum = (u >= t) if reverse else (u <= t)
    masks = [t == u]
    size = 2
    while size <= n:
        half = size // 2
        same = (t // size) == (u // size)
        if reverse:
            masks.append(same & ((t % size) < half) & ((u % size) >= half))
        else:
            masks.append(same & ((t % size) >= half) & ((u % size) < half))
        size *= 2
    return cum.astype(np.float32), np.stack(masks).astype(np.float32)


def _scan_constants():
    cf, mf = _scan_matrices(False)
    cb, mb = _scan_matrices(True)
    return (jnp.asarray(np.stack([cf, cb]), BF16), jnp.asarray(np.stack([mf, mb]), F32))


def _largest_divisor(n, candidates):
    return next(c for c in candidates if n % c == 0)


def _scan_row(j, reverse, n_ctx_chunks, n_chunks):
    if not reverse:
        return j * CHUNK
    ctx_part = (n_ctx_chunks - 1 - j) * CHUNK
    lat_part = (n_ctx_chunks + (n_chunks - 1 - j)) * CHUNK
    return jnp.where(j < n_ctx_chunks, ctx_part, lat_part)


def _hgrn2_kernel(lbl_ref, lmat_ref, mk_ref,
                  qc_ref, ffc_ref, fbc_ref, ic_ref, gc_ref,
                  ql_ref, ffl_ref, fbl_ref, il_ref, gl_ref,
                  oc_ref, ol_ref,
                  qd_s, y_s, u_s, dec_s, st_s, *, layer):
    n_ctx = qc_ref.shape[1]
    n_lat = ql_ref.shape[1]
    n_ctx_chunks = n_ctx // CHUNK
    n_chunks = (n_ctx + n_lat) // CHUNK

    logits = lbl_ref[...]
    e = jnp.exp(logits - jnp.max(logits, axis=0, keepdims=True))
    sm = e / jnp.sum(e, axis=0, keepdims=True)
    lbs = jnp.sum(sm[:layer + 1], axis=0) - sm[0]

    row = lax.broadcasted_iota(jnp.int32, (CHUNK, 1), 0)
    r4 = row % 4

    def level_factor(size, d, f, cum):
        half = size // 2
        if size == 2:
            return jnp.where((row % 2 == 1) if d == 0 else (row % 2 == 0), f, 1.0)
        if size == 4:
            f_next = pltpu.roll(f, CHUNK - 1, 0)
            f_prev = pltpu.roll(f, 1, 0)
            if d == 0:
                return jnp.where(r4 == 0, f_next, jnp.where(r4 == 1, 1.0, jnp.where(r4 == 2, f, f * f_prev)))
            return jnp.where(r4 == 0, f * f_next, jnp.where(r4 == 1, f, jnp.where(r4 == 2, 1.0, f_prev)))
        mids = [b * size + (half - 1 if d == 0 else half) for b in range(CHUNK // size)]
        ref = jnp.concatenate([jnp.broadcast_to(cum[m:m + 1, :], (size, HEAD)) for m in mids], axis=0)
        return jnp.exp2(-jnp.abs(cum - ref))

    key_lane = lax.broadcasted_iota(jnp.int32, (1, CHUNK), 1)
    fine_levels = 4

    def assemble_scores(d, dots):
        groups = []
        for g in range(CHUNK // 8):
            r = slice(8 * g, 8 * g + 8)
            acc = mk_ref[d, 0, r, :] * dots[0][r]
            for lvl in range(1, fine_levels):
                acc = acc + mk_ref[d, lvl, r, :] * dots[lvl][r]
            for lvl in range(fine_levels, N_LEVELS):
                size = 2 ** lvl
                half = size // 2
                base = (8 * g // size) * size
                on_query_side = (8 * g - base >= half) if d == 0 else (8 * g - base < half)
                if on_query_side:
                    lo = base if d == 0 else base + half
                    acc = jnp.where((key_lane >= lo) & (key_lane < lo + half), dots[lvl][r], acc)
            groups.append(acc)
        return jnp.concatenate(groups, axis=0)

    def phase_a(q_ref, f_refs, i_ref, off, n):
        par = _largest_divisor(n // CHUNK, (16, 8, 4, 2, 1))

        def body(c, carry):
            chains = []
            for u in range(par):
                ci = c * par + u
                rows = pl.ds(pl.multiple_of(ci * CHUNK, CHUNK), CHUNK)
                srows = pl.ds(pl.multiple_of(off + ci * CHUNK, CHUNK), CHUNK)
                q = _silu(q_ref[0, rows, :])
                v = i_ref[0, rows, :].astype(BF16)
                for d in range(2):
                    lb = lbs[d]
                    open_part = (1.0 - lb) * jax.nn.sigmoid(f_refs[d][0, rows, :])
                    f = lb + open_part
                    k = (1.0 - lb) - open_part
                    logf = jnp.log(f) * LOG2E
                    res = _dot(lmat_ref[d], jnp.concatenate(_split2(logf), axis=1))
                    chains.append(dict(d=d, q=q, v=v, k=k, f=f, res=res, srows=srows,
                                       chunk=off // CHUNK + ci))
            for ch in chains:
                d, q, k = ch["d"], ch["q"], ch["k"]
                cum = ch["res"][:, :HEAD] + ch["res"][:, HEAD:]
                total = cum[CHUNK - 1:CHUNK, :] if d == 0 else cum[0:1, :]
                qb, kb = q.astype(BF16), k.astype(BF16)
                dots = [_dot_nt(qb, kb)]
                for lvl in range(1, N_LEVELS):
                    xb = level_factor(2 ** lvl, d, ch["f"], cum).astype(BF16)
                    dots.append(_dot_nt(qb * xb, kb * xb))
                ch["dots"] = dots
                qd_s[d, ch["srows"], :] = (q * jnp.exp2(cum)).astype(BF16)
                ch["u"] = _dot_tn(ch["v"], (k * jnp.exp2(total - cum)).astype(BF16))
                drow = pl.multiple_of(ch["chunk"] * 8, 8)
                dec_s[d, pl.ds(drow, 8), :] = jnp.broadcast_to(jnp.exp2(total), (8, HEAD))
            for ch in chains:
                d = ch["d"]
                ch["y"] = _dot(assemble_scores(d, ch["dots"]).astype(BF16), ch["v"])
                u_s[d, pl.ds(pl.multiple_of(ch["chunk"] * HEAD, HEAD), HEAD), :] = ch["u"]
            for ch in chains:
                y_s[ch["d"], ch["srows"], :] = ch["y"]
            return carry
        lax.fori_loop(0, n // CHUNK // par, body, 0)

    phase_a(qc_ref, (ffc_ref, fbc_ref), ic_ref, 0, n_ctx)
    phase_a(ql_ref, (ffl_ref, fbl_ref), il_ref, n_ctx, n_lat)

    st_s[...] = jnp.zeros_like(st_s)

    steps = _largest_divisor(n_chunks, (17, 16, 8, 4, 2, 1))

    def phase_b(jj, carry):
        states = [st_s[0], st_s[1]]
        for u in range(steps):
            j = jj * steps + u
            for d in range(2):
                chunk = _scan_row(j, d == 1, n_ctx_chunks, n_chunks) // CHUNK
                slot = pl.ds(pl.multiple_of(chunk * HEAD, HEAD), HEAD)
                dec = dec_s[d, pl.ds(pl.multiple_of(chunk * 8, 8), 1), :]
                increment = u_s[d, slot, :]
                u_s[d, slot, :] = states[d]
                states[d] = states[d] * dec + increment
        st_s[0] = states[0]
        st_s[1] = states[1]
        return carry

    lax.fori_loop(0, n_chunks // steps, phase_b, 0)

    def readout(gate_ref, o_ref, off, n):
        par = _largest_divisor(n // CHUNK, (16, 8, 4, 2, 1))

        def body(c, carry):
            chunks = []
            for u in range(par):
                ci = c * par + u
                rows = pl.ds(pl.multiple_of(ci * CHUNK, CHUNK), CHUNK)
                srows = pl.ds(pl.multiple_of(off + ci * CHUNK, CHUNK), CHUNK)
                slot = pl.ds(pl.multiple_of((off // CHUNK + ci) * HEAD, HEAD), HEAD)
                inter = [_dot_nt(qd_s[d, srows, :], u_s[d, slot, :].astype(BF16)) for d in range(2)]
                chunks.append((rows, srows, inter))
            for rows, srows, inter in chunks:
                y = (y_s[0, srows, :] + inter[0]) + (y_s[1, srows, :] + inter[1])
                ms = jnp.mean(y * y, axis=-1, keepdims=True)
                o_ref[0, rows, :] = (y * lax.rsqrt(ms + EPS) * _silu(gate_ref[0, rows, :])).astype(o_ref.dtype)
            return carry
        lax.fori_loop(0, n // CHUNK // par, body, 0)

    readout(gc_ref, oc_ref, 0, n_ctx)
    readout(gl_ref, ol_ref, n_ctx, n_lat)


def _hgrn2(p_ctx, p_lat, lb_logits, consts, layer):
    _, bsz, n_ctx, _ = p_ctx.shape
    n_lat = p_lat.shape[2]
    t = n_ctx + n_lat
    lmat, mk = consts
    depth = lb_logits.shape[0]
    lbl = lb_logits.reshape(depth, 2, A_HEADS, 1, HEAD).transpose(0, 1, 3, 2, 4).reshape(depth, 2, 1, A_HEADS * HEAD)

    def col(n, c0):
        return pl.BlockSpec((None, 1, n, HEAD), lambda b, h, c0=c0: (c0 + h, b, 0, 0))

    cols = (COL_A_Q, COL_A_FF, COL_A_FB, COL_A_I, COL_A_GATE)
    return pl.pallas_call(
        functools.partial(_hgrn2_kernel, layer=layer),
        grid=(bsz, A_HEADS),
        in_specs=[
            pl.BlockSpec((depth, 2, 1, HEAD), lambda b, h: (0, 0, 0, h)),
            pl.BlockSpec(lmat.shape, lambda b, h: (0, 0, 0)),
            pl.BlockSpec(mk.shape, lambda b, h: (0, 0, 0, 0)),
        ] + [col(n_ctx, c) for c in cols] + [col(n_lat, c) for c in cols],
        out_specs=[
            pl.BlockSpec((1, n_ctx, HEAD), lambda b, h: (b, 0, h)),
            pl.BlockSpec((1, n_lat, HEAD), lambda b, h: (b, 0, h)),
        ],
        out_shape=[
            jax.ShapeDtypeStruct((bsz, n_ctx, A_HEADS * HEAD), BF16),
            jax.ShapeDtypeStruct((bsz, n_lat, A_HEADS * HEAD), BF16),
        ],
        scratch_shapes=[
            pltpu.VMEM((2, t, HEAD), BF16),
            pltpu.VMEM((2, t, HEAD), F32),
            pltpu.VMEM((2, (t // CHUNK) * HEAD, HEAD), F32),
            pltpu.VMEM((2, (t // CHUNK) * 8, HEAD), F32),
            pltpu.VMEM((2, HEAD, HEAD), F32),
        ],
        compiler_params=_params(("arbitrary", "arbitrary")),
        name="hgrn2",
    )(lbl, lmat, mk, *([p_ctx] * 5), *([p_lat] * 5))


def _mlstm_constants():
    n = CHUNK
    a = np.arange(n)[:, None]
    b = np.arange(n)[None, :]
    lt, rhs, causal = [], [], []
    for reverse in (False, True):
        upto = (b >= a) if reverse else (b <= a)
        after = (b < a) if reverse else (b > a)
        lt.append(np.stack([upto, after]))
        m = after.T
        ext = np.concatenate([m, np.zeros((n, HEAD - n)), np.ones((n, HEAD))], axis=1)
        rhs.append(np.concatenate([ext, ext], axis=0))
        causal.append(upto)
    return (jnp.asarray(np.stack(lt), F32), jnp.asarray(np.stack(rhs), BF16),
            jnp.asarray(np.stack(causal), F32))


def _mlstm_kernel(lt_ref, rhs_ref, causal_ref, eye_ref, cw_ref, bias_ref,
                  qc_ref, kc_ref, vc_ref, oc_ref, zc_ref, gc_ref,
                  ql_ref, kl_ref, vl_ref, ol_ref, zl_ref, gl_ref,
                  outc_ref, outl_ref,
                  q_s, s_s, cols_s, u_s, sc_s, st_s):
    h = pl.program_id(1)
    n_ctx = qc_ref.shape[1]
    n_lat = ql_ref.shape[1]
    n_ctx_chunks = n_ctx // CHUNK
    n_chunks = (n_ctx + n_lat) // CHUNK
    row_id = lax.broadcasted_iota(jnp.int32, (CHUNK, 1), 0)
    lane = lax.broadcasted_iota(jnp.int32, (1, HEAD), 1)
    sub8 = lax.broadcasted_iota(jnp.int32, (8, 1), 0)
    gate_row = lax.broadcasted_iota(jnp.int32, (N_GATES, 1), 0)
    eye_c = (row_id == lax.broadcasted_iota(jnp.int32, (1, CHUNK), 1)).astype(F32)
    ones_b = jnp.ones((CHUNK, HEAD), BF16)

    def conv_silu(src, ci, n, w):
        s = pl.multiple_of(ci * CHUNK, CHUNK)
        cur = src[0, pl.ds(s, CHUNK), :]
        up = src[0, pl.ds(pl.multiple_of(jnp.maximum(s - 8, 0), 8), 8), :]
        dn = src[0, pl.ds(pl.multiple_of(jnp.minimum(s + CHUNK, n - 8), 8), 8), :]
        prev_row = jnp.where(s > 0, up[7:8, :], 0.0)
        next_row = jnp.where(s + CHUNK < n, dn[0:1, :], 0.0)
        prev = jnp.where(row_id == 0, prev_row, pltpu.roll(cur, 1, 0))
        nxt = jnp.where(row_id == CHUNK - 1, next_row, pltpu.roll(cur, CHUNK - 1, 0))
        return _silu(w[0:1, :] * prev + w[1:2, :] * cur + w[2:3, :] * nxt)

    def pick_row(a, idx):
        return jnp.sum(jnp.where(gate_row == idx, a, 0.0), axis=0, keepdims=True)

    def phase_a(q_ref, k_ref, v_ref, g_ref, off, n):
        par = _largest_divisor(n // CHUNK, (8, 4, 2, 1))

        def body(c, carry):
            chunks = []
            for u in range(par):
                ci = c * par + u
                rows = pl.ds(pl.multiple_of(ci * CHUNK, CHUNK), CHUNK)
                srows = pl.ds(pl.multiple_of(off + ci * CHUNK, CHUNK), CHUNK)
                gt = _dot3_nt(eye_ref[...], g_ref[0, rows, :]) + bias_ref[:, 0:CHUNK]
                qb = (conv_silu(q_ref, ci, n, cw_ref[0]) * (HEAD ** -0.5)).astype(BF16)
                k = conv_silu(k_ref, ci, n, cw_ref[1])
                q_s[srows, :] = qb
                chunks.append(dict(k=k, qk=_dot_nt(qb, k.astype(BF16)), v=v_ref[0, rows, :].astype(BF16),
                                   gt=gt, srows=srows, chunk=off // CHUNK + ci))
            chains = []
            for ck in chunks:
                for d in range(2):
                    gi = 2 * B_HEADS * d + h
                    li_row = pick_row(ck["gt"], gi) * LOG2E
                    lf_row = _log_sigmoid(pick_row(ck["gt"], gi + B_HEADS)) * LOG2E
                    lhs = jnp.concatenate([lt_ref[d, 0] * lf_row,
                                           lt_ref[d, 1] * lf_row + eye_c * li_row], axis=0)
                    out = _dot(jnp.concatenate(_split2(lhs), axis=1), rhs_ref[d])
                    chains.append(dict(ck, d=d, li_row=li_row, out=out))
            for ch in chains:
                d, out = ch["d"], ch["out"]
                cum_rep = out[0:CHUNK, HEAD:]
                logw_rep = out[CHUNK:, HEAD:]
                log_d = jnp.where(causal_ref[d] > 0.0, out[0:CHUNK, 0:CHUNK] + ch["li_row"], NEG)
                a_loc = jnp.max(log_d, axis=1, keepdims=True)
                s_s[d, ch["srows"], :] = (ch["qk"] * jnp.exp2(log_d - a_loc)).astype(BF16)
                cols_s[d, ch["srows"], :] = jnp.where(lane < CHUNK, a_loc, cum_rep)
                total = cum_rep[CHUNK - 1:CHUNK, :] if d == 0 else cum_rep[0:1, :]
                m_loc = jnp.max(logw_rep, axis=0, keepdims=True)
                kw = ch["k"] * jnp.exp2(logw_rep - m_loc)
                ch["u"] = _dot_tn(kw.astype(BF16), ch["v"])
                key_sum = jnp.sum(kw, axis=0, keepdims=True)
                sc_s[d, pl.ds(pl.multiple_of(ch["chunk"] * 8, 8), 8), :] = jnp.where(
                    sub8 == 0, total, jnp.where(sub8 == 1, m_loc, jnp.where(sub8 == 3, key_sum, 0.0)))
            for ch in chains:
                u_s[ch["d"], pl.ds(pl.multiple_of(ch["chunk"] * HEAD, HEAD), HEAD), :] = ch["u"].astype(BF16)
            return carry
        lax.fori_loop(0, n // CHUNK // par, body, 0)

    phase_a(qc_ref, kc_ref, vc_ref, gc_ref, 0, n_ctx)
    phase_a(ql_ref, kl_ref, vl_ref, gl_ref, n_ctx, n_lat)

    st_s[...] = jnp.zeros_like(st_s)
    steps = _largest_divisor(n_chunks, (17, 16, 8, 4, 2, 1))
    for d in range(2):
        def phase_b(jj, carry, d=d):
            m, nvec = carry
            for u in range(steps):
                chunk = _scan_row(jj * steps + u, d == 1, n_ctx_chunks, n_chunks) // CHUNK
                srow = pl.ds(pl.multiple_of(chunk * 8, 8), 8)
                urow = pl.ds(pl.multiple_of(chunk * HEAD, HEAD), HEAD)
                sc = sc_s[d, srow, :]
                total, m_loc, key_sum = sc[0:1, :], sc[1:2, :], sc[3:4, :]
                m_new = jnp.maximum(total + m, m_loc)
                state = st_s[d]
                inc = u_s[d, urow, :].astype(F32)
                u_s[d, urow, :] = state.astype(BF16)
                sc_s[d, srow, :] = jnp.where(sub8 == 2, m, jnp.where(sub8 == 3, nvec, sc))
                keep = jnp.exp2(total + m - m_new)
                gain = jnp.exp2(m_loc - m_new)
                st_s[d] = keep * state + gain * inc
                nvec = keep * nvec + gain * key_sum
                m = m_new
            return m, nvec
        zero_row = jnp.zeros((1, HEAD), F32)
        lax.fori_loop(0, n_chunks // steps, phase_b, (zero_row, zero_row))

    def phase_c(v_ref, o_ref, z_ref, out_ref, off, n):
        par = _largest_divisor(n // CHUNK, (8, 4, 2, 1))

        def body(c, carry):
            chains = []
            for u in range(par):
                ci = c * par + u
                rows = pl.ds(pl.multiple_of(ci * CHUNK, CHUNK), CHUNK)
                srows = pl.ds(pl.multiple_of(off + ci * CHUNK, CHUNK), CHUNK)
                chunk = off // CHUNK + ci
                qb = q_s[srows, :]
                vaug = jnp.concatenate([v_ref[0, rows, :].astype(BF16), ones_b], axis=1)
                for d in range(2):
                    sc = sc_s[d, pl.ds(pl.multiple_of(chunk * 8, 8), 8), :]
                    m, nvec = sc[2:3, :], sc[3:4, :]
                    intra = _dot(s_s[d, srows, :], vaug)
                    inter = _dot(qb, u_s[d, pl.ds(pl.multiple_of(chunk * HEAD, HEAD), HEAD), :])
                    qn = _dot_nt(qb, jnp.broadcast_to(nvec, (HEAD, HEAD)).astype(BF16))
                    chains.append(dict(d=d, u=u, rows=rows, srows=srows, m=m, intra=intra, inter=inter, qn=qn))
            ys = [None] * par
            for ch in chains:
                d = ch["d"]
                cols = cols_s[d, ch["srows"], :]
                swapped = pltpu.roll(cols, CHUNK, 1)
                a_rep = jnp.where(lane < CHUNK, cols, swapped)
                cum_rep = jnp.where(lane < CHUNK, swapped, cols)
                inter_l = cum_rep + ch["m"]
                m_t = jnp.maximum(a_rep, inter_l)
                wi = jnp.exp2(a_rep - m_t)
                we = jnp.exp2(inter_l - m_t)
                num = wi * ch["intra"][:, :HEAD] + we * ch["inter"]
                den = wi * ch["intra"][:, HEAD:] + we * ch["qn"]
                hd = num / jnp.maximum(jnp.abs(den), jnp.exp2(-m_t))
                ys[ch["u"]] = hd if ys[ch["u"]] is None else ys[ch["u"]] + hd
                if d == 1:
                    rows = ch["rows"]
                    hh = jax.nn.sigmoid(o_ref[0, rows, :]) * ys[ch["u"]]
                    ms = jnp.mean(hh * hh, axis=-1, keepdims=True)
                    out_ref[0, rows, :] = (hh * lax.rsqrt(ms + EPS) * _silu(z_ref[0, rows, :])).astype(out_ref.dtype)
            return carry
        lax.fori_loop(0, n // CHUNK // par, body, 0)

    phase_c(vc_ref, oc_ref, zc_ref, outc_ref, 0, n_ctx)
    phase_c(vl_ref, ol_ref, zl_ref, outl_ref, n_ctx, n_lat)


def _mlstm(p_ctx, p_lat, g_ctx, g_lat, conv_w, gate_bias, consts):
    _, bsz, n_ctx, _ = p_ctx.shape
    n_lat = p_lat.shape[2]
    t = n_ctx + n_lat
    n_chunks = t // CHUNK
    lt, rhs, causal = consts
    eye = jnp.eye(N_GATES, HEAD, dtype=BF16)
    cw = conv_w.reshape(3, 2, B_HEADS * HEAD).transpose(1, 0, 2)
    bias = jnp.broadcast_to(gate_bias.reshape(N_GATES, 1), (N_GATES, HEAD))

    def col(n, c0):
        return pl.BlockSpec((None, 1, n, HEAD), lambda b, h, c0=c0: (c0 + h, b, 0, 0))

    def gate_spec(n):
        return pl.BlockSpec((1, n, HEAD), lambda b, h: (b, 0, 0))

    cols = (COL_B_Q, COL_B_K, COL_B_V, COL_B_O, COL_B_Z)
    return pl.pallas_call(
        _mlstm_kernel,
        grid=(bsz, B_HEADS),
        in_specs=[
            pl.BlockSpec(lt.shape, lambda b, h: (0, 0, 0, 0)),
            pl.BlockSpec(rhs.shape, lambda b, h: (0, 0, 0)),
            pl.BlockSpec(causal.shape, lambda b, h: (0, 0, 0)),
            pl.BlockSpec(eye.shape, lambda b, h: (0, 0)),
            pl.BlockSpec((2, 3, HEAD), lambda b, h: (0, 0, h)),
            pl.BlockSpec((N_GATES, HEAD), lambda b, h: (0, 0)),
        ] + [col(n_ctx, c) for c in cols] + [gate_spec(n_ctx)]
          + [col(n_lat, c) for c in cols] + [gate_spec(n_lat)],
        out_specs=[
            pl.BlockSpec((1, n_ctx, HEAD), lambda b, h: (b, 0, h)),
            pl.BlockSpec((1, n_lat, HEAD), lambda b, h: (b, 0, h)),
        ],
        out_shape=[
            jax.ShapeDtypeStruct((bsz, n_ctx, B_HEADS * HEAD), BF16),
            jax.ShapeDtypeStruct((bsz, n_lat, B_HEADS * HEAD), BF16),
        ],
        scratch_shapes=[
            pltpu.VMEM((t, HEAD), BF16),
            pltpu.VMEM((2, t, CHUNK), BF16),
            pltpu.VMEM((2, t, HEAD), F32),
            pltpu.VMEM((2, n_chunks * HEAD, HEAD), BF16),
            pltpu.VMEM((2, n_chunks * 8, HEAD), F32),
            pltpu.VMEM((2, HEAD, HEAD), F32),
        ],
        compiler_params=_params(("arbitrary", "arbitrary")),
        name="mlstm",
    )(lt, rhs, causal, eye, cw, bias, *([p_ctx] * 5), g_ctx, *([p_lat] * 5), g_lat)


def _rope(x, cos, sin_lo, sin_hi):
    return x * cos + pltpu.roll(x, HEAD - 32, 1) * sin_lo + pltpu.roll(x, 32, 1) * sin_hi


def _softmax_attend(sink_col, scores, values):
    m = jnp.maximum(sink_col, jnp.max(scores, axis=1, keepdims=True))
    p = jnp.exp2(scores - m)
    den = jnp.exp2(sink_col - m) + jnp.sum(p, axis=1, keepdims=True)
    return _dot(p.astype(BF16), values) / den


def _sink_column(sink_ref, kv, rows_per_head):
    row = lax.broadcasted_iota(jnp.int32, (C_GROUP * rows_per_head, 1), 0)
    col = jnp.zeros((C_GROUP * rows_per_head, 1), F32)
    for g in range(C_GROUP):
        col = jnp.where(row // rows_per_head == g, sink_ref[kv * C_GROUP + g] * LOG2E, col)
    return col


def _window_masks():
    r = np.arange(ATTN_BLOCK)[:, None]
    c = np.arange(3 * ATTN_BLOCK)[None, :]
    return jnp.asarray(np.stack([np.abs(c - p * ATTN_BLOCK - r) <= WINDOW for p in range(3)]), F32)


def _attn_kernel(sink_ref, mask_ref, cos_ref, slo_ref, shi_ref, q_ref, k_ref, v_ref, kc_ref, vc_ref, z_ref,
                 o_ref, kr_s):
    kv = pl.program_id(1)
    i = pl.program_id(2)
    n = k_ref.shape[1]
    span = 3 * ATTN_BLOCK

    @pl.when(i == 0)
    def _():
        group = _largest_divisor(n // ATTN_BLOCK, (4, 2, 1))

        def body(c, carry):
            for u in range(group):
                rows = pl.ds(pl.multiple_of((c * group + u) * ATTN_BLOCK, ATTN_BLOCK), ATTN_BLOCK)
                kr_s[rows, :] = _rope(k_ref[0, rows, :], cos_ref[rows, :], slo_ref[rows, :],
                                      shi_ref[rows, :]).astype(BF16)
            return carry
        lax.fori_loop(0, n // ATTN_BLOCK // group, body, 0)

    scale = HEAD ** -0.5 * LOG2E
    blocks_per_step = q_ref.shape[1] // ATTN_BLOCK

    par = _largest_divisor(blocks_per_step, (4, 2, 1))

    def block_group(trip, carry):
        chains = []
        for u in range(par):
            sub = trip * par + u
            blk = i * blocks_per_step + sub
            local = pl.ds(pl.multiple_of(sub * ATTN_BLOCK, ATTN_BLOCK), ATTN_BLOCK)
            qrows = pl.ds(pl.multiple_of(blk * ATTN_BLOCK, ATTN_BLOCK), ATTN_BLOCK)
            cos = cos_ref[qrows, :]
            slo = slo_ref[qrows, :]
            shi = shi_ref[qrows, :]
            start = pl.multiple_of(jnp.clip((blk - 1) * ATTN_BLOCK, 0, n - span), ATTN_BLOCK)
            keys = jnp.concatenate([kr_s[pl.ds(start, span), :], kc_ref[0].astype(BF16)], axis=0)
            values = jnp.concatenate([v_ref[0, pl.ds(start, span), :].astype(BF16), vc_ref[0].astype(BF16)],
                                     axis=0)
            valid = mask_ref[(blk * ATTN_BLOCK - start) // ATTN_BLOCK] > 0.0
            for g in range(C_GROUP):
                qg = (_rope(q_ref[g, local, :], cos, slo, shi) * scale).astype(BF16)
                s = _dot_nt(qg, keys)
                s = jnp.concatenate([jnp.where(valid, s[:, :span], NEG), s[:, span:]], axis=1)
                chains.append((g, local, s, values))
        for g, local, s, values in chains:
            sink = jnp.full((ATTN_BLOCK, 1), sink_ref[kv * C_GROUP + g] * LOG2E, F32)
            out = _softmax_attend(sink, s, values)
            o_ref[0, local, g * HEAD:(g + 1) * HEAD] = (out * _silu(z_ref[g, local, :])).astype(o_ref.dtype)
        return carry

    lax.fori_loop(0, blocks_per_step // par, block_group, 0)


def _attention(p_ctx, p_lat, sink, rope):
    _, bsz, n_ctx, _ = p_ctx.shape
    n = p_lat.shape[2]
    cos, slo, shi = rope
    masks = _window_masks()
    qrows = ATTN_BLOCK * _largest_divisor(n // ATTN_BLOCK, (8, 4, 2, 1))
    gw = C_GROUP * HEAD
    gblk = C_GROUP

    full = pl.BlockSpec((n, HEAD), lambda b, kv, i: (0, 0))
    return pl.pallas_call(
        _attn_kernel,
        grid=(bsz, C_KV_HEADS, n // qrows),
        in_specs=[
            pl.BlockSpec(memory_space=pltpu.SMEM),
            pl.BlockSpec(masks.shape, lambda b, kv, i: (0, 0, 0)),
            full, full, full,
            pl.BlockSpec((C_GROUP, None, qrows, HEAD), lambda b, kv, i: (COL_C_Q // gblk + kv, b, i, 0)),
            pl.BlockSpec((None, 1, n, HEAD), lambda b, kv, i: (COL_C_K + kv, b, 0, 0)),
            pl.BlockSpec((None, 1, n, HEAD), lambda b, kv, i: (COL_C_V + kv, b, 0, 0)),
            pl.BlockSpec((None, 1, n_ctx, HEAD), lambda b, kv, i: (COL_C_K + kv, b, 0, 0)),
            pl.BlockSpec((None, 1, n_ctx, HEAD), lambda b, kv, i: (COL_C_V + kv, b, 0, 0)),
            pl.BlockSpec((C_GROUP, None, qrows, HEAD), lambda b, kv, i: (COL_C_Z // gblk + kv, b, i, 0)),
        ],
        out_specs=pl.BlockSpec((1, qrows, gw), lambda b, kv, i: (b, i, kv)),
        out_shape=jax.ShapeDtypeStruct((bsz, n, C_Q_HEADS * HEAD), BF16),
        scratch_shapes=[pltpu.VMEM((n, HEAD), BF16)],
        compiler_params=_params(("arbitrary", "arbitrary", "arbitrary")),
        name="window_attention",
    )(sink, masks, cos, slo, shi, p_lat, p_lat, p_lat, p_ctx, p_ctx, p_lat)


def _ctx_attn_kernel(sink_ref, q_ref, k_ref, v_ref, z_ref, o_ref):
    kv = pl.program_id(1)
    n_ctx = q_ref.shape[1]
    scale = HEAD ** -0.5 * LOG2E
    q4 = jnp.concatenate([q_ref[g] * scale for g in range(C_GROUP)],
                         axis=0).astype(BF16)
    s = _dot_nt(q4, k_ref[0].astype(BF16))
    out = _softmax_attend(_sink_column(sink_ref, kv, n_ctx), s, v_ref[0].astype(BF16))
    for g in range(C_GROUP):
        zg = z_ref[g]
        o_ref[0, :, g * HEAD:(g + 1) * HEAD] = (out[g * n_ctx:(g + 1) * n_ctx] * _silu(zg)).astype(o_ref.dtype)


def _context_attention(p_ctx, sink):
    _, bsz, n_ctx, _ = p_ctx.shape
    gw = C_GROUP * HEAD
    gblk = C_GROUP
    return pl.pallas_call(
        _ctx_attn_kernel,
        grid=(bsz, C_KV_HEADS),
        in_specs=[
            pl.BlockSpec(memory_space=pltpu.SMEM),
            pl.BlockSpec((C_GROUP, None, n_ctx, HEAD), lambda b, kv: (COL_C_Q // gblk + kv, b, 0, 0)),
            pl.BlockSpec((None, 1, n_ctx, HEAD), lambda b, kv: (COL_C_K + kv, b, 0, 0)),
            pl.BlockSpec((None, 1, n_ctx, HEAD), lambda b, kv: (COL_C_V + kv, b, 0, 0)),
            pl.BlockSpec((C_GROUP, None, n_ctx, HEAD), lambda b, kv: (COL_C_Z // gblk + kv, b, 0, 0)),
        ],
        out_specs=pl.BlockSpec((1, n_ctx, gw), lambda b, kv: (b, 0, kv)),
        out_shape=jax.ShapeDtypeStruct((bsz, n_ctx, C_Q_HEADS * HEAD), BF16),
        compiler_params=_params(("arbitrary", "arbitrary")),
        name="context_attention",
    )(sink, p_ctx, p_ctx, p_ctx, p_ctx)


def _rope_tables(n):
    axis_dim = HEAD // 2
    rows = n // GRID_W
    inv_freq = ROPE_BASE ** (-jnp.arange(0, axis_dim, 2, dtype=F32) / axis_dim)
    ang_r = jnp.arange(rows, dtype=F32)[:, None] * inv_freq[None, :]
    ang_c = jnp.arange(GRID_W, dtype=F32)[:, None] * inv_freq[None, :]
    by_row = lambda a: jnp.repeat(a, GRID_W, axis=0)
    by_col = lambda a: jnp.tile(a, (rows, 1))
    cos_r, sin_r = by_row(jnp.cos(ang_r)), by_row(jnp.sin(ang_r))
    cos_c, sin_c = by_col(jnp.cos(ang_c)), by_col(jnp.sin(ang_c))
    zero = jnp.zeros_like(cos_r)
    cos = jnp.concatenate([cos_r, cos_r, cos_c, cos_c], axis=-1)
    sin_lo = jnp.concatenate([-sin_r, zero, -sin_c, zero], axis=-1)
    sin_hi = jnp.concatenate([zero, sin_r, zero, sin_c], axis=-1)
    return cos, sin_lo, sin_hi


def _pick_tile(n, candidates):
    for c in candidates:
        if n % c == 0:
            return c
    raise ValueError(f"no tile for {n}")


def kernel(x, c, ctx, c_ctx, w_mod, b_mod, g_pre, g_post, w_in, hgrn_lb_logits, mlstm_conv_w,
           mlstm_gate_bias, attn_sink, w_out):
    bsz, n_lat, d = x.shape
    n_ctx = ctx.shape[1]
    depth = w_mod.shape[0]
    assert n_ctx % CHUNK == 0 and n_lat % ATTN_BLOCK == 0 and n_lat >= 3 * ATTN_BLOCK

    cond = jnp.concatenate([c, c_ctx[None, :]], axis=0)
    pad = (-cond.shape[0]) % 8
    cond = jnp.pad(cond, ((0, pad), (0, 0)))
    mod = _modulation(cond, w_mod, b_mod)

    w_t = _weight_prep(jnp.swapaxes(w_in, 1, 2))
    wo = w_out.astype(BF16)
    consts = _scan_constants()
    mlstm_consts = _mlstm_constants()
    rope = _rope_tables(n_lat)

    tm_lat = _pick_tile(n_lat, (1024, 512, 256, 128))
    tm_ctx = _pick_tile(bsz * n_ctx, (1024, 512, 256, 128))
    tn = _pick_tile(MAIN_WIDTH, (1536, 512))
    to_lat = _pick_tile(n_lat, (512, 256, 128))
    to_ctx = _pick_tile(bsz * n_ctx, (512, 256, 128))

    x2d = x.reshape(bsz * n_lat, d)
    h2d = ctx.reshape(bsz * n_ctx, d)
    for layer in range(depth):
        with_ctx = layer < depth - 1
        shift_l = mod[layer, :bsz, 0:d].reshape(bsz, 1, d)
        scale_l = mod[layer, :bsz, d:2 * d].reshape(bsz, 1, d)
        gate_l = mod[layer, :bsz, 2 * d:].reshape(bsz, 1, d)
        shift_c = mod[layer, bsz:bsz + 1, 0:d].reshape(1, 1, d)
        scale_c = mod[layer, bsz:bsz + 1, d:2 * d].reshape(1, 1, d)
        gate_c = mod[layer, bsz:bsz + 1, 2 * d:].reshape(1, 1, d)

        gain_pre = g_pre[layer].reshape(1, d)
        gain_post = g_post[layer].reshape(1, d)

        p_lat, g_lat = _in_projection(x2d, shift_l, scale_l, gain_pre, w_t, layer, n_lat, tm_lat, tn)
        p_ctx, g_ctx = _in_projection(h2d, shift_c, scale_c, gain_pre, w_t, layer, bsz * n_ctx, tm_ctx, tn)
        p_lat = p_lat.reshape(MAIN_BLOCKS, bsz, n_lat, HEAD)
        p_ctx = p_ctx.reshape(MAIN_BLOCKS, bsz, n_ctx, HEAD)
        g_lat = g_lat.reshape(bsz, n_lat, HEAD)
        g_ctx = g_ctx.reshape(bsz, n_ctx, HEAD)

        a_c, a_l = _hgrn2(p_ctx, p_lat, hgrn_lb_logits, consts, layer)
        b_c, b_l = _mlstm(p_ctx, p_lat, g_ctx, g_lat, mlstm_conv_w[layer], mlstm_gate_bias[layer],
                          mlstm_consts)
        c_l = _attention(p_ctx, p_lat, attn_sink[layer], rope)

        x2d = _out_projection(a_l.reshape(bsz * n_lat, -1), b_l.reshape(bsz * n_lat, -1),
                              c_l.reshape(bsz * n_lat, -1), wo, layer, x2d, gate_l, gain_post, n_lat, to_lat)
        if with_ctx:
            c_c = _context_attention(p_ctx, attn_sink[layer])
            h2d = _out_projection(a_c.reshape(bsz * n_ctx, -1), b_c.reshape(bsz * n_ctx, -1),
                                  c_c.reshape(bsz * n_ctx, -1), wo, layer, h2d, gate_c, gain_post,
                                  bsz * n_ctx, to_ctx)
    return x2d.reshape(bsz, n_lat, d)
```
